```python
import jax, jax.numpy as jnp
from jax import lax
import numpy as np

D_MODEL = 1024
BATCH = 32
SEQ = 256
DEPTH = 4
DEC_BATCH = 4
DEC_SEQ = 1024
PAST_LEN = 512

GRID_W = 64
N_EVEN = (DEPTH + 1) // 2
N_ODD = DEPTH // 2
D_HALF = D_MODEL // 2
HD_A = 64
H_A = D_HALF // HD_A
KV_A = H_A // 4
Q_BLOCK = 128
ROPE_BASE = 10000.0
HS_B = 64
H_B = D_HALF // HS_B
W_LORA = 64
A_LORA = 64
RWKV_DECAY_SCALE = 0.606531
GN_EPS = 64e-5
B_SHIFT = 3 * D_HALF + 2 * W_LORA + 2 * A_LORA
H_C = 4
DK_C = D_MODEL // 2 // H_C
DV_C = D_MODEL // H_C
G_LORA = 16
GLA_TAU = 16.0
GLA_CHUNK = 16
EPS = 1e-6

EV_SPLITS = [H_A * HD_A, KV_A * HD_A, KV_A * HD_A, D_HALF, B_SHIFT, D_HALF]
EV_COLS = sum(EV_SPLITS)
OD_SPLITS = [H_C * DK_C, H_C * DK_C, D_MODEL, D_MODEL, G_LORA, G_LORA]
OD_COLS = sum(OD_SPLITS)

kernel_name = 'hybrid_dit_gqa_rwkv7_gla_step'


def split_cols(z, sizes):
    return jnp.split(z, np.cumsum(sizes)[:-1].tolist(), axis=-1)


def rms_norm(x, g):
    xf = x.astype(jnp.float32)
    y = xf * lax.rsqrt(jnp.mean(xf * xf, axis=-1, keepdims=True) + EPS)
    return (y * g.astype(jnp.float32)).astype(x.dtype)


def rope_2d(x):
    B, T, H, D = x.shape
    n_rows = T // GRID_W
    row = jnp.repeat(jnp.arange(n_rows), GRID_W).astype(jnp.float32)
    col = jnp.tile(jnp.arange(GRID_W), n_rows).astype(jnp.float32)
    n_freq = D // 4
    inv = ROPE_BASE ** (-jnp.arange(n_freq, dtype=jnp.float32) / n_freq)
    ang = jnp.stack([row[:, None] * inv, col[:, None] * inv], axis=1)
    cos = jnp.cos(ang)[None, :, None]
    sin = jnp.sin(ang)[None, :, None]
    xr = x.astype(jnp.float32).reshape(B, T, H, 2, 2, n_freq)
    x1, x2 = xr[..., 0, :], xr[..., 1, :]
    out = jnp.stack([x1 * cos - x2 * sin, x2 * cos + x1 * sin], axis=-2)
    return out.reshape(B, T, H, D).astype(x.dtype)


def block_attention(q, k, v):
    B, T, H, D = q.shape
    kvh = k.shape[2]
    G = H // kvh
    nblk = T // Q_BLOCK
    qb = q.reshape(B, nblk, Q_BLOCK, kvh, G, D).transpose(1, 0, 2, 3, 4, 5)
    scale = D ** -0.5

    def attend(qi):
        s = jnp.einsum('bqkgd,bskd->bkgqs', qi, k).astype(jnp.float32) * scale
        p = jax.nn.softmax(s, axis=-1).astype(v.dtype)
        return jnp.einsum('bkgqs,bskd->bqkgd', p, v)

    o = lax.map(attend, qb)
    return o.transpose(1, 0, 2, 3, 4, 5).reshape(B, T, H * D)


def token_shift(z, mu):
    zp = jnp.pad(z[:, :-1], ((0, 0), (1, 0), (0, 0)))
    zn = jnp.pad(z[:, 1:], ((0, 0), (0, 1), (0, 0)))
    return z + mu * (0.5 * (zp + zn) - z)


def rwkv7_scan(r, w, k, v, kk, a, s0):
    def step(s, inp):
        r_t, w_t, k_t, v_t, kk_t, a_t = inp
        sa = jnp.einsum('bhvk,bhk->bhv', s, -kk_t)
        s = (s * w_t[:, :, None, :] + sa[..., None] * (kk_t * a_t)[:, :, None, :]
             + v_t[..., None] * k_t[:, :, None, :])
        return s, jnp.einsum('bhvk,bhk->bhv', s, r_t)

    xs = tuple(jnp.swapaxes(t, 0, 1) for t in (r, w, k, v, kk, a))
    s, ys = lax.scan(step, s0, xs)
    return jnp.swapaxes(ys, 0, 1), s


def gla_chunked(q, k, v, log_a, s0):
    B, T, H, K = q.shape
    n_chunks = T // GLA_CHUNK

    def to_chunks(t):
        return t.reshape(B, n_chunks, GLA_CHUNK, H, t.shape[-1]).transpose(1, 0, 3, 2, 4)

    qc, kc, vc = to_chunks(q), to_chunks(k), to_chunks(v)
    bc = jnp.cumsum(to_chunks(log_a), axis=3)
    mask = jnp.tril(jnp.ones((GLA_CHUNK, GLA_CHUNK), dtype=bool))

    def step(s, inp):
        q_, k_, v_, b_ = inp
        inter = jnp.einsum('bhck,bhkv->bhcv', q_ * jnp.exp(b_), s)
        diff = b_[:, :, :, None, :] - b_[:, :, None, :, :]
        decay = jnp.exp(jnp.where(mask[:, :, None], diff, -jnp.inf))
        att = jnp.einsum('bhtk,bhsk,bhtsk->bhts', q_, k_, decay)
        o = inter + jnp.einsum('bhts,bhsv->bhtv', att, v_)
        b_last = b_[:, :, -1:, :]
        s = (jnp.exp(b_last[:, :, 0, :])[..., None] * s
             + jnp.einsum('bhsk,bhsv->bhkv', k_ * jnp.exp(b_last - b_), v_))
        return s, o

    s, o = lax.scan(step, s0, (qc, kc, vc, bc))
    return o.transpose(1, 0, 3, 2, 4).reshape(B, T, H, v.shape[-1]), s


def even_mixer(h, w_in, w_out, qn_g, kn_g, shift_mu, w0, w2, a0, a2, k_k, k_a, r_k, ln_g, ln_b, ctx):
    B, T, _ = h.shape
    f32 = jnp.float32
    z = h @ w_in
    qa, ka, va, ga, zb, gb = split_cols(z, EV_SPLITS)
    q = rms_norm(qa.reshape(B, T, H_A, HD_A), qn_g)
    k = rms_norm(ka.reshape(B, T, KV_A, HD_A), kn_g)
    v = va.reshape(B, T, KV_A, HD_A)
    if ctx is None:
        keys, vals = k, v
        zero = jnp.zeros((B, H_B, HS_B, HS_B), f32)
        s_init = (zero, zero)
    else:
        ctx_k, ctx_v, s_f, s_b = ctx
        q = rope_2d(q)
        keys = jnp.concatenate([ctx_k.astype(k.dtype), rope_2d(k)], axis=1)
        vals = jnp.concatenate([ctx_v.astype(v.dtype), v], axis=1)
        s_init = (s_f.astype(f32), s_b.astype(f32))
    o_a = block_attention(q, keys, vals) * jax.nn.silu(ga)
    zb = token_shift(zb, shift_mu).astype(f32)
    rb, kb, vb, wl_f, wl_b, al_f, al_b = split_cols(zb, [D_HALF] * 3 + [W_LORA] * 2 + [A_LORA] * 2)
    heads = lambda t: t.reshape(B, T, H_B, HS_B)
    r, k_raw, vv = heads(rb), heads(kb), heads(vb)
    kk = k_raw * k_k.reshape(H_B, HS_B)
    kk = kk * lax.rsqrt(jnp.sum(kk * kk, axis=-1, keepdims=True) + 1e-12)
    y = jnp.zeros_like(r)
    bonus = jnp.zeros_like(r)
    finals = []
    for d, (wl, al) in enumerate(((wl_f, al_f), (wl_b, al_b))):
        w = heads(jnp.exp(-RWKV_DECAY_SCALE * jax.nn.sigmoid(w0[d] + jnp.tanh(wl) @ w2[d])))
        a = heads(jax.nn.sigmoid(a0[d] + al @ a2[d]))
        kd = k_raw * (1.0 + (a - 1.0) * k_a.reshape(H_B, HS_B))
        seqs = (r, w, kd, vv, kk, a)
        if d == 1:
            seqs = tuple(jnp.flip(t, axis=1) for t in seqs)
        yd, sd = rwkv7_scan(*seqs, s_init[d])
        if d == 1:
            yd = jnp.flip(yd, axis=1)
        y = y + yd
        bonus = bonus + jnp.sum(r * kd * r_k, axis=-1, keepdims=True) * vv
        finals.append(sd)
    mu = jnp.mean(y, axis=-1, keepdims=True)
    var = jnp.mean(jnp.square(y - mu), axis=-1, keepdims=True)
    yn = ((y - mu) * lax.rsqrt(var + GN_EPS)).reshape(B, T, D_HALF) * ln_g + ln_b
    o_b = (yn + bonus.reshape(B, T, D_HALF)).astype(h.dtype) * jax.nn.silu(gb)
    out = jnp.concatenate([o_a, o_b], axis=-1) @ w_out
    return out, (k, v, finals[0], finals[1])


def odd_mixer(h, w_in, w_out, gw2, gbias, ln_g, ctx):
    B, T, _ = h.shape
    f32 = jnp.float32
    z = h @ w_in
    q, k, v, g, gl_f, gl_b = split_cols(z, OD_SPLITS)
    q = q.reshape(B, T, H_C, DK_C).astype(f32) * (DK_C ** -0.5)
    k = k.reshape(B, T, H_C, DK_C).astype(f32)
    v = v.reshape(B, T, H_C, DV_C).astype(f32)
    if ctx is None:
        zero = jnp.zeros((B, H_C, DK_C, DV_C), f32)
        s_init = (zero, zero)
    else:
        s_init = (ctx[0].astype(f32), ctx[1].astype(f32))
    o = jnp.zeros_like(v)
    finals = []
    for d, gl in enumerate((gl_f, gl_b)):
        log_a = jax.nn.log_sigmoid(gl.astype(f32) @ gw2[d] + gbias[d]).reshape(B, T, H_C, DK_C) / GLA_TAU
        seqs = (q, k, v, log_a)
        if d == 1:
            seqs = tuple(jnp.flip(t, axis=1) for t in seqs)
        od, sd = gla_chunked(*seqs, s_init[d])
        if d == 1:
            od = jnp.flip(od, axis=1)
        o = o + od
        finals.append(sd)
    o = rms_norm(o, ln_g).reshape(B, T, D_MODEL).astype(h.dtype) * jax.nn.silu(g)
    return o @ w_out, (finals[0], finals[1])


def setup_inputs(seed: int = 0) -> dict:
    key = jax.random.key(seed)
    ks = iter(jax.random.split(key, 40))
    f32 = jnp.float32
    nrm = lambda shape, scale: scale * jax.random.normal(next(ks), shape, f32)
    return {
        'x_prompt': nrm((BATCH, SEQ, D_MODEL), 1.0),
        'x_sample': nrm((DEC_BATCH, DEC_SEQ, D_MODEL), 1.0),
        'c': nrm((DEC_BATCH, D_MODEL), 1.0),
        'cache_attn_k': nrm((DEC_BATCH, N_EVEN, PAST_LEN, KV_A, HD_A), 1.0),
        'cache_attn_v': nrm((DEC_BATCH, N_EVEN, PAST_LEN, KV_A, HD_A), 1.0),
        'state_rwkv_fwd': nrm((DEC_BATCH, N_EVEN, H_B, HS_B, HS_B), 0.3),
        'state_rwkv_bwd': nrm((DEC_BATCH, N_EVEN, H_B, HS_B, HS_B), 0.3),
        'state_gla_fwd': nrm((DEC_BATCH, N_ODD, H_C, DK_C, DV_C), 0.3),
        'state_gla_bwd': nrm((DEC_BATCH, N_ODD, H_C, DK_C, DV_C), 0.3),
        'c_ctx': nrm((D_MODEL,), 1.0),
        'norm_g': 1.0 + nrm((DEPTH, D_MODEL), 0.05),
        'mod_w': nrm((DEPTH, D_MODEL, 3 * D_MODEL), 0.5 * D_MODEL ** -0.5),
        'mod_b': nrm((DEPTH, 3 * D_MODEL), 0.02),
        'ev_w_in': nrm((N_EVEN, D_MODEL, EV_COLS), D_MODEL ** -0.5),
        'ev_w_out': nrm((N_EVEN, 2 * D_HALF, D_MODEL), (2 * D_HALF) ** -0.5),
        'ev_qn_g': 1.0 + nrm((N_EVEN, HD_A), 0.05),
        'ev_kn_g': 1.0 + nrm((N_EVEN, HD_A), 0.05),
        'ev_shift_mu': jax.random.uniform(next(ks), (N_EVEN, B_SHIFT), f32),
        'rw_w0': nrm((N_EVEN, 2, D_HALF), 0.5),
        'rw_w2': nrm((N_EVEN, 2, W_LORA, D_HALF), 0.5 * W_LORA ** -0.5),
        'rw_a0': nrm((N_EVEN, 2, D_HALF), 0.5),
        'rw_a2': nrm((N_EVEN, 2, A_LORA, D_HALF), 0.5 * A_LORA ** -0.5),
        'rw_kk': 0.85 + nrm((N_EVEN, D_HALF), 0.05),
        'rw_ka': 1.0 + nrm((N_EVEN, D_HALF), 0.05),
        'rw_rk': nrm((N_EVEN, H_B, HS_B), 0.1),
        'rw_ln_g': 1.0 + nrm((N_EVEN, D_HALF), 0.05),
        'rw_ln_b': nrm((N_EVEN, D_HALF), 0.02),
        'od_w_in': nrm((N_ODD, D_MODEL, OD_COLS), D_MODEL ** -0.5),
        'od_w_out': nrm((N_ODD, D_MODEL, D_MODEL), D_MODEL ** -0.5),
        'gla_w2': nrm((N_ODD, 2, G_LORA, H_C * DK_C), G_LORA ** -0.5),
        'gla_b': nrm((N_ODD, 2, H_C * DK_C), 0.5),
        'gla_ln_g': 1.0 + nrm((N_ODD, DV_C), 0.05),
        'final_g': 1.0 + nrm((D_MODEL,), 0.05),
    }


def reference(x_prompt, x_sample, c, cache_attn_k, cache_attn_v, state_rwkv_fwd, state_rwkv_bwd,
              state_gla_fwd, state_gla_bwd, c_ctx, norm_g, mod_w, mod_b, ev_w_in, ev_w_out, ev_qn_g,
              ev_kn_g, ev_shift_mu, rw_w0, rw_w2, rw_a0, rw_a2, rw_kk, rw_ka, rw_rk, rw_ln_g, rw_ln_b,
              od_w_in, od_w_out, gla_w2, gla_b, gla_ln_g, final_g):

    def trunk_layer(x, i, cond, ctx_tensors):
        m = jax.nn.silu(cond.astype(jnp.float32)) @ mod_w[i] + mod_b[i]
        shift, scale, gate = jnp.split(m.reshape(-1, 1, 3 * D_MODEL), 3, axis=-1)
        h = (rms_norm(x, norm_g[i]).astype(jnp.float32) * (1.0 + scale) + shift).astype(x.dtype)
        j = i // 2
        if i % 2 == 0:
            out, new = even_mixer(h, ev_w_in[j], ev_w_out[j], ev_qn_g[j], ev_kn_g[j], ev_shift_mu[j],
                                  rw_w0[j], rw_w2[j], rw_a0[j], rw_a2[j], rw_kk[j], rw_ka[j], rw_rk[j],
                                  rw_ln_g[j], rw_ln_b[j], ctx_tensors)
        else:
            out, new = odd_mixer(h, od_w_in[j], od_w_out[j], gla_w2[j], gla_b[j], gla_ln_g[j], ctx_tensors)
        return x + gate.astype(x.dtype) * out.astype(x.dtype), new

    new_k, new_v, new_rf, new_rb, new_gf, new_gb = [], [], [], [], [], []
    x = x_prompt
    for i in range(DEPTH):
        x, new = trunk_layer(x, i, c_ctx, None)
        if i % 2 == 0:
            new_k.append(new[0]); new_v.append(new[1]); new_rf.append(new[2]); new_rb.append(new[3])
        else:
            new_gf.append(new[0]); new_gb.append(new[1])
    y_prompt = rms_norm(x, final_g)

    x = x_sample
    for i in range(DEPTH):
        j = i // 2
        if i % 2 == 0:
            ctx = (cache_attn_k[:, j], cache_attn_v[:, j], state_rwkv_fwd[:, j], state_rwkv_bwd[:, j])
        else:
            ctx = (state_gla_fwd[:, j], state_gla_bwd[:, j])
        x, _ = trunk_layer(x, i, c, ctx)
    y_sample = rms_norm(x, final_g)

    return (y_prompt, y_sample, jnp.stack(new_k, axis=1), jnp.stack(new_v, axis=1),
            jnp.stack(new_rf, axis=1), jnp.stack(new_rb, axis=1),
            jnp.stack(new_gf, axis=1), jnp.stack(new_gb, axis=1))
```

```python
import functools

import numpy as np
import jax
import jax.numpy as jnp
from jax import lax
from jax.experimental import pallas as pl
from jax.experimental.pallas import tpu as pltpu

F32 = jnp.float32
BF16 = jnp.bfloat16

D_MODEL = 1024
DEPTH = 4
GRID_W = 64
D_HALF = D_MODEL // 2
HD_A = 64
H_A = D_HALF // HD_A
KV_A = H_A // 4
ROPE_BASE = 10000.0
HS_B = 64
H_B = D_HALF // HS_B
W_LORA = 64
A_LORA = 64
RWKV_DECAY_SCALE = 0.606531
GN_EPS = 64e-5
B_SHIFT = 3 * D_HALF + 2 * W_LORA + 2 * A_LORA
H_C = 4
DK_C = D_MODEL // 2 // H_C
DV_C = D_MODEL // H_C
G_LORA = 16
GLA_TAU = 16.0
EPS = 1e-6

EV_SPLITS = (H_A * HD_A, KV_A * HD_A, KV_A * HD_A, D_HALF, B_SHIFT, D_HALF)
OD_SPLITS = (H_C * DK_C, H_C * DK_C, D_MODEL, D_MODEL, 2 * G_LORA)

LANES = 128
SUBLANES = 8
VMEM_LIMIT_BYTES = 56 * 1024 * 1024

ROW_TILE = 256
Q_TILE = 256
GLA_CHUNK = 64
SCAN_T = 8


def _params(*sem):
    return pltpu.CompilerParams(dimension_semantics=sem, vmem_limit_bytes=VMEM_LIMIT_BYTES)


def _silu(x):
    return x * jax.nn.sigmoid(x)


def _dot(a, b):
    return jnp.dot(a, b, preferred_element_type=F32)


def _seg_sum(x, ones_bd):
    hi = x.astype(BF16)
    r1 = x - hi.astype(F32)
    mid = r1.astype(BF16)
    lo = (r1 - mid.astype(F32)).astype(BF16)
    return _dot(hi, ones_bd) + _dot(mid, ones_bd) + _dot(lo, ones_bd)


def _block_diag_ones(n, blk):
    i = np.arange(n) // blk
    return jnp.asarray((i[:, None] == i[None, :]).astype(np.float32), dtype=BF16)


def _mod_kernel(cond_ref, w_ref, b_ref, o_ref):
    s = _silu(cond_ref[...])
    o_ref[0] = _dot(s.astype(BF16), w_ref[0].astype(BF16)) + b_ref[0]


def _modulation(cond, mod_w, mod_b):
    n = cond.shape[0]
    return pl.pallas_call(
        _mod_kernel,
        grid=(DEPTH, 3),
        in_specs=[
            pl.BlockSpec((n, D_MODEL), lambda i, j: (0, 0)),
            pl.BlockSpec((1, D_MODEL, D_MODEL), lambda i, j: (i, 0, j)),
            pl.BlockSpec((1, 1, D_MODEL), lambda i, j: (i, 0, j)),
        ],
        out_specs=pl.BlockSpec((1, n, D_MODEL), lambda i, j: (i, 0, j)),
        out_shape=jax.ShapeDtypeStruct((DEPTH, n, 3 * D_MODEL), F32),
        compiler_params=_params("parallel", "parallel"),
    )(cond, mod_w, mod_b.reshape(DEPTH, 1, 3 * D_MODEL))


def _mod_row(latent):
    return (1 + pl.program_id(0)) if latent else 0


def _inproj_kernel(latent, splits, x_ref, g_ref, sh_ref, sc_ref, w_ref, *out_refs):
    r = _mod_row(latent)
    x = x_ref[0]
    y = x * lax.rsqrt(jnp.mean(x * x, axis=-1, keepdims=True) + EPS) * g_ref[...]
    shift = sh_ref[0, pl.ds(r, 1), :]
    scale = sc_ref[0, pl.ds(r, 1), :]
    h = (y * (1.0 + scale) + shift).astype(BF16)
    off = 0
    for o_ref, n in zip(out_refs, splits):
        o_ref[0] = _dot(h, w_ref[:, off:off + n])
        off += n


def _inproj(x, norm_g, mod, layer, w_bf16, splits, latent):
    B, T, _ = x.shape
    cols = w_bf16.shape[1]
    nrow = mod.shape[1]
    return pl.pallas_call(
        functools.partial(_inproj_kernel, latent, splits),
        grid=(B, T // ROW_TILE),
        in_specs=[
            pl.BlockSpec((1, ROW_TILE, D_MODEL), lambda b, i: (b, i, 0)),
            pl.BlockSpec((1, D_MODEL), lambda b, i: (0, 0)),
            pl.BlockSpec((1, nrow, D_MODEL), lambda b, i: (layer, 0, 0)),
            pl.BlockSpec((1, nrow, D_MODEL), lambda b, i: (layer, 0, 1)),
            pl.BlockSpec((D_MODEL, cols), lambda b, i: (0, 0)),
        ],
        out_specs=[pl.BlockSpec((1, ROW_TILE, n), lambda b, i: (b, i, 0)) for n in splits],
        out_shape=[jax.ShapeDtypeStruct((B, T, n), F32) for n in splits],
        compiler_params=_params("parallel", "parallel"),
    )(x, norm_g.reshape(1, D_MODEL), mod, mod, w_bf16)


def _outproj_kernel(latent, final, n_in, *refs):
    o_refs = refs[:n_in]
    w_refs = refs[n_in:2 * n_in]
    x_ref, gate_ref = refs[2 * n_in], refs[2 * n_in + 1]
    rest = refs[2 * n_in + 2:]
    r = _mod_row(latent)
    acc = _dot(o_refs[0][0].astype(BF16), w_refs[0][...])
    for o_ref, w_ref in zip(o_refs[1:], w_refs[1:]):
        acc = acc + _dot(o_ref[0].astype(BF16), w_ref[...])
    y = x_ref[0] + gate_ref[0, pl.ds(r, 1), :] * acc
    if final:
        fg_ref, out_ref = rest
        y = y * lax.rsqrt(jnp.mean(y * y, axis=-1, keepdims=True) + EPS) * fg_ref[...]
    else:
        (out_ref,) = rest
    out_ref[0] = y


def _outproj(outs, ws_bf16, x, mod, layer, latent, final_g=None):
    B, T, _ = x.shape
    nrow = mod.shape[1]
    n_in = len(outs)
    final = final_g is not None
    in_specs = [pl.BlockSpec((1, ROW_TILE, o.shape[-1]), lambda b, i: (b, i, 0)) for o in outs]
    in_specs += [pl.BlockSpec(w.shape, lambda b, i: (0, 0)) for w in ws_bf16]
    in_specs += [
        pl.BlockSpec((1, ROW_TILE, D_MODEL), lambda b, i: (b, i, 0)),
        pl.BlockSpec((1, nrow, D_MODEL), lambda b, i: (layer, 0, 2)),
    ]
    args = list(outs) + list(ws_bf16) + [x, mod]
    if final:
        in_specs.append(pl.BlockSpec((1, D_MODEL), lambda b, i: (0, 0)))
        args.append(final_g.reshape(1, D_MODEL))
    return pl.pallas_call(
        functools.partial(_outproj_kernel, latent, final, n_in),
        grid=(B, T // ROW_TILE),
        in_specs=in_specs,
        out_specs=pl.BlockSpec((1, ROW_TILE, D_MODEL), lambda b, i: (b, i, 0)),
        out_shape=jax.ShapeDtypeStruct((B, T, D_MODEL), F32),
        compiler_params=_params("parallel", "parallel"),
    )(*args)


def _rope_tables(T):
    n_rows = T // GRID_W
    row = jnp.repeat(jnp.arange(n_rows), GRID_W).astype(F32)
    col = jnp.tile(jnp.arange(GRID_W), n_rows).astype(F32)
    n_freq = HD_A // 4
    inv = ROPE_BASE ** (-jnp.arange(n_freq, dtype=F32) / n_freq)
    ang_r = row[:, None] * inv
    ang_c = col[:, None] * inv
    zero = jnp.zeros_like(ang_r)
    cos = jnp.concatenate([jnp.cos(ang_r), jnp.cos(ang_r), jnp.cos(ang_c), jnp.cos(ang_c)], axis=1)
    s1 = jnp.concatenate([-jnp.sin(ang_r), zero, -jnp.sin(ang_c), zero], axis=1)
    s2 = jnp.concatenate([zero, jnp.sin(ang_r), zero, jnp.sin(ang_c)], axis=1)
    return cos, s1, s2


def _rope(x, cos, s1, s2):
    n = x.shape[-1]
    q = HD_A // 4
    return x * cos + pltpu.roll(x, n - q, 1) * s1 + pltpu.roll(x, q, 1) * s2


def _attn_kernel(latent, S, *refs):
    if latent:
        (q_ref, k_ref, v_ref, ga_ref, qg_ref, kg_ref, bd_ref, cos_ref, s1_ref, s2_ref,
         cosk_ref, s1k_ref, s2k_ref, ck_ref, cv_ref, o_ref, km_ref, vm_ref) = refs
    else:
        (q_ref, k_ref, v_ref, ga_ref, qg_ref, kg_ref, bd_ref, o_ref, kn_ref, km_ref, vm_ref) = refs
    bd = bd_ref[...]
    inv_d = 1.0 / HD_A
    kw = KV_A * HD_A

    @pl.when(pl.program_id(1) == 0)
    def _():
        k = k_ref[0]
        kn = k * lax.rsqrt(_seg_sum(k * k, bd[:kw, :kw]) * inv_d + EPS) * kg_ref[...]
        v = v_ref[0]
        if latent:
            kn = _rope(kn, cosk_ref[...], s1k_ref[...], s2k_ref[...])
            k_all = jnp.concatenate([ck_ref[0, 0], kn], axis=0)
            v_all = jnp.concatenate([cv_ref[0, 0], v], axis=0)
        else:
            kn_ref[0] = kn
            k_all, v_all = kn, v
        lane = lax.broadcasted_iota(jnp.int32, (S, kw), 1)
        k_sw = pltpu.roll(k_all, HD_A, 1)
        v_sw = pltpu.roll(v_all, HD_A, 1)
        for j in range(KV_A):
            for half in range(2):
                keep = (lane < HD_A) if half == 0 else (lane >= HD_A)
                src_k, src_v = (k_all, v_all) if j == half else (k_sw, v_sw)
                km_ref[2 * j + half] = jnp.where(keep, src_k, 0.0).astype(BF16)
                vm_ref[2 * j + half] = jnp.where(keep, src_v, 0.0).astype(BF16)

    q = q_ref[0]
    qn = q * lax.rsqrt(_seg_sum(q * q, bd) * inv_d + EPS) * qg_ref[...]
    if latent:
        qn = _rope(qn, cos_ref[...], s1_ref[...], s2_ref[...])
    qb = (qn * HD_A ** -0.5).astype(BF16)
    for m in range(H_A // 2):
        blk = slice(m * LANES, (m + 1) * LANES)
        qs = qb[:, blk]
        acc = None
        for half in range(2):
            j = (2 * m + half) // (H_A // KV_A)
            s = lax.dot_general(qs, km_ref[2 * j + half], (((1,), (1,)), ((), ())),
                                preferred_element_type=F32)
            e = jnp.exp(s - jnp.max(s, axis=-1, keepdims=True))
            l = jnp.sum(e, axis=-1, keepdims=True)
            pv = _dot(e.astype(BF16), vm_ref[2 * j + half]) * (1.0 / l)
            acc = pv if acc is None else acc + pv
        o_ref[0, :, blk] = acc * _silu(ga_ref[0, :, blk])


def _attention(qa, ka, va, ga, qn_g, kn_g, ctx_k=None, ctx_v=None, layer_j=0):
    B, T, _ = qa.shape
    latent = ctx_k is not None
    S = T + (ctx_k.shape[2] if latent else 0)
    kw = KV_A * HD_A
    qw = H_A * HD_A
    bd = _block_diag_ones(qw, HD_A)
    qblk = pl.BlockSpec((1, Q_TILE, qw), lambda b, i: (b, i, 0))
    kblk = pl.BlockSpec((1, T, kw), lambda b, i: (b, 0, 0))
    in_specs = [qblk, kblk, kblk, qblk,
                pl.BlockSpec((1, qw), lambda b, i: (0, 0)),
                pl.BlockSpec((1, kw), lambda b, i: (0, 0)),
                pl.BlockSpec(bd.shape, lambda b, i: (0, 0))]
    args = [qa, ka, va, ga, jnp.tile(qn_g, H_A).reshape(1, -1), jnp.tile(kn_g, KV_A).reshape(1, -1), bd]
    out_specs = [qblk]
    out_shape = [jax.ShapeDtypeStruct((B, T, qw), F32)]
    if latent:
        tabs = _rope_tables(T)
        P = ctx_k.shape[2]
        in_specs += [pl.BlockSpec((Q_TILE, qw), lambda b, i: (i, 0))] * 3
        in_specs += [pl.BlockSpec((T, kw), lambda b, i: (0, 0))] * 3
        in_specs += [pl.BlockSpec((1, 1, P, kw), lambda b, i: (b, layer_j, 0, 0))] * 2
        args += [jnp.tile(t, (1, H_A)) for t in tabs] + [jnp.tile(t, (1, KV_A)) for t in tabs] + [ctx_k, ctx_v]
    else:
        out_specs.append(kblk)
        out_shape.append(jax.ShapeDtypeStruct((B, T, kw), F32))
    res = pl.pallas_call(
        functools.partial(_attn_kernel, latent, S),
        grid=(B, T // Q_TILE),
        in_specs=in_specs,
        out_specs=out_specs,
        out_shape=out_shape,
        scratch_shapes=[pltpu.VMEM((2 * KV_A, S, kw), BF16), pltpu.VMEM((2 * KV_A, S, kw), BF16)],
        compiler_params=_params("parallel", "arbitrary"),
    )(*args)
    return res if not latent else (res[0], None)


def _rwkv_prep_kernel(n_t, z_ref, zp_ref, zn_ref, mu_ref, w0_ref, a0_ref, w2_ref, a2_ref,
                      kkg_ref, ka_ref, rk_ref, bd_ref,
                      wf_ref, wb_ref, kdf_ref, kdb_ref, kaf_ref, kab_ref, nkk_ref, r_ref, v_ref, bonus_ref):
    i = pl.program_id(1)
    z = z_ref[0]
    n = z.shape[0]
    prev_row = jnp.where(i > 0, zp_ref[0, SUBLANES - 1:SUBLANES, :], 0.0)
    next_row = jnp.where(i < n_t - 1, zn_ref[0, 0:1, :], 0.0)
    rows = lax.broadcasted_iota(jnp.int32, (n, 1), 0)
    zp = jnp.where(rows == 0, prev_row, pltpu.roll(z, 1, 0))
    zn = jnp.where(rows == n - 1, next_row, pltpu.roll(z, n - 1, 0))
    zs = z + mu_ref[...] * (0.5 * (zp + zn) - z)

    rb = zs[:, 0:D_HALF]
    kb = zs[:, D_HALF:2 * D_HALF]
    vb = zs[:, 2 * D_HALF:3 * D_HALF]
    lw = zs[:, 3 * D_HALF:3 * D_HALF + 2 * W_LORA]
    la = zs[:, 3 * D_HALF + 2 * W_LORA:]
    w = jnp.exp(-RWKV_DECAY_SCALE * jax.nn.sigmoid(w0_ref[...] + _dot(jnp.tanh(lw).astype(BF16), w2_ref[...])))
    a = jax.nn.sigmoid(a0_ref[...] + _dot(la.astype(BF16), a2_ref[...]))
    bd = bd_ref[...]
    kk = kb * kkg_ref[...]
    kk = kk * lax.rsqrt(_seg_sum(kk * kk, bd) + 1e-12)
    ka = ka_ref[...]
    a_f, a_b = a[:, :D_HALF], a[:, D_HALF:]
    kd_f = kb * (1.0 + (a_f - 1.0) * ka)
    kd_b = kb * (1.0 + (a_b - 1.0) * ka)
    wf_ref[0] = w[:, :D_HALF]
    wb_ref[0] = w[:, D_HALF:]
    kdf_ref[0] = kd_f
    kdb_ref[0] = kd_b
    kaf_ref[0] = kk * a_f
    kab_ref[0] = kk * a_b
    nkk_ref[0] = -kk
    r_ref[0] = rb
    v_ref[0] = vb
    bonus_ref[0] = _seg_sum(rb * rk_ref[...] * (kd_f + kd_b), bd) * vb


def _block_diag2(m0, m1):
    z = jnp.zeros_like(m0)
    return jnp.concatenate([jnp.concatenate([m0, z], axis=1), jnp.concatenate([z, m1], axis=1)], axis=0)


def _rwkv_prep(zb, shift_mu, w0, w2, a0, a2, k_k, k_a, r_k):
    B, T, _ = zb.shape
    n_t = T // ROW_TILE
    per_tile = ROW_TILE // SUBLANES
    bd = _block_diag_ones(D_HALF, HS_B)
    row = lambda x: x.reshape(1, -1)
    vec = pl.BlockSpec((1, D_HALF), lambda b, i: (0, 0))
    vec2 = pl.BlockSpec((1, 2 * D_HALF), lambda b, i: (0, 0))
    out = pl.BlockSpec((1, ROW_TILE, D_HALF), lambda b, i: (b, i, 0))
    return pl.pallas_call(
        functools.partial(_rwkv_prep_kernel, n_t),
        grid=(B, n_t),
        in_specs=[
            pl.BlockSpec((1, ROW_TILE, B_SHIFT), lambda b, i: (b, i, 0)),
            pl.BlockSpec((1, SUBLANES, B_SHIFT), lambda b, i: (b, jnp.maximum(i * per_tile - 1, 0), 0)),
            pl.BlockSpec((1, SUBLANES, B_SHIFT),
                         lambda b, i: (b, jnp.minimum((i + 1) * per_tile, n_t * per_tile - 1), 0)),
            pl.BlockSpec((1, B_SHIFT), lambda b, i: (0, 0)),
            vec2, vec2,
            pl.BlockSpec((2 * W_LORA, 2 * D_HALF), lambda b, i: (0, 0)),
            pl.BlockSpec((2 * A_LORA, 2 * D_HALF), lambda b, i: (0, 0)),
            vec, vec, vec,
            pl.BlockSpec(bd.shape, lambda b, i: (0, 0)),
        ],
        out_specs=[out] * 10,
        out_shape=[jax.ShapeDtypeStruct((B, T, D_HALF), F32)] * 10,
        compiler_params=_params("parallel", "parallel"),
    )(zb, zb, zb, row(shift_mu), row(w0), row(a0),
      _block_diag2(w2[0], w2[1]).astype(BF16), _block_diag2(a2[0], a2[1]).astype(BF16),
      row(k_k), row(k_a), row(r_k), bd)


def _rwkv_scan_kernel(vh, w_ref, nkk_ref, kka_ref, kd_ref, r_ref, v_ref, s0_ref, y_ref, s_ref):
    @pl.when(pl.program_id(1) == 0)
    def _():
        s_ref[...] = s0_ref[...]

    def t_step(t, carry):
        w, nkk, kka, kd, r = w_ref[t], nkk_ref[t], kka_ref[t], kd_ref[t], r_ref[t]

        def v_step(g, c):
            base = pl.multiple_of(g * SUBLANES, SUBLANES)
            vblk = v_ref[t, pl.ds(base, SUBLANES), :]
            ys = []
            for u in range(SUBLANES):
                s = s_ref[base + u]
                sa = jnp.sum(s * nkk, axis=0, keepdims=True)
                s = s * w + sa * kka + vblk[u:u + 1, :] * kd
                s_ref[base + u] = s
                ys.append(jnp.sum(s * r, axis=0, keepdims=True))
            y_ref[t, pl.ds(base, SUBLANES), :] = jnp.concatenate(ys, axis=0)
            return c

        lax.fori_loop(0, vh // SUBLANES, v_step, 0)
        return carry

    lax.fori_loop(0, SCAN_T, t_step, 0)


def _rwkv_scan(w, nkk, kka, kd, r, v, s0):
    T, _, C = w.shape
    vh = v.shape[1]
    kvec = pl.BlockSpec((SCAN_T, HS_B, LANES), lambda g, t: (t, 0, g))
    return pl.pallas_call(
        functools.partial(_rwkv_scan_kernel, vh),
        grid=(C // LANES, T // SCAN_T),
        in_specs=[kvec] * 5 + [
            pl.BlockSpec((SCAN_T, vh, LANES), lambda g, t: (t, 0, g)),
            pl.BlockSpec((vh, HS_B, LANES), lambda g, t: (0, 0, g)),
        ],
        out_specs=[
            pl.BlockSpec((SCAN_T, vh, LANES), lambda g, t: (t, 0, g)),
            pl.BlockSpec((vh, HS_B, LANES), lambda g, t: (0, 0, g)),
        ],
        out_shape=[jax.ShapeDtypeStruct((T, vh, C), F32), jax.ShapeDtypeStruct((vh, HS_B, C), F32)],
        compiler_params=_params("parallel", "arbitrary"),
    )(w, nkk, kka, kd, r, v, s0)


def _rwkv_post_kernel(yf_ref, yb_ref, bonus_ref, gb_ref, lng_ref, lnb_ref, bd_ref, o_ref):
    bd = bd_ref[...]
    y = yf_ref[0] + yb_ref[0]
    inv_n = 1.0 / HS_B
    d = y - _seg_sum(y, bd) * inv_n
    var = _seg_sum(d * d, bd) * inv_n
    yn = d * lax.rsqrt(var + GN_EPS) * lng_ref[...] + lnb_ref[...]
    o_ref[0] = (yn + bonus_ref[0]) * _silu(gb_ref[0])


def _rwkv_post(y_f, y_b, bonus, gb, ln_g, ln_b):
    B, T, _ = y_f.shape
    bd = _block_diag_ones(D_HALF, HS_B)
    blk = pl.BlockSpec((1, ROW_TILE, D_HALF), lambda b, i: (b, i, 0))
    vec = pl.BlockSpec((1, D_HALF), lambda b, i: (0, 0))
    return pl.pallas_call(
        _rwkv_post_kernel,
        grid=(B, T // ROW_TILE),
        in_specs=[blk, blk, blk, blk, vec, vec, pl.BlockSpec(bd.shape, lambda b, i: (0, 0))],
        out_specs=blk,
        out_shape=jax.ShapeDtypeStruct((B, T, D_HALF), F32),
        compiler_params=_params("parallel", "parallel"),
    )(y_f, y_b, bonus, gb, ln_g.reshape(1, -1), ln_b.reshape(1, -1), bd)


def _to_chains(x):
    B, T, _ = x.shape
    return x.reshape(B, T, H_B, HS_B).transpose(1, 3, 0, 2).reshape(T, HS_B, B * H_B)


def _from_chains(y, B):
    T = y.shape[0]
    return y.reshape(T, HS_B, B, H_B).transpose(2, 0, 3, 1).reshape(B, T, H_B * HS_B)


def _rwkv_mixer(zb, gb, p, s_init):
    B, T, _ = zb.shape
    (w_f, w_b, kd_f, kd_b, ka_f, ka_b, nkk, r, v, bonus) = _rwkv_prep(
        zb, p["shift_mu"], p["w0"], p["w2"], p["a0"], p["a2"], p["k_k"], p["k_a"], p["r_k"])
    nbh = B * H_B
    split_v = (2 * nbh) % LANES != 0

    def both(xf, xb):
        x = jnp.concatenate([_to_chains(xf), jnp.flip(_to_chains(xb), axis=0)], axis=2)
        return jnp.concatenate([x, x], axis=2) if split_v else x

    vv = jnp.concatenate([_to_chains(v), jnp.flip(_to_chains(v), axis=0)], axis=2)
    if s_init is None:
        s0 = jnp.zeros((HS_B, HS_B, 2 * nbh), F32)
    else:
        s0 = jnp.concatenate([s.reshape(nbh, HS_B, HS_B).transpose(1, 2, 0) for s in s_init], axis=2)
    if split_v:
        half = HS_B // 2
        vv = jnp.concatenate([vv[:, :half], vv[:, half:]], axis=2)
        s0 = jnp.concatenate([s0[:half], s0[half:]], axis=2)
    y, s_fin = _rwkv_scan(both(w_f, w_b), both(nkk, nkk), both(ka_f, ka_b), both(kd_f, kd_b), both(r, r), vv, s0)
    if split_v:
        y = jnp.concatenate([y[:, :, :2 * nbh], y[:, :, 2 * nbh:]], axis=1)
        s_fin = jnp.concatenate([s_fin[:, :, :2 * nbh], s_fin[:, :, 2 * nbh:]], axis=0)
    y_f = _from_chains(y[:, :, :nbh], B)
    y_b = _from_chains(jnp.flip(y[:, :, nbh:], axis=0), B)
    o_b = _rwkv_post(y_f, y_b, bonus, gb, p["ln_g"], p["ln_b"])
    fin = [s_fin[:, :, d * nbh:(d + 1) * nbh].transpose(2, 0, 1).reshape(B, H_B, HS_B, HS_B) for d in range(2)]
    return o_b, fin


def _gla_kernel(T, q_ref, k_ref, v_ref, g_ref, gl_ref, w2_ref, gb_ref, lng_ref, tri_ref, s0f_ref, s0b_ref,
                o_ref, sf_ref, sb_ref, la_ref, acc_ref):
    C = GLA_CHUNK
    n_c = T // C
    pre = _dot(gl_ref[0].astype(BF16), w2_ref[0]) + gb_ref[0]
    la_ref[...] = jax.nn.log_sigmoid(pre) * (1.0 / GLA_TAU)
    qscale = DK_C ** -0.5
    contract_last = (((1,), (1,)), ((), ()))
    contract_first = (((0,), (0,)), ((), ()))

    for d, (s0_ref, sout_ref) in enumerate(((s0f_ref, sf_ref), (s0b_ref, sb_ref))):
        tri = tri_ref[d]
        tri_f = tri.astype(F32)

        def chunk(ci, st):
            c = ci if d == 0 else n_c - 1 - ci
            rows = pl.ds(pl.multiple_of(c * C, C), C)
            la = la_ref[rows, d * DK_C:(d + 1) * DK_C]
            hi = la.astype(BF16)
            r1 = la - hi.astype(F32)
            mid = r1.astype(BF16)
            lo = (r1 - mid.astype(F32)).astype(BF16)
            b = _dot(tri, hi) + _dot(tri, mid) + _dot(tri, lo)
            btot = jnp.sum(la, axis=0, keepdims=True)
            mref = 0.5 * btot
            q = q_ref[0, rows, :] * qscale
            k = k_ref[0, rows, :]
            vc = v_ref[0, rows, :].astype(BF16)
            qe = (q * jnp.exp(b - mref)).astype(BF16)
            ke = (k * jnp.exp(mref - b)).astype(BF16)
            qb = (q * jnp.exp(b)).astype(BF16)
            kl = (k * jnp.exp(btot - b)).astype(BF16)
            att = lax.dot_general(qe, ke, contract_last, preferred_element_type=F32) * tri_f
            o = lax.dot_general(qb, st.astype(BF16), contract_last, preferred_element_type=F32)
            o = o + _dot(att.astype(BF16), vc)
            if d == 0:
                acc_ref[rows, :] = o
            else:
                acc_ref[rows, :] = acc_ref[rows, :] + o
            return st * jnp.exp(btot) + lax.dot_general(vc, kl, contract_first, preferred_element_type=F32)

        st = lax.fori_loop(0, n_c, chunk, s0_ref[0, 0, 0])
        sout_ref[0, 0] = st

    o = acc_ref[...]
    o = o * lax.rsqrt(jnp.mean(o * o, axis=-1, keepdims=True) + EPS) * lng_ref[...]
    o_ref[0] = o * _silu(g_ref[0])


def _gla_mixer(q, k, v, g, gl, gw2, gbias, ln_g, s_init, layer_j):
    B, T, _ = q.shape
    C = GLA_CHUNK
    idx = np.arange(C)
    tri = jnp.asarray(np.stack([idx[:, None] >= idx[None, :], idx[:, None] <= idx[None, :]]).astype(np.float32),
                      dtype=BF16)
    w2 = jnp.stack([_block_diag2(gw2[0][:, h * DK_C:(h + 1) * DK_C], gw2[1][:, h * DK_C:(h + 1) * DK_C])
                    for h in range(H_C)]).astype(BF16)
    gb = jnp.stack([jnp.concatenate([gbias[0][h * DK_C:(h + 1) * DK_C], gbias[1][h * DK_C:(h + 1) * DK_C]])
                    for h in range(H_C)]).reshape(H_C, 1, 2 * DK_C)
    if s_init is None:
        zeros = jnp.zeros((1, 1, 1, DV_C, DK_C), F32)
        s0f = s0b = zeros
        s_spec = pl.BlockSpec((1, 1, 1, DV_C, DK_C), lambda b, h: (0, 0, 0, 0, 0))
    else:
        s0f, s0b = (jnp.swapaxes(s, -1, -2) for s in s_init)
        s_spec = pl.BlockSpec((1, 1, 1, DV_C, DK_C), lambda b, h: (b, layer_j, h, 0, 0))
    o, sf, sb = pl.pallas_call(
        functools.partial(_gla_kernel, T),
        grid=(B, H_C),
        in_specs=[
            pl.BlockSpec((1, T, DK_C), lambda b, h: (b, 0, h)),
            pl.BlockSpec((1, T, DK_C), lambda b, h: (b, 0, h)),
            pl.BlockSpec((1, T, DV_C), lambda b, h: (b, 0, h)),
            pl.BlockSpec((1, T, DV_C), lambda b, h: (b, 0, h)),
            pl.BlockSpec((1, T, 2 * G_LORA), lambda b, h: (b, 0, 0)),
            pl.BlockSpec((1, 2 * G_LORA, 2 * DK_C), lambda b, h: (h, 0, 0)),
            pl.BlockSpec((1, 1, 2 * DK_C), lambda b, h: (h, 0, 0)),
            pl.BlockSpec((1, DV_C), lambda b, h: (0, 0)),
            pl.BlockSpec((2, C, C), lambda b, h: (0, 0, 0)),
            s_spec, s_spec,
        ],
        out_specs=[
            pl.BlockSpec((1, T, DV_C), lambda b, h: (b, 0, h)),
            pl.BlockSpec((1, 1, DV_C, DK_C), lambda b, h: (b, h, 0, 0)),
            pl.BlockSpec((1, 1, DV_C, DK_C), lambda b, h: (b, h, 0, 0)),
        ],
        out_shape=[
            jax.ShapeDtypeStruct((B, T, H_C * DV_C), F32),
            jax.ShapeDtypeStruct((B, H_C, DV_C, DK_C), F32),
            jax.ShapeDtypeStruct((B, H_C, DV_C, DK_C), F32),
        ],
        scratch_shapes=[pltpu.VMEM((T, 2 * DK_C), F32), pltpu.VMEM((T, DV_C), F32)],
        compiler_params=_params("parallel", "parallel"),
    )(q, k, v, g, gl, w2, gb, ln_g.reshape(1, -1), tri, s0f, s0b)
    return o, (jnp.swapaxes(sf, -1, -2), jnp.swapaxes(sb, -1, -2))


def kernel(x_prompt, x_sample, c, cache_attn_k, cache_attn_v, state_rwkv_fwd, state_rwkv_bwd, state_gla_fwd, state_gla_bwd, c_ctx, norm_g, mod_w, mod_b, ev_w_in, ev_w_out, ev_qn_g, ev_kn_g, ev_shift_mu, rw_w0, rw_w2, rw_a0, rw_a2, rw_kk, rw_ka, rw_rk, rw_ln_g, rw_ln_b, od_w_in, od_w_out, gla_w2, gla_b, gla_ln_g, final_g):
    n_dec = c.shape[0]
    cond = jnp.concatenate([c_ctx[None], c, jnp.zeros((SUBLANES - 1 - n_dec, D_MODEL), F32)], axis=0)
    mod = _modulation(cond, mod_w, mod_b)

    ev_in = ev_w_in.astype(BF16)
    ev_out = ev_w_out.astype(BF16)
    od_in = od_w_in.astype(BF16)
    od_out = od_w_out.astype(BF16)
    kw = KV_A * HD_A
    ck = cache_attn_k.reshape(cache_attn_k.shape[:3] + (kw,))
    cv = cache_attn_v.reshape(cache_attn_v.shape[:3] + (kw,))
    gla_f_t = state_gla_fwd
    gla_b_t = state_gla_bwd

    def trunk(x, latent):
        new = {"k": [], "v": [], "rf": [], "rb": [], "gf": [], "gb": []}
        for i in range(DEPTH):
            j = i // 2
            fg = final_g if i == DEPTH - 1 else None
            if i % 2 == 0:
                qa, ka, va, ga, zb, gb = _inproj(x, norm_g[i], mod, i, ev_in[j], EV_SPLITS, latent)
                if latent:
                    o_a, _ = _attention(qa, ka, va, ga, ev_qn_g[j], ev_kn_g[j], ck, cv, j)
                    s_init = (state_rwkv_fwd[:, j], state_rwkv_bwd[:, j])
                else:
                    o_a, kn = _attention(qa, ka, va, ga, ev_qn_g[j], ev_kn_g[j])
                    s_init = None
                    new["k"].append(kn)
                    new["v"].append(va)
                p = dict(shift_mu=ev_shift_mu[j], w0=rw_w0[j], w2=rw_w2[j], a0=rw_a0[j], a2=rw_a2[j],
                         k_k=rw_kk[j], k_a=rw_ka[j], r_k=rw_rk[j], ln_g=rw_ln_g[j], ln_b=rw_ln_b[j])
                o_b, fin = _rwkv_mixer(zb, gb, p, s_init)
                new["rf"].append(fin[0])
                new["rb"].append(fin[1])
                x = _outproj([o_a, o_b], [ev_out[j, :D_HALF], ev_out[j, D_HALF:]], x, mod, i, latent, fg)
            else:
                q, k, v, g, gl = _inproj(x, norm_g[i], mod, i, od_in[j], OD_SPLITS, latent)
                s_init = (gla_f_t, gla_b_t) if latent else None
                o, fin = _gla_mixer(q, k, v, g, gl, gla_w2[j], gla_b[j], gla_ln_g[j], s_init, j)
                new["gf"].append(fin[0])
                new["gb"].append(fin[1])
                x = _outproj([o], [od_out[j]], x, mod, i, latent, fg)
        return x, new

    y_prompt, new = trunk(x_prompt, False)
    y_sample, _ = trunk(x_sample, True)
    B, T = x_prompt.shape[:2]
    heads = lambda t: t.reshape(B, T, KV_A, HD_A)
    return (y_prompt, y_sample,
            jnp.stack([heads(t) for t in new["k"]], axis=1), jnp.stack([heads(t) for t in new["v"]], axis=1),
            jnp.stack(new["rf"], axis=1), jnp.stack(new["rb"], axis=1),
            jnp.stack(new["gf"], axis=1), jnp.stack(new["gb"], axis=1))
```

```python
import functools

import numpy as np
import jax
import jax.numpy as jnp
from jax import lax
from jax.experimental import pallas as pl
from jax.experimental.pallas import tpu as pltpu

F32 = jnp.float32
BF16 = jnp.bfloat16

D_MODEL = 1024
DEPTH = 4
GRID_W = 64
D_HALF = D_MODEL // 2
HD_A = 64
H_A = D_HALF // HD_A
KV_A = H_A // 4
ROPE_BASE = 10000.0
HS_B = 64
H_B = D_HALF // HS_B
W_LORA = 64
A_LORA = 64
RWKV_DECAY_SCALE = 0.606531
GN_EPS = 64e-5
B_SHIFT = 3 * D_HALF + 2 * W_LORA + 2 * A_LORA
H_C = 4
DK_C = D_MODEL // 2 // H_C
DV_C = D_MODEL // H_C
G_LORA = 16
GLA_TAU = 16.0
EPS = 1e-6

EV_SPLITS = (H_A * HD_A, KV_A * HD_A, KV_A * HD_A, D_HALF, B_SHIFT, D_HALF)
OD_SPLITS = (H_C * DK_C, H_C * DK_C, D_MODEL, D_MODEL, 2 * G_LORA)

LANES = 128
SUBLANES = 8
VMEM_LIMIT_BYTES = 56 * 1024 * 1024

ROW_TILE = 256
Q_TILE = 256
GLA_CHUNK = 64
SCAN_T = 8


def _params(*sem):
    return pltpu.CompilerParams(dimension_semantics=sem, vmem_limit_bytes=VMEM_LIMIT_BYTES)


def _silu(x):
    return x * jax.nn.sigmoid(x)


def _dot(a, b):
    return jnp.dot(a, b, preferred_element_type=F32)


def _seg_sum(x, ones_bd):
    hi = x.astype(BF16)
    r1 = x - hi.astype(F32)
    mid = r1.astype(BF16)
    lo = (r1 - mid.astype(F32)).astype(BF16)
    return _dot(hi, ones_bd) + _dot(mid, ones_bd) + _dot(lo, ones_bd)


def _block_diag_ones(n, blk):
    i = np.arange(n) // blk
    return jnp.asarray((i[:, None] == i[None, :]).astype(np.float32), dtype=BF16)


def _mod_kernel(cond_ref, w_ref, b_ref, o_ref):
    s = _silu(cond_ref[...])
    o_ref[0] = _dot(s.astype(BF16), w_ref[0].astype(BF16)) + b_ref[0]


def _modulation(cond, mod_w, mod_b):
    n = cond.shape[0]
    return pl.pallas_call(
        _mod_kernel,
        grid=(DEPTH, 3),
        in_specs=[
            pl.BlockSpec((n, D_MODEL), lambda i, j: (0, 0)),
            pl.BlockSpec((1, D_MODEL, D_MODEL), lambda i, j: (i, 0, j)),
            pl.BlockSpec((1, 1, D_MODEL), lambda i, j: (i, 0, j)),
        ],
        out_specs=pl.BlockSpec((1, n, D_MODEL), lambda i, j: (i, 0, j)),
        out_shape=jax.ShapeDtypeStruct((DEPTH, n, 3 * D_MODEL), F32),
        compiler_params=_params("parallel", "parallel"),
    )(cond, mod_w, mod_b.reshape(DEPTH, 1, 3 * D_MODEL))


def _mod_row(latent):
    return (1 + pl.program_id(0)) if latent else 0


def _inproj_kernel(latent, splits, x_ref, g_ref, sh_ref, sc_ref, w_ref, *out_refs):
    r = _mod_row(latent)
    x = x_ref[0]
    y = x * lax.rsqrt(jnp.mean(x * x, axis=-1, keepdims=True) + EPS) * g_ref[...]
    shift = sh_ref[0, pl.ds(r, 1), :]
    scale = sc_ref[0, pl.ds(r, 1), :]
    h = (y * (1.0 + scale) + shift).astype(BF16)
    off = 0
    for o_ref, n in zip(out_refs, splits):
        o_ref[0] = _dot(h, w_ref[:, off:off + n])
        off += n


def _inproj(x, norm_g, mod, layer, w_bf16, splits, latent):
    B, T, _ = x.shape
    cols = w_bf16.shape[1]
    nrow = mod.shape[1]
    return pl.pallas_call(
        functools.partial(_inproj_kernel, latent, splits),
        grid=(B, T // ROW_TILE),
        in_specs=[
            pl.BlockSpec((1, ROW_TILE, D_MODEL), lambda b, i: (b, i, 0)),
            pl.BlockSpec((1, D_MODEL), lambda b, i: (0, 0)),
            pl.BlockSpec((1, nrow, D_MODEL), lambda b, i: (layer, 0, 0)),
            pl.BlockSpec((1, nrow, D_MODEL), lambda b, i: (layer, 0, 1)),
            pl.BlockSpec((D_MODEL, cols), lambda b, i: (0, 0)),
        ],
        out_specs=[pl.BlockSpec((1, ROW_TILE, n), lambda b, i: (b, i, 0)) for n in splits],
        out_shape=[jax.ShapeDtypeStruct((B, T, n), F32) for n in splits],
        compiler_params=_params("parallel", "parallel"),
    )(x, norm_g.reshape(1, D_MODEL), mod, mod, w_bf16)


def _outproj_kernel(latent, final, n_in, *refs):
    o_refs = refs[:n_in]
    w_refs = refs[n_in:2 * n_in]
    x_ref, gate_ref = refs[2 * n_in], refs[2 * n_in + 1]
    rest = refs[2 * n_in + 2:]
    r = _mod_row(latent)
    acc = _dot(o_refs[0][0].astype(BF16), w_refs[0][...])
    for o_ref, w_ref in zip(o_refs[1:], w_refs[1:]):
        acc = acc + _dot(o_ref[0].astype(BF16), w_ref[...])
    y = x_ref[0] + gate_ref[0, pl.ds(r, 1), :] * acc
    if final:
        fg_ref, out_ref = rest
        y = y * lax.rsqrt(jnp.mean(y * y, axis=-1, keepdims=True) + EPS) * fg_ref[...]
    else:
        (out_ref,) = rest
    out_ref[0] = y


def _outproj(outs, ws_bf16, x, mod, layer, latent, final_g=None):
    B, T, _ = x.shape
    nrow = mod.shape[1]
    n_in = len(outs)
    final = final_g is not None
    in_specs = [pl.BlockSpec((1, ROW_TILE, o.shape[-1]), lambda b, i: (b, i, 0)) for o in outs]
    in_specs += [pl.BlockSpec(w.shape, lambda b, i: (0, 0)) for w in ws_bf16]
    in_specs += [
        pl.BlockSpec((1, ROW_TILE, D_MODEL), lambda b, i: (b, i, 0)),
        pl.BlockSpec((1, nrow, D_MODEL), lambda b, i: (layer, 0, 2)),
    ]
    args = list(outs) + list(ws_bf16) + [x, mod]
    if final:
        in_specs.append(pl.BlockSpec((1, D_MODEL), lambda b, i: (0, 0)))
        args.append(final_g.reshape(1, D_MODEL))
    return pl.pallas_call(
        functools.partial(_outproj_kernel, latent, final, n_in),
        grid=(B, T // ROW_TILE),
        in_specs=in_specs,
        out_specs=pl.BlockSpec((1, ROW_TILE, D_MODEL), lambda b, i: (b, i, 0)),
        out_shape=jax.ShapeDtypeStruct((B, T, D_MODEL), F32),
        compiler_params=_params("parallel", "parallel"),
    )(*args)


def _rope_tables(T):
    n_rows = T // GRID_W
    row = jnp.repeat(jnp.arange(n_rows), GRID_W).astype(F32)
    col = jnp.tile(jnp.arange(GRID_W), n_rows).astype(F32)
    n_freq = HD_A // 4
    inv = ROPE_BASE ** (-jnp.arange(n_freq, dtype=F32) / n_freq)
    ang_r = row[:, None] * inv
    ang_c = col[:, None] * inv
    zero = jnp.zeros_like(ang_r)
    cos = jnp.concatenate([jnp.cos(ang_r), jnp.cos(ang_r), jnp.cos(ang_c), jnp.cos(ang_c)], axis=1)
    s1 = jnp.concatenate([-jnp.sin(ang_r), zero, -jnp.sin(ang_c), zero], axis=1)
    s2 = jnp.concatenate([zero, jnp.sin(ang_r), zero, jnp.sin(ang_c)], axis=1)
    return cos, s1, s2


def _rope(x, cos, s1, s2):
    n = x.shape[-1]
    q = HD_A // 4
    return x * cos + pltpu.roll(x, n - q, 1) * s1 + pltpu.roll(x, q, 1) * s2


def _attn_kernel(latent, S, *refs):
    if latent:
        (q_ref, k_ref, v_ref, ga_ref, qg_ref, kg_ref, bd_ref, cos_ref, s1_ref, s2_ref,
         cosk_ref, s1k_ref, s2k_ref, ck_ref, cv_ref, o_ref, km_ref, vm_ref) = refs
    else:
        (q_ref, k_ref, v_ref, ga_ref, qg_ref, kg_ref, bd_ref, o_ref, kn_ref, km_ref, vm_ref) = refs
    bd = bd_ref[...]
    inv_d = 1.0 / HD_A
    kw = KV_A * HD_A

    @pl.when(pl.program_id(1) == 0)
    def _():
        k = k_ref[0]
        kn = k * lax.rsqrt(_seg_sum(k * k, bd[:kw, :kw]) * inv_d + EPS) * kg_ref[...]
        v = v_ref[0]
        if latent:
            kn = _rope(kn, cosk_ref[...], s1k_ref[...], s2k_ref[...])
            k_all = jnp.concatenate([ck_ref[0, 0], kn], axis=0)
            v_all = jnp.concatenate([cv_ref[0, 0], v], axis=0)
        else:
            kn_ref[0] = kn
            k_all, v_all = kn, v
        lane = lax.broadcasted_iota(jnp.int32, (S, kw), 1)
        k_sw = pltpu.roll(k_all, HD_A, 1)
        v_sw = pltpu.roll(v_all, HD_A, 1)
        for j in range(KV_A):
            for half in range(2):
                keep = (lane < HD_A) if half == 0 else (lane >= HD_A)
                src_k, src_v = (k_all, v_all) if j == half else (k_sw, v_sw)
                km_ref[2 * j + half] = jnp.where(keep, src_k, 0.0).astype(BF16)
                vm_ref[2 * j + half] = jnp.where(keep, src_v, 0.0).astype(BF16)

    q = q_ref[0]
    qn = q * lax.rsqrt(_seg_sum(q * q, bd) * inv_d + EPS) * qg_ref[...]
    if latent:
        qn = _rope(qn, cos_ref[...], s1_ref[...], s2_ref[...])
    qb = (qn * HD_A ** -0.5).astype(BF16)
    for m in range(H_A // 2):
        blk = slice(m * LANES, (m + 1) * LANES)
        qs = qb[:, blk]
        acc = None
        for half in range(2):
            j = (2 * m + half) // (H_A // KV_A)
            s = lax.dot_general(qs, km_ref[2 * j + half], (((1,), (1,)), ((), ())),
                                preferred_element_type=F32)
            e = jnp.exp(s - jnp.max(s, axis=-1, keepdims=True))
            l = jnp.sum(e, axis=-1, keepdims=True)
            pv = _dot(e.astype(BF16), vm_ref[2 * j + half]) * (1.0 / l)
            acc = pv if acc is None else acc + pv
        o_ref[0, :, blk] = acc * _silu(ga_ref[0, :, blk])


def _attention(qa, ka, va, ga, qn_g, kn_g, ctx_k=None, ctx_v=None, layer_j=0):
    B, T, _ = qa.shape
    latent = ctx_k is not None
    S = T + (ctx_k.shape[2] if latent else 0)
    kw = KV_A * HD_A
    qw = H_A * HD_A
    bd = _block_diag_ones(qw, HD_A)
    qblk = pl.BlockSpec((1, Q_TILE, qw), lambda b, i: (b, i, 0))
    kblk = pl.BlockSpec((1, T, kw), lambda b, i: (b, 0, 0))
    in_specs = [qblk, kblk, kblk, qblk,
                pl.BlockSpec((1, qw), lambda b, i: (0, 0)),
                pl.BlockSpec((1, kw), lambda b, i: (0, 0)),
                pl.BlockSpec(bd.shape, lambda b, i: (0, 0))]
    args = [qa, ka, va, ga, jnp.tile(qn_g, H_A).reshape(1, -1), jnp.tile(kn_g, KV_A).reshape(1, -1), bd]
    out_specs = [qblk]
    out_shape = [jax.ShapeDtypeStruct((B, T, qw), F32)]
    if latent:
        tabs = _rope_tables(T)
        P = ctx_k.shape[2]
        in_specs += [pl.BlockSpec((Q_TILE, qw), lambda b, i: (i, 0))] * 3
        in_specs += [pl.BlockSpec((T, kw), lambda b, i: (0, 0))] * 3
        in_specs += [pl.BlockSpec((1, 1, P, kw), lambda b, i: (b, layer_j, 0, 0))] * 2
        args += [jnp.tile(t, (1, H_A)) for t in tabs] + [jnp.tile(t, (1, KV_A)) for t in tabs] + [ctx_k, ctx_v]
    else:
        out_specs.append(kblk)
        out_shape.append(jax.ShapeDtypeStruct((B, T, kw), F32))
    res = pl.pallas_call(
        functools.partial(_attn_kernel, latent, S),
        grid=(B, T // Q_TILE),
        in_specs=in_specs,
        out_specs=out_specs,
        out_shape=out_shape,
        scratch_shapes=[pltpu.VMEM((2 * KV_A, S, kw), BF16), pltpu.VMEM((2 * KV_A, S, kw), BF16)],
        compiler_params=_params("parallel", "arbitrary"),
    )(*args)
    return res if not latent else (res[0], None)


def _rwkv_prep_kernel(n_t, z_ref, zp_ref, zn_ref, mu_ref, w0_ref, a0_ref, w2_ref, a2_ref,
                      kkg_ref, ka_ref, rk_ref, bd_ref,
                      wf_ref, wb_ref, kdf_ref, kdb_ref, kaf_ref, kab_ref, nkk_ref, r_ref, v_ref, bonus_ref):
    i = pl.program_id(1)
    z = z_ref[0]
    n = z.shape[0]
    prev_row = jnp.where(i > 0, zp_ref[0, SUBLANES - 1:SUBLANES, :], 0.0)
    next_row = jnp.where(i < n_t - 1, zn_ref[0, 0:1, :], 0.0)
    rows = lax.broadcasted_iota(jnp.int32, (n, 1), 0)
    zp = jnp.where(rows == 0, prev_row, pltpu.roll(z, 1, 0))
    zn = jnp.where(rows == n - 1, next_row, pltpu.roll(z, n - 1, 0))
    zs = z + mu_ref[...] * (0.5 * (zp + zn) - z)

    rb = zs[:, 0:D_HALF]
    kb = zs[:, D_HALF:2 * D_HALF]
    vb = zs[:, 2 * D_HALF:3 * D_HALF]
    lw = zs[:, 3 * D_HALF:3 * D_HALF + 2 * W_LORA]
    la = zs[:, 3 * D_HALF + 2 * W_LORA:]
    w = jnp.exp(-RWKV_DECAY_SCALE * jax.nn.sigmoid(w0_ref[...] + _dot(jnp.tanh(lw).astype(BF16), w2_ref[...])))
    a = jax.nn.sigmoid(a0_ref[...] + _dot(la.astype(BF16), a2_ref[...]))
    bd = bd_ref[...]
    kk = kb * kkg_ref[...]
    kk = kk * lax.rsqrt(_seg_sum(kk * kk, bd) + 1e-12)
    ka = ka_ref[...]
    a_f, a_b = a[:, :D_HALF], a[:, D_HALF:]
    kd_f = kb * (1.0 + (a_f - 1.0) * ka)
    kd_b = kb * (1.0 + (a_b - 1.0) * ka)
    wf_ref[0] = w[:, :D_HALF]
    wb_ref[0] = w[:, D_HALF:]
    kdf_ref[0] = kd_f
    kdb_ref[0] = kd_b
    kaf_ref[0] = kk * a_f
    kab_ref[0] = kk * a_b
    nkk_ref[0] = -kk
    r_ref[0] = rb
    v_ref[0] = vb
    bonus_ref[0] = _seg_sum(rb * rk_ref[...] * (kd_f + kd_b), bd) * vb


def _block_diag2(m0, m1):
    z = jnp.zeros_like(m0)
    return jnp.concatenate([jnp.concatenate([m0, z], axis=1), jnp.concatenate([z, m1], axis=1)], axis=0)


def _rwkv_prep(zb, shift_mu, w0, w2, a0, a2, k_k, k_a, r_k):
    B, T, _ = zb.shape
    n_t = T // ROW_TILE
    per_tile = ROW_TILE // SUBLANES
    bd = _block_diag_ones(D_HALF, HS_B)
    row = lambda x: x.reshape(1, -1)
    vec = pl.BlockSpec((1, D_HALF), lambda b, i: (0, 0))
    vec2 = pl.BlockSpec((1, 2 * D_HALF), lambda b, i: (0, 0))
    out = pl.BlockSpec((1, ROW_TILE, D_HALF), lambda b, i: (b, i, 0))
    return pl.pallas_call(
        functools.partial(_rwkv_prep_kernel, n_t),
        grid=(B, n_t),
        in_specs=[
            pl.BlockSpec((1, ROW_TILE, B_SHIFT), lambda b, i: (b, i, 0)),
            pl.BlockSpec((1, SUBLANES, B_SHIFT), lambda b, i: (b, jnp.maximum(i * per_tile - 1, 0), 0)),
            pl.BlockSpec((1, SUBLANES, B_SHIFT),
                         lambda b, i: (b, jnp.minimum((i + 1) * per_tile, n_t * per_tile - 1), 0)),
            pl.BlockSpec((1, B_SHIFT), lambda b, i: (0, 0)),
            vec2, vec2,
            pl.BlockSpec((2 * W_LORA, 2 * D_HALF), lambda b, i: (0, 0)),
            pl.BlockSpec((2 * A_LORA, 2 * D_HALF), lambda b, i: (0, 0)),
            vec, vec, vec,
            pl.BlockSpec(bd.shape, lambda b, i: (0, 0)),
        ],
        out_specs=[out] * 10,
        out_shape=[jax.ShapeDtypeStruct((B, T, D_HALF), F32)] * 10,
        compiler_params=_params("parallel", "parallel"),
    )(zb, zb, zb, row(shift_mu), row(w0), row(a0),
      _block_diag2(w2[0], w2[1]).astype(BF16), _block_diag2(a2[0], a2[1]).astype(BF16),
      row(k_k), row(k_a), row(r_k), bd)


def _rwkv_scan_kernel(vh, reverse, w_ref, nkk_ref, kka_ref, kd_ref, r_ref, v_ref, s0_ref, y_ref, s_ref):
    @pl.when(pl.program_id(1) == 0)
    def _():
        s_ref[...] = s0_ref[...]

    def t_step(i, carry):
        t = (SCAN_T - 1 - i) if reverse else i
        w, nkk, kka, kd, r = w_ref[t], nkk_ref[t], kka_ref[t], kd_ref[t], r_ref[t]

        def v_step(g, c):
            base = pl.multiple_of(g * SUBLANES, SUBLANES)
            vblk = v_ref[t, pl.ds(base, SUBLANES), :]
            ys = []
            for u in range(SUBLANES):
                s = s_ref[base + u]
                sa = jnp.sum(s * nkk, axis=0, keepdims=True)
                s = s * w + sa * kka + vblk[u:u + 1, :] * kd
                s_ref[base + u] = s
                ys.append(jnp.sum(s * r, axis=0, keepdims=True))
            y_ref[t, pl.ds(base, SUBLANES), :] = jnp.concatenate(ys, axis=0)
            return c

        lax.fori_loop(0, vh // SUBLANES, v_step, 0)
        return carry

    lax.fori_loop(0, SCAN_T, t_step, 0)


def _rwkv_scan(w, nkk, kka, kd, r, v, s0, reverse):
    T, _, C = w.shape
    vh = v.shape[1]
    n_t = T // SCAN_T
    tblk = (lambda t: n_t - 1 - t) if reverse else (lambda t: t)
    kvec = pl.BlockSpec((SCAN_T, HS_B, LANES), lambda g, t: (tblk(t), 0, g))
    vvec = pl.BlockSpec((SCAN_T, vh, LANES), lambda g, t: (tblk(t), 0, g))
    state = pl.BlockSpec((vh, HS_B, LANES), lambda g, t: (0, 0, g))
    return pl.pallas_call(
        functools.partial(_rwkv_scan_kernel, vh, reverse),
        grid=(C // LANES, n_t),
        in_specs=[kvec] * 5 + [vvec, state],
        out_specs=[vvec, state],
        out_shape=[jax.ShapeDtypeStruct((T, vh, C), F32), jax.ShapeDtypeStruct((vh, HS_B, C), F32)],
        compiler_params=_params("parallel", "arbitrary"),
    )(w, nkk, kka, kd, r, v, s0)


def _rwkv_post_kernel(yf_ref, yb_ref, bonus_ref, gb_ref, lng_ref, lnb_ref, bd_ref, o_ref):
    bd = bd_ref[...]
    y = yf_ref[0] + yb_ref[0]
    inv_n = 1.0 / HS_B
    d = y - _seg_sum(y, bd) * inv_n
    var = _seg_sum(d * d, bd) * inv_n
    yn = d * lax.rsqrt(var + GN_EPS) * lng_ref[...] + lnb_ref[...]
    o_ref[0] = (yn + bonus_ref[0]) * _silu(gb_ref[0])


def _rwkv_post(y_f, y_b, bonus, gb, ln_g, ln_b):
    B, T, _ = y_f.shape
    bd = _block_diag_ones(D_HALF, HS_B)
    blk = pl.BlockSpec((1, ROW_TILE, D_HALF), lambda b, i: (b, i, 0))
    vec = pl.BlockSpec((1, D_HALF), lambda b, i: (0, 0))
    return pl.pallas_call(
        _rwkv_post_kernel,
        grid=(B, T // ROW_TILE),
        in_specs=[blk, blk, blk, blk, vec, vec, pl.BlockSpec(bd.shape, lambda b, i: (0, 0))],
        out_specs=blk,
        out_shape=jax.ShapeDtypeStruct((B, T, D_HALF), F32),
        compiler_params=_params("parallel", "parallel"),
    )(y_f, y_b, bonus, gb, ln_g.reshape(1, -1), ln_b.reshape(1, -1), bd)


def _rwkv_mixer(zb, gb, p, s_init):
    B, T, _ = zb.shape
    (w_f, w_b, kd_f, kd_b, ka_f, ka_b, nkk, r, v, bonus) = _rwkv_prep(
        zb, p["shift_mu"], p["w0"], p["w2"], p["a0"], p["a2"], p["k_k"], p["k_a"], p["r_k"])
    nbh = B * H_B
    vs = max(LANES // nbh, 1)
    vh = HS_B // vs

    def kchains(x):
        x = x.reshape(B, T, H_B, HS_B).transpose(1, 3, 0, 2).reshape(T, HS_B, 1, nbh)
        return jnp.broadcast_to(x, (T, HS_B, vs, nbh)).reshape(T, HS_B, vs * nbh)

    def vchains(x):
        x = x.reshape(B, T, H_B, vs, vh).transpose(1, 4, 3, 0, 2)
        return x.reshape(T, vh, vs * nbh)

    def unchain(y):
        return y.reshape(T, vh, vs, B, H_B).transpose(3, 0, 4, 2, 1).reshape(B, T, H_B * HS_B)

    nkk_c, r_c, v_c = kchains(nkk), kchains(r), vchains(v)
    ys, fin = [], []
    for d, (w_d, ka_d, kd_d) in enumerate(((w_f, ka_f, kd_f), (w_b, ka_b, kd_b))):
        if s_init is None:
            s0 = jnp.zeros((vh, HS_B, vs * nbh), F32)
        else:
            s0 = s_init[d].reshape(B, H_B, vs, vh, HS_B).transpose(3, 4, 2, 0, 1).reshape(vh, HS_B, vs * nbh)
        y, s_fin = _rwkv_scan(kchains(w_d), nkk_c, kchains(ka_d), kchains(kd_d), r_c, v_c, s0, reverse=(d == 1))
        ys.append(unchain(y))
        fin.append(s_fin.reshape(vh, HS_B, vs, B, H_B).transpose(3, 4, 2, 0, 1).reshape(B, H_B, HS_B, HS_B))
    o_b = _rwkv_post(ys[0], ys[1], bonus, gb, p["ln_g"], p["ln_b"])
    return o_b, fin


def _gla_kernel(T, q_ref, k_ref, v_ref, g_ref, gl_ref, w2_ref, gb_ref, lng_ref, tri_ref, s0f_ref, s0b_ref,
                o_ref, sf_ref, sb_ref, la_ref, acc_ref, qb_ref, kl_ref, dec_ref, st_ref):
    C = GLA_CHUNK
    n_c = T // C
    pre = _dot(gl_ref[0].astype(BF16), w2_ref[0]) + gb_ref[0]
    la_ref[...] = jax.nn.log_sigmoid(pre) * (1.0 / GLA_TAU)
    qscale = DK_C ** -0.5
    contract_last = (((1,), (1,)), ((), ()))
    contract_first = (((0,), (0,)), ((), ()))
    tri = [tri_ref[0], tri_ref[1]]
    tri_f = [t.astype(F32) for t in tri]

    def chunk_rows(c):
        return pl.ds(pl.multiple_of(c * C, C), C)

    def intra(c, carry):
        rows = chunk_rows(c)
        q = q_ref[0, rows, :] * qscale
        k = k_ref[0, rows, :]
        vc = v_ref[0, rows, :].astype(BF16)
        o = None
        for d in range(2):
            la = la_ref[rows, d * DK_C:(d + 1) * DK_C]
            hi = la.astype(BF16)
            r1 = la - hi.astype(F32)
            mid = r1.astype(BF16)
            lo = (r1 - mid.astype(F32)).astype(BF16)
            b3 = _dot(tri[d], jnp.concatenate([hi, mid, lo], axis=1))
            b = b3[:, :DK_C] + b3[:, DK_C:2 * DK_C] + b3[:, 2 * DK_C:]
            btot = jnp.sum(la, axis=0, keepdims=True)
            mref = 0.5 * btot
            e_up = jnp.exp(b - mref)
            e_dn = jnp.exp(mref - b)
            e_half = jnp.exp(mref)
            qe = q * e_up
            ke = k * e_dn
            qb_ref[d, rows, :] = (qe * e_half).astype(BF16)
            kl_ref[d, rows, :] = (ke * e_half).astype(BF16)
            dec_ref[d, pl.ds(pl.multiple_of(c * SUBLANES, SUBLANES), SUBLANES), :] = jnp.broadcast_to(
                jnp.exp(btot), (SUBLANES, DK_C))
            att = lax.dot_general(qe.astype(BF16), ke.astype(BF16), contract_last,
                                  preferred_element_type=F32) * tri_f[d]
            od = _dot(att.astype(BF16), vc)
            o = od if o is None else o + od
        acc_ref[rows, :] = o
        return carry

    lax.fori_loop(0, n_c, intra, 0, unroll=2)

    st_ref[0] = s0f_ref[0, 0, 0]
    st_ref[1] = s0b_ref[0, 0, 0]

    def inter(ci, carry):
        for d in range(2):
            c = ci if d == 0 else n_c - 1 - ci
            rows = chunk_rows(c)
            st = st_ref[d]
            acc_ref[rows, :] = acc_ref[rows, :] + lax.dot_general(
                qb_ref[d, rows, :], st.astype(BF16), contract_last, preferred_element_type=F32)
            dec = dec_ref[d, pl.ds(pl.multiple_of(c * SUBLANES, SUBLANES), 1), :]
            st_ref[d] = st * dec + lax.dot_general(v_ref[0, rows, :].astype(BF16), kl_ref[d, rows, :],
                                                   contract_first, preferred_element_type=F32)
        return carry

    lax.fori_loop(0, n_c, inter, 0)
    sf_ref[0, 0] = st_ref[0]
    sb_ref[0, 0] = st_ref[1]

    o = acc_ref[...]
    o = o * lax.rsqrt(jnp.mean(o * o, axis=-1, keepdims=True) + EPS) * lng_ref[...]
    o_ref[0] = o * _silu(g_ref[0])


def _gla_mixer(q, k, v, g, gl, gw2, gbias, ln_g, s_init, layer_j):
    B, T, _ = q.shape
    C = GLA_CHUNK
    idx = np.arange(C)
    tri = jnp.asarray(np.stack([idx[:, None] >= idx[None, :], idx[:, None] <= idx[None, :]]).astype(np.float32),
                      dtype=BF16)
    w2 = jnp.stack([_block_diag2(gw2[0][:, h * DK_C:(h + 1) * DK_C], gw2[1][:, h * DK_C:(h + 1) * DK_C])
                    for h in range(H_C)]).astype(BF16)
    gb = jnp.stack([jnp.concatenate([gbias[0][h * DK_C:(h + 1) * DK_C], gbias[1][h * DK_C:(h + 1) * DK_C]])
                    for h in range(H_C)]).reshape(H_C, 1, 2 * DK_C)
    if s_init is None:
        zeros = jnp.zeros((1, 1, 1, DV_C, DK_C), F32)
        s0f = s0b = zeros
        s_spec = pl.BlockSpec((1, 1, 1, DV_C, DK_C), lambda b, h: (0, 0, 0, 0, 0))
    else:
        s0f, s0b = (jnp.swapaxes(s, -1, -2) for s in s_init)
        s_spec = pl.BlockSpec((1, 1, 1, DV_C, DK_C), lambda b, h: (b, layer_j, h, 0, 0))
    o, sf, sb = pl.pallas_call(
        functools.partial(_gla_kernel, T),
        grid=(B, H_C),
        in_specs=[
            pl.BlockSpec((1, T, DK_C), lambda b, h: (b, 0, h)),
            pl.BlockSpec((1, T, DK_C), lambda b, h: (b, 0, h)),
            pl.BlockSpec((1, T, DV_C), lambda b, h: (b, 0, h)),
            pl.BlockSpec((1, T, DV_C), lambda b, h: (b, 0, h)),
            pl.BlockSpec((1, T, 2 * G_LORA), lambda b, h: (b, 0, 0)),
            pl.BlockSpec((1, 2 * G_LORA, 2 * DK_C), lambda b, h: (h, 0, 0)),
            pl.BlockSpec((1, 1, 2 * DK_C), lambda b, h: (h, 0, 0)),
            pl.BlockSpec((1, DV_C), lambda b, h: (0, 0)),
            pl.BlockSpec((2, C, C), lambda b, h: (0, 0, 0)),
            s_spec, s_spec,
        ],
        out_specs=[
            pl.BlockSpec((1, T, DV_C), lambda b, h: (b, 0, h)),
            pl.BlockSpec((1, 1, DV_C, DK_C), lambda b, h: (b, h, 0, 0)),
            pl.BlockSpec((1, 1, DV_C, DK_C), lambda b, h: (b, h, 0, 0)),
        ],
        out_shape=[
            jax.ShapeDtypeStruct((B, T, H_C * DV_C), F32),
            jax.ShapeDtypeStruct((B, H_C, DV_C, DK_C), F32),
            jax.ShapeDtypeStruct((B, H_C, DV_C, DK_C), F32),
        ],
        scratch_shapes=[
            pltpu.VMEM((T, 2 * DK_C), F32), pltpu.VMEM((T, DV_C), F32),
            pltpu.VMEM((2, T, DK_C), BF16), pltpu.VMEM((2, T, DK_C), BF16),
            pltpu.VMEM((2, T // C * SUBLANES, DK_C), F32), pltpu.VMEM((2, DV_C, DK_C), F32),
        ],
        compiler_params=_params("parallel", "parallel"),
    )(q, k, v, g, gl, w2, gb, ln_g.reshape(1, -1), tri, s0f, s0b)
    return o, (jnp.swapaxes(sf, -1, -2), jnp.swapaxes(sb, -1, -2))


def kernel(x_prompt, x_sample, c, cache_attn_k, cache_attn_v, state_rwkv_fwd, state_rwkv_bwd, state_gla_fwd, state_gla_bwd, c_ctx, norm_g, mod_w, mod_b, ev_w_in, ev_w_out, ev_qn_g, ev_kn_g, ev_shift_mu, rw_w0, rw_w2, rw_a0, rw_a2, rw_kk, rw_ka, rw_rk, rw_ln_g, rw_ln_b, od_w_in, od_w_out, gla_w2, gla_b, gla_ln_g, final_g):
    n_dec = c.shape[0]
    cond = jnp.concatenate([c_ctx[None], c, jnp.zeros((SUBLANES - 1 - n_dec, D_MODEL), F32)], axis=0)
    mod = _modulation(cond, mod_w, mod_b)

    ev_in = ev_w_in.astype(BF16)
    ev_out = ev_w_out.astype(BF16)
    od_in = od_w_in.astype(BF16)
    od_out = od_w_out.astype(BF16)
    kw = KV_A * HD_A
    ck = cache_attn_k.reshape(cache_attn_k.shape[:3] + (kw,))
    cv = cache_attn_v.reshape(cache_attn_v.shape[:3] + (kw,))
    gla_f_t = state_gla_fwd
    gla_b_t = state_gla_bwd

    def trunk(x, latent):
        new = {"k": [], "v": [], "rf": [], "rb": [], "gf": [], "gb": []}
        for i in range(DEPTH):
            j = i // 2
            fg = final_g if i == DEPTH - 1 else None
            if i % 2 == 0:
                qa, ka, va, ga, zb, gb = _inproj(x, norm_g[i], mod, i, ev_in[j], EV_SPLITS, latent)
                if latent:
                    o_a, _ = _attention(qa, ka, va, ga, ev_qn_g[j], ev_kn_g[j], ck, cv, j)
                    s_init = (state_rwkv_fwd[:, j], state_rwkv_bwd[:, j])
                else:
                    o_a, kn = _attention(qa, ka, va, ga, ev_qn_g[j], ev_kn_g[j])
                    s_init = None
                    new["k"].append(kn)
                    new["v"].append(va)
                p = dict(shift_mu=ev_shift_mu[j], w0=rw_w0[j], w2=rw_w2[j], a0=rw_a0[j], a2=rw_a2[j],
                         k_k=rw_kk[j], k_a=rw_ka[j], r_k=rw_rk[j], ln_g=rw_ln_g[j], ln_b=rw_ln_b[j])
                o_b, fin = _rwkv_mixer(zb, gb, p, s_init)
                new["rf"].append(fin[0])
                new["rb"].append(fin[1])
                x = _outproj([o_a, o_b], [ev_out[j, :D_HALF], ev_out[j, D_HALF:]], x, mod, i, latent, fg)
            else:
                q, k, v, g, gl = _inproj(x, norm_g[i], mod, i, od_in[j], OD_SPLITS, latent)
                s_init = (gla_f_t, gla_b_t) if latent else None
                o, fin = _gla_mixer(q, k, v, g, gl, gla_w2[j], gla_b[j], gla_ln_g[j], s_init, j)
                new["gf"].append(fin[0])
                new["gb"].append(fin[1])
                x = _outproj([o], [od_out[j]], x, mod, i, latent, fg)
        return x, new

    y_prompt, new = trunk(x_prompt, False)
    y_sample, _ = trunk(x_sample, True)
    B, T = x_prompt.shape[:2]
    heads = lambda t: t.reshape(B, T, KV_A, HD_A)
    return (y_prompt, y_sample,
            jnp.stack([heads(t) for t in new["k"]], axis=1), jnp.stack([heads(t) for t in new["v"]], axis=1),
            jnp.stack(new["rf"], axis=1), jnp.stack(new["rb"], axis=1),
            jnp.stack(new["gf"], axis=1), jnp.stack(new["gb"], axis=1))
```

```python
import functools

import numpy as np
import jax
import jax.numpy as jnp
from jax import lax
from jax.experimental import pallas as pl
from jax.experimental.pallas import tpu as pltpu

F32 = jnp.float32
BF16 = jnp.bfloat16

D_MODEL = 1024
DEPTH = 4
GRID_W = 64
D_HALF = D_MODEL // 2
HD_A = 64
H_A = D_HALF // HD_A
KV_A = H_A // 4
ROPE_BASE = 10000.0
HS_B = 64
H_B = D_HALF // HS_B
W_LORA = 64
A_LORA = 64
RWKV_DECAY_SCALE = 0.606531
GN_EPS = 64e-5
B_SHIFT = 3 * D_HALF + 2 * W_LORA + 2 * A_LORA
H_C = 4
DK_C = D_MODEL // 2 // H_C
DV_C = D_MODEL // H_C
G_LORA = 16
GLA_TAU = 16.0
EPS = 1e-6

EV_SPLITS = (H_A * HD_A, KV_A * HD_A, KV_A * HD_A, D_HALF, B_SHIFT, D_HALF)
OD_SPLITS = (H_C * DK_C, H_C * DK_C, D_MODEL, D_MODEL, 2 * G_LORA)

LANES = 128
SUBLANES = 8
VMEM_LIMIT_BYTES = 56 * 1024 * 1024

ROW_TILE = 256
Q_TILE = 256
GLA_CHUNK = 64
SCAN_T = 8


def _params(*sem):
    return pltpu.CompilerParams(dimension_semantics=sem, vmem_limit_bytes=VMEM_LIMIT_BYTES)


def _silu(x):
    return x * jax.nn.sigmoid(x)


def _dot(a, b):
    return jnp.dot(a, b, preferred_element_type=F32)


def _seg_sum(x, ones_bd):
    hi = x.astype(BF16)
    r1 = x - hi.astype(F32)
    mid = r1.astype(BF16)
    lo = (r1 - mid.astype(F32)).astype(BF16)
    return _dot(hi, ones_bd) + _dot(mid, ones_bd) + _dot(lo, ones_bd)


def _get(ref):
    return ref[...].reshape(ref.shape[-2:])


def _put(ref, val):
    ref[...] = val.reshape(ref.shape)


def _block_diag_ones(n, blk):
    i = np.arange(n) // blk
    return jnp.asarray((i[:, None] == i[None, :]).astype(np.float32), dtype=BF16)


def _mod_kernel(cond_ref, w_ref, b_ref, o_ref):
    s = _silu(cond_ref[...])
    o_ref[0] = _dot(s.astype(BF16), w_ref[0].astype(BF16)) + b_ref[0]


def _modulation(cond, mod_w, mod_b):
    n = cond.shape[0]
    return pl.pallas_call(
        _mod_kernel,
        grid=(DEPTH, 3),
        in_specs=[
            pl.BlockSpec((n, D_MODEL), lambda i, j: (0, 0)),
            pl.BlockSpec((1, D_MODEL, D_MODEL), lambda i, j: (i, 0, j)),
            pl.BlockSpec((1, 1, D_MODEL), lambda i, j: (i, 0, j)),
        ],
        out_specs=pl.BlockSpec((1, n, D_MODEL), lambda i, j: (i, 0, j)),
        out_shape=jax.ShapeDtypeStruct((DEPTH, n, 3 * D_MODEL), F32),
        compiler_params=_params("parallel", "parallel"),
    )(cond, mod_w, mod_b.reshape(DEPTH, 1, 3 * D_MODEL))


def _mod_row(latent):
    return (1 + pl.program_id(0)) if latent else 0


def _inproj_kernel(latent, splits, x_ref, g_ref, sh_ref, sc_ref, w_ref, *out_refs):
    r = _mod_row(latent)
    x = x_ref[0]
    y = x * lax.rsqrt(jnp.mean(x * x, axis=-1, keepdims=True) + EPS) * g_ref[...]
    shift = sh_ref[0, pl.ds(r, 1), :]
    scale = sc_ref[0, pl.ds(r, 1), :]
    h = (y * (1.0 + scale) + shift).astype(BF16)
    off = 0
    for o_ref, n in zip(out_refs, splits):
        o_ref[0] = _dot(h, w_ref[:, off:off + n])
        off += n


def _inproj(x, norm_g, mod, layer, w_bf16, splits, latent):
    B, T, _ = x.shape
    cols = w_bf16.shape[1]
    nrow = mod.shape[1]
    return pl.pallas_call(
        functools.partial(_inproj_kernel, latent, splits),
        grid=(B, T // ROW_TILE),
        in_specs=[
            pl.BlockSpec((1, ROW_TILE, D_MODEL), lambda b, i: (b, i, 0)),
            pl.BlockSpec((1, D_MODEL), lambda b, i: (0, 0)),
            pl.BlockSpec((1, nrow, D_MODEL), lambda b, i: (layer, 0, 0)),
            pl.BlockSpec((1, nrow, D_MODEL), lambda b, i: (layer, 0, 1)),
            pl.BlockSpec((D_MODEL, cols), lambda b, i: (0, 0)),
        ],
        out_specs=[pl.BlockSpec((1, ROW_TILE, n), lambda b, i: (b, i, 0)) for n in splits],
        out_shape=[jax.ShapeDtypeStruct((B, T, n), F32) for n in splits],
        compiler_params=_params("parallel", "parallel"),
    )(x, norm_g.reshape(1, D_MODEL), mod, mod, w_bf16)


def _outproj_kernel(latent, final, n_in, *refs):
    o_refs = refs[:n_in]
    w_refs = refs[n_in:2 * n_in]
    x_ref, gate_ref = refs[2 * n_in], refs[2 * n_in + 1]
    rest = refs[2 * n_in + 2:]
    r = _mod_row(latent)
    acc = _dot(o_refs[0][0].astype(BF16), w_refs[0][...])
    for o_ref, w_ref in zip(o_refs[1:], w_refs[1:]):
        acc = acc + _dot(o_ref[0].astype(BF16), w_ref[...])
    y = x_ref[0] + gate_ref[0, pl.ds(r, 1), :] * acc
    if final:
        fg_ref, out_ref = rest
        y = y * lax.rsqrt(jnp.mean(y * y, axis=-1, keepdims=True) + EPS) * fg_ref[...]
    else:
        (out_ref,) = rest
    out_ref[0] = y


def _outproj(outs, ws_bf16, x, mod, layer, latent, final_g=None):
    B, T, _ = x.shape
    nrow = mod.shape[1]
    n_in = len(outs)
    final = final_g is not None
    in_specs = [pl.BlockSpec((1, ROW_TILE, o.shape[-1]), lambda b, i: (b, i, 0)) for o in outs]
    in_specs += [pl.BlockSpec(w.shape, lambda b, i: (0, 0)) for w in ws_bf16]
    in_specs += [
        pl.BlockSpec((1, ROW_TILE, D_MODEL), lambda b, i: (b, i, 0)),
        pl.BlockSpec((1, nrow, D_MODEL), lambda b, i: (layer, 0, 2)),
    ]
    args = list(outs) + list(ws_bf16) + [x, mod]
    if final:
        in_specs.append(pl.BlockSpec((1, D_MODEL), lambda b, i: (0, 0)))
        args.append(final_g.reshape(1, D_MODEL))
    return pl.pallas_call(
        functools.partial(_outproj_kernel, latent, final, n_in),
        grid=(B, T // ROW_TILE),
        in_specs=in_specs,
        out_specs=pl.BlockSpec((1, ROW_TILE, D_MODEL), lambda b, i: (b, i, 0)),
        out_shape=jax.ShapeDtypeStruct((B, T, D_MODEL), F32),
        compiler_params=_params("parallel", "parallel"),
    )(*args)


def _rope_tables(T):
    n_rows = T // GRID_W
    row = jnp.repeat(jnp.arange(n_rows), GRID_W).astype(F32)
    col = jnp.tile(jnp.arange(GRID_W), n_rows).astype(F32)
    n_freq = HD_A // 4
    inv = ROPE_BASE ** (-jnp.arange(n_freq, dtype=F32) / n_freq)
    ang_r = row[:, None] * inv
    ang_c = col[:, None] * inv
    zero = jnp.zeros_like(ang_r)
    cos = jnp.concatenate([jnp.cos(ang_r), jnp.cos(ang_r), jnp.cos(ang_c), jnp.cos(ang_c)], axis=1)
    s1 = jnp.concatenate([-jnp.sin(ang_r), zero, -jnp.sin(ang_c), zero], axis=1)
    s2 = jnp.concatenate([zero, jnp.sin(ang_r), zero, jnp.sin(ang_c)], axis=1)
    return cos, s1, s2


def _rope(x, cos, s1, s2):
    n = x.shape[-1]
    q = HD_A // 4
    return x * cos + pltpu.roll(x, n - q, 1) * s1 + pltpu.roll(x, q, 1) * s2


def _attn_kernel(latent, S, *refs):
    if latent:
        (q_ref, k_ref, v_ref, ga_ref, qg_ref, kg_ref, bd_ref, cos_ref, s1_ref, s2_ref,
         cosk_ref, s1k_ref, s2k_ref, ck_ref, cv_ref, o_ref, km_ref, vm_ref) = refs
    else:
        (q_ref, k_ref, v_ref, ga_ref, qg_ref, kg_ref, bd_ref, o_ref, kn_ref, km_ref, vm_ref) = refs
    bd = bd_ref[...]
    inv_d = 1.0 / HD_A
    kw = KV_A * HD_A

    @pl.when(pl.program_id(1) == 0)
    def _():
        k = k_ref[0]
        kn = k * lax.rsqrt(_seg_sum(k * k, bd[:kw, :kw]) * inv_d + EPS) * kg_ref[...]
        v = v_ref[0]
        if latent:
            kn = _rope(kn, cosk_ref[...], s1k_ref[...], s2k_ref[...])
            k_all = jnp.concatenate([ck_ref[0, 0], kn], axis=0)
            v_all = jnp.concatenate([cv_ref[0, 0], v], axis=0)
        else:
            kn_ref[0] = kn
            k_all, v_all = kn, v
        lane = lax.broadcasted_iota(jnp.int32, (S, kw), 1)
        k_sw = pltpu.roll(k_all, HD_A, 1)
        v_sw = pltpu.roll(v_all, HD_A, 1)
        for j in range(KV_A):
            for half in range(2):
                keep = (lane < HD_A) if half == 0 else (lane >= HD_A)
                src_k, src_v = (k_all, v_all) if j == half else (k_sw, v_sw)
                km_ref[2 * j + half] = jnp.where(keep, src_k, 0.0).astype(BF16)
                vm_ref[2 * j + half] = jnp.where(keep, src_v, 0.0).astype(BF16)

    q = q_ref[0]
    qn = q * lax.rsqrt(_seg_sum(q * q, bd) * inv_d + EPS) * qg_ref[...]
    if latent:
        qn = _rope(qn, cos_ref[...], s1_ref[...], s2_ref[...])
    qb = (qn * HD_A ** -0.5).astype(BF16)
    for m in range(H_A // 2):
        blk = slice(m * LANES, (m + 1) * LANES)
        qs = qb[:, blk]
        acc = None
        for half in range(2):
            j = (2 * m + half) // (H_A // KV_A)
            s = lax.dot_general(qs, km_ref[2 * j + half], (((1,), (1,)), ((), ())),
                                preferred_element_type=F32)
            e = jnp.exp(s - jnp.max(s, axis=-1, keepdims=True))
            l = jnp.sum(e, axis=-1, keepdims=True)
            pv = _dot(e.astype(BF16), vm_ref[2 * j + half]) * (1.0 / l)
            acc = pv if acc is None else acc + pv
        o_ref[0, :, blk] = acc * _silu(ga_ref[0, :, blk])


def _attention(qa, ka, va, ga, qn_g, kn_g, ctx_k=None, ctx_v=None, layer_j=0):
    B, T, _ = qa.shape
    latent = ctx_k is not None
    S = T + (ctx_k.shape[2] if latent else 0)
    kw = KV_A * HD_A
    qw = H_A * HD_A
    bd = _block_diag_ones(qw, HD_A)
    qblk = pl.BlockSpec((1, Q_TILE, qw), lambda b, i: (b, i, 0))
    kblk = pl.BlockSpec((1, T, kw), lambda b, i: (b, 0, 0))
    in_specs = [qblk, kblk, kblk, qblk,
                pl.BlockSpec((1, qw), lambda b, i: (0, 0)),
                pl.BlockSpec((1, kw), lambda b, i: (0, 0)),
                pl.BlockSpec(bd.shape, lambda b, i: (0, 0))]
    args = [qa, ka, va, ga, jnp.tile(qn_g, H_A).reshape(1, -1), jnp.tile(kn_g, KV_A).reshape(1, -1), bd]
    out_specs = [qblk]
    out_shape = [jax.ShapeDtypeStruct((B, T, qw), F32)]
    if latent:
        tabs = _rope_tables(T)
        P = ctx_k.shape[2]
        in_specs += [pl.BlockSpec((Q_TILE, qw), lambda b, i: (i, 0))] * 3
        in_specs += [pl.BlockSpec((T, kw), lambda b, i: (0, 0))] * 3
        in_specs += [pl.BlockSpec((1, 1, P, kw), lambda b, i: (b, layer_j, 0, 0))] * 2
        args += [jnp.tile(t, (1, H_A)) for t in tabs] + [jnp.tile(t, (1, KV_A)) for t in tabs] + [ctx_k, ctx_v]
    else:
        out_specs.append(kblk)
        out_shape.append(jax.ShapeDtypeStruct((B, T, kw), F32))
    res = pl.pallas_call(
        functools.partial(_attn_kernel, latent, S),
        grid=(B, T // Q_TILE),
        in_specs=in_specs,
        out_specs=out_specs,
        out_shape=out_shape,
        scratch_shapes=[pltpu.VMEM((2 * KV_A, S, kw), BF16), pltpu.VMEM((2 * KV_A, S, kw), BF16)],
        compiler_params=_params("parallel", "arbitrary"),
    )(*args)
    return res if not latent else (res[0], None)


def _rwkv_prep_kernel(n_t, z_ref, zp_ref, zn_ref, mu_ref, w0_ref, a0_ref, w2_ref, a2_ref,
                      kkg_ref, ka_ref, rk_ref, bd_ref,
                      wf_ref, wb_ref, kdf_ref, kdb_ref, kaf_ref, kab_ref, nkk_ref, r_ref, v_ref, bonus_ref):
    i = pl.program_id(1)
    z = z_ref[0]
    n = z.shape[0]
    prev_row = jnp.where(i > 0, zp_ref[0, SUBLANES - 1:SUBLANES, :], 0.0)
    next_row = jnp.where(i < n_t - 1, zn_ref[0, 0:1, :], 0.0)
    rows = lax.broadcasted_iota(jnp.int32, (n, 1), 0)
    zp = jnp.where(rows == 0, prev_row, pltpu.roll(z, 1, 0))
    zn = jnp.where(rows == n - 1, next_row, pltpu.roll(z, n - 1, 0))
    zs = z + mu_ref[...] * (0.5 * (zp + zn) - z)

    rb = zs[:, 0:D_HALF]
    kb = zs[:, D_HALF:2 * D_HALF]
    vb = zs[:, 2 * D_HALF:3 * D_HALF]
    lw = zs[:, 3 * D_HALF:3 * D_HALF + 2 * W_LORA]
    la = zs[:, 3 * D_HALF + 2 * W_LORA:]
    w = jnp.exp(-RWKV_DECAY_SCALE * jax.nn.sigmoid(w0_ref[...] + _dot(jnp.tanh(lw).astype(BF16), w2_ref[...])))
    a = jax.nn.sigmoid(a0_ref[...] + _dot(la.astype(BF16), a2_ref[...]))
    bd = bd_ref[...]
    kk = kb * kkg_ref[...]
    kk = kk * lax.rsqrt(_seg_sum(kk * kk, bd) + 1e-12)
    ka = ka_ref[...]
    a_f, a_b = a[:, :D_HALF], a[:, D_HALF:]
    kd_f = kb * (1.0 + (a_f - 1.0) * ka)
    kd_b = kb * (1.0 + (a_b - 1.0) * ka)
    _put(wf_ref, w[:, :D_HALF])
    _put(wb_ref, w[:, D_HALF:])
    _put(kdf_ref, kd_f)
    _put(kdb_ref, kd_b)
    _put(kaf_ref, kk * a_f)
    _put(kab_ref, kk * a_b)
    _put(nkk_ref, -kk)
    _put(r_ref, rb)
    _put(v_ref, vb)
    _put(bonus_ref, _seg_sum(rb * rk_ref[...] * (kd_f + kd_b), bd) * vb)


def _block_diag2(m0, m1):
    z = jnp.zeros_like(m0)
    return jnp.concatenate([jnp.concatenate([m0, z], axis=1), jnp.concatenate([z, m1], axis=1)], axis=0)


def _rwkv_prep(zb, shift_mu, w0, w2, a0, a2, k_k, k_a, r_k, time_major):
    B, T, _ = zb.shape
    n_t = T // ROW_TILE
    per_tile = ROW_TILE // SUBLANES
    bd = _block_diag_ones(D_HALF, HS_B)
    row = lambda x: x.reshape(1, -1)
    vec = pl.BlockSpec((1, D_HALF), lambda b, i: (0, 0))
    vec2 = pl.BlockSpec((1, 2 * D_HALF), lambda b, i: (0, 0))
    out = pl.BlockSpec((1, ROW_TILE, D_HALF), lambda b, i: (b, i, 0))
    out_tm = pl.BlockSpec((ROW_TILE, D_HALF), lambda b, i: (i, b))
    return pl.pallas_call(
        functools.partial(_rwkv_prep_kernel, n_t),
        grid=(B, n_t),
        in_specs=[
            pl.BlockSpec((1, ROW_TILE, B_SHIFT), lambda b, i: (b, i, 0)),
            pl.BlockSpec((1, SUBLANES, B_SHIFT), lambda b, i: (b, jnp.maximum(i * per_tile - 1, 0), 0)),
            pl.BlockSpec((1, SUBLANES, B_SHIFT),
                         lambda b, i: (b, jnp.minimum((i + 1) * per_tile, n_t * per_tile - 1), 0)),
            pl.BlockSpec((1, B_SHIFT), lambda b, i: (0, 0)),
            vec2, vec2,
            pl.BlockSpec((2 * W_LORA, 2 * D_HALF), lambda b, i: (0, 0)),
            pl.BlockSpec((2 * A_LORA, 2 * D_HALF), lambda b, i: (0, 0)),
            vec, vec, vec,
            pl.BlockSpec(bd.shape, lambda b, i: (0, 0)),
        ],
        out_specs=[out_tm if time_major else out] * 9 + [out],
        out_shape=[jax.ShapeDtypeStruct((T, B * D_HALF) if time_major else (B, T, D_HALF), F32)] * 9
        + [jax.ShapeDtypeStruct((B, T, D_HALF), F32)],
        compiler_params=_params("parallel", "parallel"),
    )(zb, zb, zb, row(shift_mu), row(w0), row(a0),
      _block_diag2(w2[0], w2[1]).astype(BF16), _block_diag2(a2[0], a2[1]).astype(BF16),
      row(k_k), row(k_a), row(r_k), bd)


def _scan_steps(vh, n_steps, reverse, kvecs, vblock, yput, s_ref):
    def t_step(i, carry):
        t = (n_steps - 1 - i) if reverse else i
        w, nkk, kka, kd, r = kvecs(t)

        def v_step(g, c):
            base = pl.multiple_of(g * SUBLANES, SUBLANES)
            vblk = vblock(t, base)
            ys = []
            for u in range(SUBLANES):
                s = s_ref[base + u]
                sa = jnp.sum(s * nkk, axis=0, keepdims=True)
                s = s * w + sa * kka + vblk[u:u + 1, :] * kd
                s_ref[base + u] = s
                ys.append(jnp.sum(s * r, axis=0, keepdims=True))
            yput(t, base, jnp.concatenate(ys, axis=0))
            return c

        lax.fori_loop(0, vh // SUBLANES, v_step, 0)
        return carry

    lax.fori_loop(0, n_steps, t_step, 0)


def _rwkv_scan_kernel(vh, reverse, w_ref, nkk_ref, kka_ref, kd_ref, r_ref, v_ref, s0_ref, y_ref, s_ref):
    @pl.when(pl.program_id(1) == 0)
    def _():
        s_ref[...] = s0_ref[...]

    def yput(t, base, val):
        y_ref[t, pl.ds(base, SUBLANES), :] = val

    _scan_steps(vh, SCAN_T, reverse,
                lambda t: (w_ref[t], nkk_ref[t], kka_ref[t], kd_ref[t], r_ref[t]),
                lambda t, base: v_ref[t, pl.ds(base, SUBLANES), :], yput, s_ref)


def _rwkv_scan(w, nkk, kka, kd, r, v, s0, reverse):
    T, _, C = w.shape
    vh = v.shape[1]
    n_t = T // SCAN_T
    tblk = (lambda t: n_t - 1 - t) if reverse else (lambda t: t)
    kvec = pl.BlockSpec((SCAN_T, HS_B, LANES), lambda g, t: (tblk(t), 0, g))
    vvec = pl.BlockSpec((SCAN_T, vh, LANES), lambda g, t: (tblk(t), 0, g))
    state = pl.BlockSpec((vh, HS_B, LANES), lambda g, t: (0, 0, g))
    return pl.pallas_call(
        functools.partial(_rwkv_scan_kernel, vh, reverse),
        grid=(C // LANES, n_t),
        in_specs=[kvec] * 5 + [vvec, state],
        out_specs=[vvec, state],
        out_shape=[jax.ShapeDtypeStruct((T, vh, C), F32), jax.ShapeDtypeStruct((vh, HS_B, C), F32)],
        compiler_params=_params("parallel", "arbitrary"),
    )(w, nkk, kka, kd, r, v, s0)


HEADS_PER_GROUP = LANES // 32


def _chain_pair(x1, x2):
    lo = lax.broadcasted_iota(jnp.int32, (x1.shape[0], LANES), 1) < HS_B
    pieces = []
    for h in range(HEADS_PER_GROUP):
        blk = slice((h // 2) * LANES, (h // 2 + 1) * LANES)
        a, b = x1[:, blk], x2[:, blk]
        if h % 2 == 0:
            pieces.append(jnp.where(lo, a, pltpu.roll(b, HS_B, 1)))
        else:
            pieces.append(jnp.where(lo, pltpu.roll(a, HS_B, 1), b))
    return jnp.concatenate(pieces, axis=0).T


def _unchain_pair(y0, y1):
    yt = jnp.concatenate([y0, y1], axis=0).T
    nb = yt.shape[0] // HEADS_PER_GROUP
    lo = lax.broadcasted_iota(jnp.int32, (nb, LANES), 1) < HS_B
    rows = [yt[h * nb:(h + 1) * nb] for h in range(HEADS_PER_GROUP)]
    out0, out1 = [], []
    for m in range(HEADS_PER_GROUP // 2):
        even, odd = rows[2 * m], rows[2 * m + 1]
        out0.append(jnp.where(lo, even, pltpu.roll(odd, HS_B, 1)))
        out1.append(jnp.where(lo, pltpu.roll(even, HS_B, 1), odd))
    return jnp.concatenate(out0, axis=1), jnp.concatenate(out1, axis=1)


def _rwkv_scan_tm_kernel(reverse, w_ref, nkk_ref, kka_ref, kd_ref, r_ref, v_ref, s0_ref, y_ref, s_ref,
                         kbuf, ybuf):
    @pl.when(pl.program_id(1) == 0)
    def _():
        s_ref[...] = s0_ref[...]

    pairs = ((w_ref, nkk_ref), (kka_ref, kd_ref), (r_ref, v_ref))
    for t in range(SCAN_T):
        for p, (a_ref, b_ref) in enumerate(pairs):
            both = _chain_pair(a_ref[t], b_ref[t])
            kbuf[2 * p, t] = both[:HS_B]
            kbuf[2 * p + 1, t] = both[HS_B:]

    def yput(t, base, val):
        ybuf[t, pl.ds(base, SUBLANES), :] = val

    _scan_steps(HS_B, SCAN_T, reverse,
                lambda t: (kbuf[0, t], kbuf[1, t], kbuf[2, t], kbuf[3, t], kbuf[4, t]),
                lambda t, base: kbuf[5, t, pl.ds(base, SUBLANES), :], yput, s_ref)

    for t in range(0, SCAN_T, 2):
        y_ref[t], y_ref[t + 1] = _unchain_pair(ybuf[t], ybuf[t + 1])


def _rwkv_scan_tm(w, nkk, kka, kd, r, v, s0, reverse):
    T, B, W = w.shape
    gw = HEADS_PER_GROUP * HS_B
    n_t = T // SCAN_T
    tblk = (lambda t: n_t - 1 - t) if reverse else (lambda t: t)
    vec = pl.BlockSpec((SCAN_T, B, gw), lambda g, t: (tblk(t), 0, g))
    state = pl.BlockSpec((HS_B, HS_B, LANES), lambda g, t: (0, 0, g))
    return pl.pallas_call(
        functools.partial(_rwkv_scan_tm_kernel, reverse),
        grid=(W // gw, n_t),
        in_specs=[vec] * 6 + [state],
        out_specs=[vec, state],
        out_shape=[jax.ShapeDtypeStruct((T, B, W), F32), jax.ShapeDtypeStruct((HS_B, HS_B, W // gw * LANES), F32)],
        scratch_shapes=[pltpu.VMEM((6, SCAN_T, HS_B, LANES), F32), pltpu.VMEM((SCAN_T, HS_B, LANES), F32)],
        compiler_params=_params("parallel", "arbitrary"),
    )(w, nkk, kka, kd, r, v, s0)


def _rwkv_post_kernel(yf_ref, yb_ref, bonus_ref, gb_ref, lng_ref, lnb_ref, bd_ref, o_ref):
    bd = bd_ref[...]
    y = _get(yf_ref) + _get(yb_ref)
    inv_n = 1.0 / HS_B
    d = y - _seg_sum(y, bd) * inv_n
    var = _seg_sum(d * d, bd) * inv_n
    yn = d * lax.rsqrt(var + GN_EPS) * lng_ref[...] + lnb_ref[...]
    o_ref[0] = (yn + bonus_ref[0]) * _silu(gb_ref[0])


def _rwkv_post(y_f, y_b, bonus, gb, ln_g, ln_b):
    B, T, _ = bonus.shape
    bd = _block_diag_ones(D_HALF, HS_B)
    blk = pl.BlockSpec((1, ROW_TILE, D_HALF), lambda b, i: (b, i, 0))
    yblk = pl.BlockSpec((ROW_TILE, D_HALF), lambda b, i: (i, b)) if y_f.ndim == 2 else blk
    vec = pl.BlockSpec((1, D_HALF), lambda b, i: (0, 0))
    return pl.pallas_call(
        _rwkv_post_kernel,
        grid=(B, T // ROW_TILE),
        in_specs=[yblk, yblk, blk, blk, vec, vec, pl.BlockSpec(bd.shape, lambda b, i: (0, 0))],
        out_specs=blk,
        out_shape=jax.ShapeDtypeStruct((B, T, D_HALF), F32),
        compiler_params=_params("parallel", "parallel"),
    )(y_f, y_b, bonus, gb, ln_g.reshape(1, -1), ln_b.reshape(1, -1), bd)


def _rwkv_mixer(zb, gb, p, s_init):
    B, T, _ = zb.shape
    nbh = B * H_B
    time_major = B * HEADS_PER_GROUP == LANES
    (w_f, w_b, kd_f, kd_b, ka_f, ka_b, nkk, r, v, bonus) = _rwkv_prep(
        zb, p["shift_mu"], p["w0"], p["w2"], p["a0"], p["a2"], p["k_k"], p["k_a"], p["r_k"], time_major)
    ys, fin = [], []
    if time_major:
        tm = lambda x: x.reshape(T, B, H_B * HS_B)
        for d, (w_d, ka_d, kd_d) in enumerate(((w_f, ka_f, kd_f), (w_b, ka_b, kd_b))):
            if s_init is None:
                s0 = jnp.zeros((HS_B, HS_B, nbh), F32)
            else:
                s0 = s_init[d].transpose(2, 3, 1, 0).reshape(HS_B, HS_B, nbh)
            y, s_fin = _rwkv_scan_tm(tm(w_d), tm(nkk), tm(ka_d), tm(kd_d), tm(r), tm(v), s0, reverse=(d == 1))
            ys.append(y.reshape(T, B * H_B * HS_B))
            fin.append(s_fin.reshape(HS_B, HS_B, H_B, B).transpose(3, 2, 0, 1))
        return _rwkv_post(ys[0], ys[1], bonus, gb, p["ln_g"], p["ln_b"]), fin

    vs = max(LANES // nbh, 1)
    vh = HS_B // vs

    def kchains(x):
        x = x.reshape(B, T, H_B, HS_B).transpose(1, 3, 0, 2).reshape(T, HS_B, 1, nbh)
        return jnp.broadcast_to(x, (T, HS_B, vs, nbh)).reshape(T, HS_B, vs * nbh)

    def vchains(x):
        x = x.reshape(B, T, H_B, vs, vh).transpose(1, 4, 3, 0, 2)
        return x.reshape(T, vh, vs * nbh)

    def unchain(y):
        return y.reshape(T, vh, vs, B, H_B).transpose(3, 0, 4, 2, 1).reshape(B, T, H_B * HS_B)

    nkk_c, r_c, v_c = kchains(nkk), kchains(r), vchains(v)
    for d, (w_d, ka_d, kd_d) in enumerate(((w_f, ka_f, kd_f), (w_b, ka_b, kd_b))):
        if s_init is None:
            s0 = jnp.zeros((vh, HS_B, vs * nbh), F32)
        else:
            s0 = s_init[d].reshape(B, H_B, vs, vh, HS_B).transpose(3, 4, 2, 0, 1).reshape(vh, HS_B, vs * nbh)
        y, s_fin = _rwkv_scan(kchains(w_d), nkk_c, kchains(ka_d), kchains(kd_d), r_c, v_c, s0, reverse=(d == 1))
        ys.append(unchain(y))
        fin.append(s_fin.reshape(vh, HS_B, vs, B, H_B).transpose(3, 4, 2, 0, 1).reshape(B, H_B, HS_B, HS_B))
    o_b = _rwkv_post(ys[0], ys[1], bonus, gb, p["ln_g"], p["ln_b"])
    return o_b, fin


def _gla_kernel(T, q_ref, k_ref, v_ref, g_ref, gl_ref, w2_ref, gb_ref, lng_ref, tri_ref, s0f_ref, s0b_ref,
                o_ref, sf_ref, sb_ref, la_ref, acc_ref, qb_ref, kl_ref, dec_ref, st_ref):
    C = GLA_CHUNK
    n_c = T // C
    pre = _dot(gl_ref[0].astype(BF16), w2_ref[0]) + gb_ref[0]
    la_ref[...] = jax.nn.log_sigmoid(pre) * (1.0 / GLA_TAU)
    qscale = DK_C ** -0.5
    contract_last = (((1,), (1,)), ((), ()))
    contract_first = (((0,), (0,)), ((), ()))
    tri = [tri_ref[0], tri_ref[1]]
    tri_f = [t.astype(F32) for t in tri]

    def chunk_rows(c):
        return pl.ds(pl.multiple_of(c * C, C), C)

    def intra(c, carry):
        rows = chunk_rows(c)
        q = q_ref[0, rows, :] * qscale
        k = k_ref[0, rows, :]
        vc = v_ref[0, rows, :].astype(BF16)
        o = None
        for d in range(2):
            la = la_ref[rows, d * DK_C:(d + 1) * DK_C]
            hi = la.astype(BF16)
            r1 = la - hi.astype(F32)
            mid = r1.astype(BF16)
            lo = (r1 - mid.astype(F32)).astype(BF16)
            b3 = _dot(tri[d], jnp.concatenate([hi, mid, lo], axis=1))
            b = b3[:, :DK_C] + b3[:, DK_C:2 * DK_C] + b3[:, 2 * DK_C:]
            btot = jnp.sum(la, axis=0, keepdims=True)
            mref = 0.5 * btot
            e_up = jnp.exp(b - mref)
            e_dn = jnp.exp(mref - b)
            e_half = jnp.exp(mref)
            qe = q * e_up
            ke = k * e_dn
            qb_ref[d, rows, :] = (qe * e_half).astype(BF16)
            kl_ref[d, rows, :] = (ke * e_half).astype(BF16)
            dec_ref[d, pl.ds(pl.multiple_of(c * SUBLANES, SUBLANES), SUBLANES), :] = jnp.broadcast_to(
                jnp.exp(btot), (SUBLANES, DK_C))
            att = lax.dot_general(qe.astype(BF16), ke.astype(BF16), contract_last,
                                  preferred_element_type=F32) * tri_f[d]
            od = _dot(att.astype(BF16), vc)
            o = od if o is None else o + od
        acc_ref[rows, :] = o
        return carry

    lax.fori_loop(0, n_c, intra, 0, unroll=4)

    st_ref[0] = s0f_ref[0, 0, 0]
    st_ref[1] = s0b_ref[0, 0, 0]

    def inter(ci, carry):
        for d in range(2):
            c = ci if d == 0 else n_c - 1 - ci
            rows = chunk_rows(c)
            st = st_ref[d]
            acc_ref[rows, :] = acc_ref[rows, :] + lax.dot_general(
                qb_ref[d, rows, :], st.astype(BF16), contract_last, preferred_element_type=F32)
            dec = dec_ref[d, pl.ds(pl.multiple_of(c * SUBLANES, SUBLANES), 1), :]
            st_ref[d] = st * dec + lax.dot_general(v_ref[0, rows, :].astype(BF16), kl_ref[d, rows, :],
                                                   contract_first, preferred_element_type=F32)
        return carry

    lax.fori_loop(0, n_c, inter, 0, unroll=2)
    sf_ref[0, 0] = st_ref[0]
    sb_ref[0, 0] = st_ref[1]

    o = acc_ref[...]
    o = o * lax.rsqrt(jnp.mean(o * o, axis=-1, keepdims=True) + EPS) * lng_ref[...]
    o_ref[0] = o * _silu(g_ref[0])


def _gla_mixer(q, k, v, g, gl, gw2, gbias, ln_g, s_init, layer_j):
    B, T, _ = q.shape
    C = GLA_CHUNK
    idx = np.arange(C)
    tri = jnp.asarray(np.stack([idx[:, None] >= idx[None, :], idx[:, None] <= idx[None, :]]).astype(np.float32),
                      dtype=BF16)
    w2 = jnp.stack([_block_diag2(gw2[0][:, h * DK_C:(h + 1) * DK_C], gw2[1][:, h * DK_C:(h + 1) * DK_C])
                    for h in range(H_C)]).astype(BF16)
    gb = jnp.stack([jnp.concatenate([gbias[0][h * DK_C:(h + 1) * DK_C], gbias[1][h * DK_C:(h + 1) * DK_C]])
                    for h in range(H_C)]).reshape(H_C, 1, 2 * DK_C)
    if s_init is None:
        zeros = jnp.zeros((1, 1, 1, DV_C, DK_C), F32)
        s0f = s0b = zeros
        s_spec = pl.BlockSpec((1, 1, 1, DV_C, DK_C), lambda b, h: (0, 0, 0, 0, 0))
    else:
        s0f, s0b = (jnp.swapaxes(s, -1, -2) for s in s_init)
        s_spec = pl.BlockSpec((1, 1, 1, DV_C, DK_C), lambda b, h: (b, layer_j, h, 0, 0))
    o, sf, sb = pl.pallas_call(
        functools.partial(_gla_kernel, T),
        grid=(B, H_C),
        in_specs=[
            pl.BlockSpec((1, T, DK_C), lambda b, h: (b, 0, h)),
            pl.BlockSpec((1, T, DK_C), lambda b, h: (b, 0, h)),
            pl.BlockSpec((1, T, DV_C), lambda b, h: (b, 0, h)),
            pl.BlockSpec((1, T, DV_C), lambda b, h: (b, 0, h)),
            pl.BlockSpec((1, T, 2 * G_LORA), lambda b, h: (b, 0, 0)),
            pl.BlockSpec((1, 2 * G_LORA, 2 * DK_C), lambda b, h: (h, 0, 0)),
            pl.BlockSpec((1, 1, 2 * DK_C), lambda b, h: (h, 0, 0)),
            pl.BlockSpec((1, DV_C), lambda b, h: (0, 0)),
            pl.BlockSpec((2, C, C), lambda b, h: (0, 0, 0)),
            s_spec, s_spec,
        ],
        out_specs=[
            pl.BlockSpec((1, T, DV_C), lambda b, h: (b, 0, h)),
            pl.BlockSpec((1, 1, DV_C, DK_C), lambda b, h: (b, h, 0, 0)),
            pl.BlockSpec((1, 1, DV_C, DK_C), lambda b, h: (b, h, 0, 0)),
        ],
        out_shape=[
            jax.ShapeDtypeStruct((B, T, H_C * DV_C), F32),
            jax.ShapeDtypeStruct((B, H_C, DV_C, DK_C), F32),
            jax.ShapeDtypeStruct((B, H_C, DV_C, DK_C), F32),
        ],
        scratch_shapes=[
            pltpu.VMEM((T, 2 * DK_C), F32), pltpu.VMEM((T, DV_C), F32),
            pltpu.VMEM((2, T, DK_C), BF16), pltpu.VMEM((2, T, DK_C), BF16),
            pltpu.VMEM((2, T // C * SUBLANES, DK_C), F32), pltpu.VMEM((2, DV_C, DK_C), F32),
        ],
        compiler_params=_params("parallel", "parallel"),
    )(q, k, v, g, gl, w2, gb, ln_g.reshape(1, -1), tri, s0f, s0b)
    return o, (jnp.swapaxes(sf, -1, -2), jnp.swapaxes(sb, -1, -2))


def kernel(x_prompt, x_sample, c, cache_attn_k, cache_attn_v, state_rwkv_fwd, state_rwkv_bwd, state_gla_fwd, state_gla_bwd, c_ctx, norm_g, mod_w, mod_b, ev_w_in, ev_w_out, ev_qn_g, ev_kn_g, ev_shift_mu, rw_w0, rw_w2, rw_a0, rw_a2, rw_kk, rw_ka, rw_rk, rw_ln_g, rw_ln_b, od_w_in, od_w_out, gla_w2, gla_b, gla_ln_g, final_g):
    n_dec = c.shape[0]
    cond = jnp.concatenate([c_ctx[None], c, jnp.zeros((SUBLANES - 1 - n_dec, D_MODEL), F32)], axis=0)
    mod = _modulation(cond, mod_w, mod_b)

    ev_in = ev_w_in.astype(BF16)
    ev_out = ev_w_out.astype(BF16)
    od_in = od_w_in.astype(BF16)
    od_out = od_w_out.astype(BF16)
    kw = KV_A * HD_A
    ck = cache_attn_k.reshape(cache_attn_k.shape[:3] + (kw,))
    cv = cache_attn_v.reshape(cache_attn_v.shape[:3] + (kw,))
    gla_f_t = state_gla_fwd
    gla_b_t = state_gla_bwd

    def trunk(x, latent):
        new = {"k": [], "v": [], "rf": [], "rb": [], "gf": [], "gb": []}
        for i in range(DEPTH):
            j = i // 2
            fg = final_g if i == DEPTH - 1 else None
            if i % 2 == 0:
                qa, ka, va, ga, zb, gb = _inproj(x, norm_g[i], mod, i, ev_in[j], EV_SPLITS, latent)
                if latent:
                    o_a, _ = _attention(qa, ka, va, ga, ev_qn_g[j], ev_kn_g[j], ck, cv, j)
                    s_init = (state_rwkv_fwd[:, j], state_rwkv_bwd[:, j])
                else:
                    o_a, kn = _attention(qa, ka, va, ga, ev_qn_g[j], ev_kn_g[j])
                    s_init = None
                    new["k"].append(kn)
                    new["v"].append(va)
                p = dict(shift_mu=ev_shift_mu[j], w0=rw_w0[j], w2=rw_w2[j], a0=rw_a0[j], a2=rw_a2[j],
                         k_k=rw_kk[j], k_a=rw_ka[j], r_k=rw_rk[j], ln_g=rw_ln_g[j], ln_b=rw_ln_b[j])
                o_b, fin = _rwkv_mixer(zb, gb, p, s_init)
                new["rf"].append(fin[0])
                new["rb"].append(fin[1])
                x = _outproj([o_a, o_b], [ev_out[j, :D_HALF], ev_out[j, D_HALF:]], x, mod, i, latent, fg)
            else:
                q, k, v, g, gl = _inproj(x, norm_g[i], mod, i, od_in[j], OD_SPLITS, latent)
                s_init = (gla_f_t, gla_b_t) if latent else None
                o, fin = _gla_mixer(q, k, v, g, gl, gla_w2[j], gla_b[j], gla_ln_g[j], s_init, j)
                new["gf"].append(fin[0])
                new["gb"].append(fin[1])
                x = _outproj([o], [od_out[j]], x, mod, i, latent, fg)
        return x, new

    y_prompt, new = trunk(x_prompt, False)
    y_sample, _ = trunk(x_sample, True)
    B, T = x_prompt.shape[:2]
    heads = lambda t: t.reshape(B, T, KV_A, HD_A)
    return (y_prompt, y_sample,
            jnp.stack([heads(t) for t in new["k"]], axis=1), jnp.stack([heads(t) for t in new["v"]], axis=1),
            jnp.stack(new["rf"], axis=1), jnp.stack(new["rb"], axis=1),
            jnp.stack(new["gf"], axis=1), jnp.stack(new["gb"], axis=1))
```

```python
import functools

import numpy as np
import jax
import jax.numpy as jnp
from jax import lax
from jax.experimental import pallas as pl
from jax.experimental.pallas import tpu as pltpu

F32 = jnp.float32
BF16 = jnp.bfloat16

D_MODEL = 1024
DEPTH = 4
GRID_W = 64
D_HALF = D_MODEL // 2
HD_A = 64
H_A = D_HALF // HD_A
KV_A = H_A // 4
ROPE_BASE = 10000.0
HS_B = 64
H_B = D_HALF // HS_B
W_LORA = 64
A_LORA = 64
RWKV_DECAY_SCALE = 0.606531
GN_EPS = 64e-5
B_SHIFT = 3 * D_HALF + 2 * W_LORA + 2 * A_LORA
H_C = 4
DK_C = D_MODEL // 2 // H_C
DV_C = D_MODEL // H_C
G_LORA = 16
GLA_TAU = 16.0
EPS = 1e-6

EV_SPLITS = (H_A * HD_A, KV_A * HD_A, KV_A * HD_A, D_HALF, B_SHIFT, D_HALF)
OD_SPLITS = (H_C * DK_C, H_C * DK_C, D_MODEL, D_MODEL, 2 * G_LORA)

LANES = 128
SUBLANES = 8
VMEM_LIMIT_BYTES = 56 * 1024 * 1024

ROW_TILE = 256
Q_TILE = 256
GLA_CHUNK = 64
SCAN_T = 8


def _params(*sem):
    return pltpu.CompilerParams(dimension_semantics=sem, vmem_limit_bytes=VMEM_LIMIT_BYTES)


def _silu(x):
    return x * jax.nn.sigmoid(x)


def _dot(a, b):
    return jnp.dot(a, b, preferred_element_type=F32)


def _seg_sum(x, ones_bd):
    hi = x.astype(BF16)
    r1 = x - hi.astype(F32)
    mid = r1.astype(BF16)
    lo = (r1 - mid.astype(F32)).astype(BF16)
    return _dot(hi, ones_bd) + _dot(mid, ones_bd) + _dot(lo, ones_bd)


def _get(ref):
    return ref[...].reshape(ref.shape[-2:])


def _put(ref, val):
    ref[...] = val.reshape(ref.shape)


def _block_diag_ones(n, blk):
    i = np.arange(n) // blk
    return jnp.asarray((i[:, None] == i[None, :]).astype(np.float32), dtype=BF16)


def _mod_kernel(cond_ref, w_ref, b_ref, o_ref):
    s = _silu(cond_ref[...])
    o_ref[0] = _dot(s.astype(BF16), w_ref[0].astype(BF16)) + b_ref[0]


def _modulation(cond, mod_w, mod_b):
    n = cond.shape[0]
    return pl.pallas_call(
        _mod_kernel,
        grid=(DEPTH, 3),
        in_specs=[
            pl.BlockSpec((n, D_MODEL), lambda i, j: (0, 0)),
            pl.BlockSpec((1, D_MODEL, D_MODEL), lambda i, j: (i, 0, j)),
            pl.BlockSpec((1, 1, D_MODEL), lambda i, j: (i, 0, j)),
        ],
        out_specs=pl.BlockSpec((1, n, D_MODEL), lambda i, j: (i, 0, j)),
        out_shape=jax.ShapeDtypeStruct((DEPTH, n, 3 * D_MODEL), F32),
        compiler_params=_params("parallel", "parallel"),
    )(cond, mod_w, mod_b.reshape(DEPTH, 1, 3 * D_MODEL))


def _mod_row(latent):
    return (1 + pl.program_id(0)) if latent else 0


def _inproj_kernel(latent, splits, x_ref, g_ref, sh_ref, sc_ref, w_ref, *out_refs):
    r = _mod_row(latent)
    x = x_ref[0]
    y = x * lax.rsqrt(jnp.mean(x * x, axis=-1, keepdims=True) + EPS) * g_ref[...]
    shift = sh_ref[0, pl.ds(r, 1), :]
    scale = sc_ref[0, pl.ds(r, 1), :]
    h = (y * (1.0 + scale) + shift).astype(BF16)
    off = 0
    for o_ref, n in zip(out_refs, splits):
        o_ref[0] = _dot(h, w_ref[:, off:off + n])
        off += n


def _inproj(x, norm_g, mod, layer, w_bf16, splits, latent):
    B, T, _ = x.shape
    cols = w_bf16.shape[1]
    nrow = mod.shape[1]
    return pl.pallas_call(
        functools.partial(_inproj_kernel, latent, splits),
        grid=(B, T // ROW_TILE),
        in_specs=[
            pl.BlockSpec((1, ROW_TILE, D_MODEL), lambda b, i: (b, i, 0)),
            pl.BlockSpec((1, D_MODEL), lambda b, i: (0, 0)),
            pl.BlockSpec((1, nrow, D_MODEL), lambda b, i: (layer, 0, 0)),
            pl.BlockSpec((1, nrow, D_MODEL), lambda b, i: (layer, 0, 1)),
            pl.BlockSpec((D_MODEL, cols), lambda b, i: (0, 0)),
        ],
        out_specs=[pl.BlockSpec((1, ROW_TILE, n), lambda b, i: (b, i, 0)) for n in splits],
        out_shape=[jax.ShapeDtypeStruct((B, T, n), F32) for n in splits],
        compiler_params=_params("parallel", "parallel"),
    )(x, norm_g.reshape(1, D_MODEL), mod, mod, w_bf16)


def _outproj_kernel(latent, final, n_in, *refs):
    o_refs = refs[:n_in]
    w_refs = refs[n_in:2 * n_in]
    x_ref, gate_ref = refs[2 * n_in], refs[2 * n_in + 1]
    rest = refs[2 * n_in + 2:]
    r = _mod_row(latent)
    acc = _dot(o_refs[0][0].astype(BF16), w_refs[0][...])
    for o_ref, w_ref in zip(o_refs[1:], w_refs[1:]):
        acc = acc + _dot(o_ref[0].astype(BF16), w_ref[...])
    y = x_ref[0] + gate_ref[0, pl.ds(r, 1), :] * acc
    if final:
        fg_ref, out_ref = rest
        y = y * lax.rsqrt(jnp.mean(y * y, axis=-1, keepdims=True) + EPS) * fg_ref[...]
    else:
        (out_ref,) = rest
    out_ref[0] = y


def _outproj(outs, ws_bf16, x, mod, layer, latent, final_g=None):
    B, T, _ = x.shape
    nrow = mod.shape[1]
    n_in = len(outs)
    final = final_g is not None
    in_specs = [pl.BlockSpec((1, ROW_TILE, o.shape[-1]), lambda b, i: (b, i, 0)) for o in outs]
    in_specs += [pl.BlockSpec(w.shape, lambda b, i: (0, 0)) for w in ws_bf16]
    in_specs += [
        pl.BlockSpec((1, ROW_TILE, D_MODEL), lambda b, i: (b, i, 0)),
        pl.BlockSpec((1, nrow, D_MODEL), lambda b, i: (layer, 0, 2)),
    ]
    args = list(outs) + list(ws_bf16) + [x, mod]
    if final:
        in_specs.append(pl.BlockSpec((1, D_MODEL), lambda b, i: (0, 0)))
        args.append(final_g.reshape(1, D_MODEL))
    return pl.pallas_call(
        functools.partial(_outproj_kernel, latent, final, n_in),
        grid=(B, T // ROW_TILE),
        in_specs=in_specs,
        out_specs=pl.BlockSpec((1, ROW_TILE, D_MODEL), lambda b, i: (b, i, 0)),
        out_shape=jax.ShapeDtypeStruct((B, T, D_MODEL), F32),
        compiler_params=_params("parallel", "parallel"),
    )(*args)


def _rope_tables(T):
    n_rows = T // GRID_W
    row = jnp.repeat(jnp.arange(n_rows), GRID_W).astype(F32)
    col = jnp.tile(jnp.arange(GRID_W), n_rows).astype(F32)
    n_freq = HD_A // 4
    inv = ROPE_BASE ** (-jnp.arange(n_freq, dtype=F32) / n_freq)
    ang_r = row[:, None] * inv
    ang_c = col[:, None] * inv
    zero = jnp.zeros_like(ang_r)
    cos = jnp.concatenate([jnp.cos(ang_r), jnp.cos(ang_r), jnp.cos(ang_c), jnp.cos(ang_c)], axis=1)
    s1 = jnp.concatenate([-jnp.sin(ang_r), zero, -jnp.sin(ang_c), zero], axis=1)
    s2 = jnp.concatenate([zero, jnp.sin(ang_r), zero, jnp.sin(ang_c)], axis=1)
    return cos, s1, s2


def _rope(x, cos, s1, s2):
    n = x.shape[-1]
    q = HD_A // 4
    return x * cos + pltpu.roll(x, n - q, 1) * s1 + pltpu.roll(x, q, 1) * s2


def _attn_kernel(latent, S, *refs):
    if latent:
        (q_ref, k_ref, v_ref, ga_ref, qg_ref, kg_ref, bd_ref, cos_ref, s1_ref, s2_ref,
         cosk_ref, s1k_ref, s2k_ref, ck_ref, cv_ref, o_ref, km_ref, vm_ref) = refs
    else:
        (q_ref, k_ref, v_ref, ga_ref, qg_ref, kg_ref, bd_ref, o_ref, kn_ref, km_ref, vm_ref) = refs
    bd = bd_ref[...]
    inv_d = 1.0 / HD_A
    kw = KV_A * HD_A

    @pl.when(pl.program_id(1) == 0)
    def _():
        k = k_ref[0]
        kn = k * lax.rsqrt(_seg_sum(k * k, bd[:kw, :kw]) * inv_d + EPS) * kg_ref[...]
        v = v_ref[0]
        if latent:
            kn = _rope(kn, cosk_ref[...], s1k_ref[...], s2k_ref[...])
            k_all = jnp.concatenate([ck_ref[0, 0], kn], axis=0)
            v_all = jnp.concatenate([cv_ref[0, 0], v], axis=0)
        else:
            kn_ref[0] = kn
            k_all, v_all = kn, v
        lane = lax.broadcasted_iota(jnp.int32, (S, kw), 1)
        k_sw = pltpu.roll(k_all, HD_A, 1)
        v_sw = pltpu.roll(v_all, HD_A, 1)
        for j in range(KV_A):
            for half in range(2):
                keep = (lane < HD_A) if half == 0 else (lane >= HD_A)
                src_k, src_v = (k_all, v_all) if j == half else (k_sw, v_sw)
                km_ref[2 * j + half] = jnp.where(keep, src_k, 0.0).astype(BF16)
                vm_ref[2 * j + half] = jnp.where(keep, src_v, 0.0).astype(BF16)

    q = q_ref[0]
    qn = q * lax.rsqrt(_seg_sum(q * q, bd) * inv_d + EPS) * qg_ref[...]
    if latent:
        qn = _rope(qn, cos_ref[...], s1_ref[...], s2_ref[...])
    qb = (qn * HD_A ** -0.5).astype(BF16)
    for m in range(H_A // 2):
        blk = slice(m * LANES, (m + 1) * LANES)
        qs = qb[:, blk]
        acc = None
        for half in range(2):
            j = (2 * m + half) // (H_A // KV_A)
            s = lax.dot_general(qs, km_ref[2 * j + half], (((1,), (1,)), ((), ())),
                                preferred_element_type=F32)
            e = jnp.exp(s - jnp.max(s, axis=-1, keepdims=True))
            l = jnp.sum(e, axis=-1, keepdims=True)
            pv = _dot(e.astype(BF16), vm_ref[2 * j + half]) * (1.0 / l)
            acc = pv if acc is None else acc + pv
        o_ref[0, :, blk] = acc * _silu(ga_ref[0, :, blk])


def _attention(qa, ka, va, ga, qn_g, kn_g, ctx_k=None, ctx_v=None, layer_j=0):
    B, T, _ = qa.shape
    latent = ctx_k is not None
    S = T + (ctx_k.shape[2] if latent else 0)
    kw = KV_A * HD_A
    qw = H_A * HD_A
    bd = _block_diag_ones(qw, HD_A)
    qblk = pl.BlockSpec((1, Q_TILE, qw), lambda b, i: (b, i, 0))
    kblk = pl.BlockSpec((1, T, kw), lambda b, i: (b, 0, 0))
    in_specs = [qblk, kblk, kblk, qblk,
                pl.BlockSpec((1, qw), lambda b, i: (0, 0)),
                pl.BlockSpec((1, kw), lambda b, i: (0, 0)),
                pl.BlockSpec(bd.shape, lambda b, i: (0, 0))]
    args = [qa, ka, va, ga, jnp.tile(qn_g, H_A).reshape(1, -1), jnp.tile(kn_g, KV_A).reshape(1, -1), bd]
    out_specs = [qblk]
    out_shape = [jax.ShapeDtypeStruct((B, T, qw), F32)]
    if latent:
        tabs = _rope_tables(T)
        P = ctx_k.shape[2]
        in_specs += [pl.BlockSpec((Q_TILE, qw), lambda b, i: (i, 0))] * 3
        in_specs += [pl.BlockSpec((T, kw), lambda b, i: (0, 0))] * 3
        in_specs += [pl.BlockSpec((1, 1, P, kw), lambda b, i: (b, layer_j, 0, 0))] * 2
        args += [jnp.tile(t, (1, H_A)) for t in tabs] + [jnp.tile(t, (1, KV_A)) for t in tabs] + [ctx_k, ctx_v]
    else:
        out_specs.append(kblk)
        out_shape.append(jax.ShapeDtypeStruct((B, T, kw), F32))
    res = pl.pallas_call(
        functools.partial(_attn_kernel, latent, S),
        grid=(B, T // Q_TILE),
        in_specs=in_specs,
        out_specs=out_specs,
        out_shape=out_shape,
        scratch_shapes=[pltpu.VMEM((2 * KV_A, S, kw), BF16), pltpu.VMEM((2 * KV_A, S, kw), BF16)],
        compiler_params=_params("parallel", "arbitrary"),
    )(*args)
    return res if not latent else (res[0], None)


def _rwkv_prep_kernel(n_t, z_ref, zp_ref, zn_ref, mu_ref, w0_ref, a0_ref, w2_ref, a2_ref,
                      kkg_ref, ka_ref, rk_ref, bd_ref,
                      wf_ref, wb_ref, kdf_ref, kdb_ref, kaf_ref, kab_ref, nkk_ref, r_ref, v_ref, bonus_ref):
    i = pl.program_id(1)
    z = z_ref[0]
    n = z.shape[0]
    prev_row = jnp.where(i > 0, zp_ref[0, SUBLANES - 1:SUBLANES, :], 0.0)
    next_row = jnp.where(i < n_t - 1, zn_ref[0, 0:1, :], 0.0)
    rows = lax.broadcasted_iota(jnp.int32, (n, 1), 0)
    zp = jnp.where(rows == 0, prev_row, pltpu.roll(z, 1, 0))
    zn = jnp.where(rows == n - 1, next_row, pltpu.roll(z, n - 1, 0))
    zs = z + mu_ref[...] * (0.5 * (zp + zn) - z)

    rb = zs[:, 0:D_HALF]
    kb = zs[:, D_HALF:2 * D_HALF]
    vb = zs[:, 2 * D_HALF:3 * D_HALF]
    lw = zs[:, 3 * D_HALF:3 * D_HALF + 2 * W_LORA]
    la = zs[:, 3 * D_HALF + 2 * W_LORA:]
    w = jnp.exp(-RWKV_DECAY_SCALE * jax.nn.sigmoid(w0_ref[...] + _dot(jnp.tanh(lw).astype(BF16), w2_ref[...])))
    a = jax.nn.sigmoid(a0_ref[...] + _dot(la.astype(BF16), a2_ref[...]))
    bd = bd_ref[...]
    kk = kb * kkg_ref[...]
    kk = kk * lax.rsqrt(_seg_sum(kk * kk, bd) + 1e-12)
    ka = ka_ref[...]
    a_f, a_b = a[:, :D_HALF], a[:, D_HALF:]
    kd_f = kb * (1.0 + (a_f - 1.0) * ka)
    kd_b = kb * (1.0 + (a_b - 1.0) * ka)
    _put(wf_ref, w[:, :D_HALF])
    _put(wb_ref, w[:, D_HALF:])
    _put(kdf_ref, kd_f)
    _put(kdb_ref, kd_b)
    _put(kaf_ref, kk * a_f)
    _put(kab_ref, kk * a_b)
    _put(nkk_ref, -kk)
    _put(r_ref, rb)
    _put(v_ref, vb)
    _put(bonus_ref, _seg_sum(rb * rk_ref[...] * (kd_f + kd_b), bd) * vb)


def _block_diag2(m0, m1):
    z = jnp.zeros_like(m0)
    return jnp.concatenate([jnp.concatenate([m0, z], axis=1), jnp.concatenate([z, m1], axis=1)], axis=0)


def _rwkv_prep(zb, shift_mu, w0, w2, a0, a2, k_k, k_a, r_k):
    B, T, _ = zb.shape
    n_t = T // ROW_TILE
    per_tile = ROW_TILE // SUBLANES
    bd = _block_diag_ones(D_HALF, HS_B)
    row = lambda x: x.reshape(1, -1)
    vec = pl.BlockSpec((1, D_HALF), lambda b, i: (0, 0))
    vec2 = pl.BlockSpec((1, 2 * D_HALF), lambda b, i: (0, 0))
    out = pl.BlockSpec((1, ROW_TILE, D_HALF), lambda b, i: (b, i, 0))
    return pl.pallas_call(
        functools.partial(_rwkv_prep_kernel, n_t),
        grid=(B, n_t),
        in_specs=[
            pl.BlockSpec((1, ROW_TILE, B_SHIFT), lambda b, i: (b, i, 0)),
            pl.BlockSpec((1, SUBLANES, B_SHIFT), lambda b, i: (b, jnp.maximum(i * per_tile - 1, 0), 0)),
            pl.BlockSpec((1, SUBLANES, B_SHIFT),
                         lambda b, i: (b, jnp.minimum((i + 1) * per_tile, n_t * per_tile - 1), 0)),
            pl.BlockSpec((1, B_SHIFT), lambda b, i: (0, 0)),
            vec2, vec2,
            pl.BlockSpec((2 * W_LORA, 2 * D_HALF), lambda b, i: (0, 0)),
            pl.BlockSpec((2 * A_LORA, 2 * D_HALF), lambda b, i: (0, 0)),
            vec, vec, vec,
            pl.BlockSpec(bd.shape, lambda b, i: (0, 0)),
        ],
        out_specs=[out] * 10,
        out_shape=[jax.ShapeDtypeStruct((B, T, D_HALF), F32)] * 10,
        compiler_params=_params("parallel", "parallel"),
    )(zb, zb, zb, row(shift_mu), row(w0), row(a0),
      _block_diag2(w2[0], w2[1]).astype(BF16), _block_diag2(a2[0], a2[1]).astype(BF16),
      row(k_k), row(k_a), row(r_k), bd)


def _scan_steps(vh, n_steps, reverse, kvecs, vblock, yput, s_ref):
    def t_step(i, carry):
        t = (n_steps - 1 - i) if reverse else i
        w, nkk, kka, kd, r = kvecs(t)

        def v_step(g, c):
            base = pl.multiple_of(g * SUBLANES, SUBLANES)
            vblk = vblock(t, base)
            ys = []
            for u in range(SUBLANES):
                s = s_ref[base + u]
                sa = jnp.sum(s * nkk, axis=0, keepdims=True)
                s = s * w + sa * kka + vblk[u:u + 1, :] * kd
                s_ref[base + u] = s
                ys.append(jnp.sum(s * r, axis=0, keepdims=True))
            yput(t, base, jnp.concatenate(ys, axis=0))
            return c

        lax.fori_loop(0, vh // SUBLANES, v_step, 0)
        return carry

    lax.fori_loop(0, n_steps, t_step, 0)


def _rwkv_scan_kernel(vh, reverse, w_ref, nkk_ref, kka_ref, kd_ref, r_ref, v_ref, s0_ref, y_ref, s_ref):
    @pl.when(pl.program_id(1) == 0)
    def _():
        s_ref[...] = s0_ref[...]

    def yput(t, base, val):
        y_ref[t, pl.ds(base, SUBLANES), :] = val

    _scan_steps(vh, SCAN_T, reverse,
                lambda t: (w_ref[t], nkk_ref[t], kka_ref[t], kd_ref[t], r_ref[t]),
                lambda t, base: v_ref[t, pl.ds(base, SUBLANES), :], yput, s_ref)


def _rwkv_scan(w, nkk, kka, kd, r, v, s0, reverse):
    T, _, C = w.shape
    vh = v.shape[1]
    n_t = T // SCAN_T
    tblk = (lambda t: n_t - 1 - t) if reverse else (lambda t: t)
    kvec = pl.BlockSpec((SCAN_T, HS_B, LANES), lambda g, t: (tblk(t), 0, g))
    vvec = pl.BlockSpec((SCAN_T, vh, LANES), lambda g, t: (tblk(t), 0, g))
    state = pl.BlockSpec((vh, HS_B, LANES), lambda g, t: (0, 0, g))
    return pl.pallas_call(
        functools.partial(_rwkv_scan_kernel, vh, reverse),
        grid=(C // LANES, n_t),
        in_specs=[kvec] * 5 + [vvec, state],
        out_specs=[vvec, state],
        out_shape=[jax.ShapeDtypeStruct((T, vh, C), F32), jax.ShapeDtypeStruct((vh, HS_B, C), F32)],
        compiler_params=_params("parallel", "arbitrary"),
    )(w, nkk, kka, kd, r, v, s0)


HEADS_PER_GROUP = LANES // 32


def _chain_pair(x1, x2):
    lo = lax.broadcasted_iota(jnp.int32, (x1.shape[0], LANES), 1) < HS_B
    pieces = []
    for h in range(HEADS_PER_GROUP):
        blk = slice((h // 2) * LANES, (h // 2 + 1) * LANES)
        a, b = x1[:, blk], x2[:, blk]
        if h % 2 == 0:
            pieces.append(jnp.where(lo, a, pltpu.roll(b, HS_B, 1)))
        else:
            pieces.append(jnp.where(lo, pltpu.roll(a, HS_B, 1), b))
    return jnp.concatenate(pieces, axis=0).T


def _unchain_pair(y0, y1):
    yt = jnp.concatenate([y0, y1], axis=0).T
    nb = yt.shape[0] // HEADS_PER_GROUP
    lo = lax.broadcasted_iota(jnp.int32, (nb, LANES), 1) < HS_B
    rows = [yt[h * nb:(h + 1) * nb] for h in range(HEADS_PER_GROUP)]
    out0, out1 = [], []
    for m in range(HEADS_PER_GROUP // 2):
        even, odd = rows[2 * m], rows[2 * m + 1]
        out0.append(jnp.where(lo, even, pltpu.roll(odd, HS_B, 1)))
        out1.append(jnp.where(lo, pltpu.roll(even, HS_B, 1), odd))
    return jnp.concatenate(out0, axis=1), jnp.concatenate(out1, axis=1)


def _rwkv_scan_tm_kernel(reverse, w_ref, nkk_ref, kka_ref, kd_ref, r_ref, v_ref, s0_ref, y_ref, s_ref,
                         kbuf, ybuf):
    @pl.when(pl.program_id(1) == 0)
    def _():
        s_ref[...] = s0_ref[...]

    pairs = ((w_ref, nkk_ref), (kka_ref, kd_ref), (r_ref, v_ref))
    for t in range(SCAN_T):
        for p, (a_ref, b_ref) in enumerate(pairs):
            both = _chain_pair(a_ref[:, t, :], b_ref[:, t, :])
            kbuf[2 * p, t] = both[:HS_B]
            kbuf[2 * p + 1, t] = both[HS_B:]

    def yput(t, base, val):
        ybuf[t, pl.ds(base, SUBLANES), :] = val

    _scan_steps(HS_B, SCAN_T, reverse,
                lambda t: (kbuf[0, t], kbuf[1, t], kbuf[2, t], kbuf[3, t], kbuf[4, t]),
                lambda t, base: kbuf[5, t, pl.ds(base, SUBLANES), :], yput, s_ref)

    for t in range(0, SCAN_T, 2):
        y_ref[:, t, :], y_ref[:, t + 1, :] = _unchain_pair(ybuf[t], ybuf[t + 1])


def _rwkv_scan_tm(w, nkk, kka, kd, r, v, s0, reverse):
    B, T, W = w.shape
    gw = HEADS_PER_GROUP * HS_B
    n_t = T // SCAN_T
    tblk = (lambda t: n_t - 1 - t) if reverse else (lambda t: t)
    vec = pl.BlockSpec((B, SCAN_T, gw), lambda g, t: (0, tblk(t), g))
    state = pl.BlockSpec((HS_B, HS_B, LANES), lambda g, t: (0, 0, g))
    return pl.pallas_call(
        functools.partial(_rwkv_scan_tm_kernel, reverse),
        grid=(W // gw, n_t),
        in_specs=[vec] * 6 + [state],
        out_specs=[vec, state],
        out_shape=[jax.ShapeDtypeStruct((B, T, W), F32), jax.ShapeDtypeStruct((HS_B, HS_B, W // gw * LANES), F32)],
        scratch_shapes=[pltpu.VMEM((6, SCAN_T, HS_B, LANES), F32), pltpu.VMEM((SCAN_T, HS_B, LANES), F32)],
        compiler_params=_params("parallel", "arbitrary"),
    )(w, nkk, kka, kd, r, v, s0)


def _rwkv_post_kernel(yf_ref, yb_ref, bonus_ref, gb_ref, lng_ref, lnb_ref, bd_ref, o_ref):
    bd = bd_ref[...]
    y = _get(yf_ref) + _get(yb_ref)
    inv_n = 1.0 / HS_B
    d = y - _seg_sum(y, bd) * inv_n
    var = _seg_sum(d * d, bd) * inv_n
    yn = d * lax.rsqrt(var + GN_EPS) * lng_ref[...] + lnb_ref[...]
    o_ref[0] = (yn + bonus_ref[0]) * _silu(gb_ref[0])


def _rwkv_post(y_f, y_b, bonus, gb, ln_g, ln_b):
    B, T, _ = y_f.shape
    bd = _block_diag_ones(D_HALF, HS_B)
    blk = pl.BlockSpec((1, ROW_TILE, D_HALF), lambda b, i: (b, i, 0))
    vec = pl.BlockSpec((1, D_HALF), lambda b, i: (0, 0))
    return pl.pallas_call(
        _rwkv_post_kernel,
        grid=(B, T // ROW_TILE),
        in_specs=[blk, blk, blk, blk, vec, vec, pl.BlockSpec(bd.shape, lambda b, i: (0, 0))],
        out_specs=blk,
        out_shape=jax.ShapeDtypeStruct((B, T, D_HALF), F32),
        compiler_params=_params("parallel", "parallel"),
    )(y_f, y_b, bonus, gb, ln_g.reshape(1, -1), ln_b.reshape(1, -1), bd)


def _rwkv_mixer(zb, gb, p, s_init):
    B, T, _ = zb.shape
    nbh = B * H_B
    (w_f, w_b, kd_f, kd_b, ka_f, ka_b, nkk, r, v, bonus) = _rwkv_prep(
        zb, p["shift_mu"], p["w0"], p["w2"], p["a0"], p["a2"], p["k_k"], p["k_a"], p["r_k"])
    ys, fin = [], []
    if B * HEADS_PER_GROUP == LANES:
        for d, (w_d, ka_d, kd_d) in enumerate(((w_f, ka_f, kd_f), (w_b, ka_b, kd_b))):
            if s_init is None:
                s0 = jnp.zeros((HS_B, HS_B, nbh), F32)
            else:
                s0 = s_init[d].transpose(2, 3, 1, 0).reshape(HS_B, HS_B, nbh)
            y, s_fin = _rwkv_scan_tm(w_d, nkk, ka_d, kd_d, r, v, s0, reverse=(d == 1))
            ys.append(y)
            fin.append(s_fin.reshape(HS_B, HS_B, H_B, B).transpose(3, 2, 0, 1))
        return _rwkv_post(ys[0], ys[1], bonus, gb, p["ln_g"], p["ln_b"]), fin

    vs = max(LANES // nbh, 1)
    vh = HS_B // vs

    def kchains(x):
        x = x.reshape(B, T, H_B, HS_B).transpose(1, 3, 0, 2).reshape(T, HS_B, 1, nbh)
        return jnp.broadcast_to(x, (T, HS_B, vs, nbh)).reshape(T, HS_B, vs * nbh)

    def vchains(x):
        x = x.reshape(B, T, H_B, vs, vh).transpose(1, 4, 3, 0, 2)
        return x.reshape(T, vh, vs * nbh)

    def unchain(y):
        return y.reshape(T, vh, vs, B, H_B).transpose(3, 0, 4, 2, 1).reshape(B, T, H_B * HS_B)

    nkk_c, r_c, v_c = kchains(nkk), kchains(r), vchains(v)
    for d, (w_d, ka_d, kd_d) in enumerate(((w_f, ka_f, kd_f), (w_b, ka_b, kd_b))):
        if s_init is None:
            s0 = jnp.zeros((vh, HS_B, vs * nbh), F32)
        else:
            s0 = s_init[d].reshape(B, H_B, vs, vh, HS_B).transpose(3, 4, 2, 0, 1).reshape(vh, HS_B, vs * nbh)
        y, s_fin = _rwkv_scan(kchains(w_d), nkk_c, kchains(ka_d), kchains(kd_d), r_c, v_c, s0, reverse=(d == 1))
        ys.append(unchain(y))
        fin.append(s_fin.reshape(vh, HS_B, vs, B, H_B).transpose(3, 4, 2, 0, 1).reshape(B, H_B, HS_B, HS_B))
    o_b = _rwkv_post(ys[0], ys[1], bonus, gb, p["ln_g"], p["ln_b"])
    return o_b, fin


def _gla_kernel(T, q_ref, k_ref, v_ref, g_ref, gl_ref, w2_ref, gb_ref, lng_ref, tri_ref, s0f_ref, s0b_ref,
                o_ref, sf_ref, sb_ref, la_ref, acc_ref, qb_ref, kl_ref, dec_ref, st_ref):
    C = GLA_CHUNK
    n_c = T // C
    pre = _dot(gl_ref[0].astype(BF16), w2_ref[0]) + gb_ref[0]
    la_ref[...] = jax.nn.log_sigmoid(pre) * (1.0 / GLA_TAU)
    qscale = DK_C ** -0.5
    contract_last = (((1,), (1,)), ((), ()))
    contract_first = (((0,), (0,)), ((), ()))
    tri = [tri_ref[0], tri_ref[1]]
    tri_f = [t.astype(F32) for t in tri]

    def chunk_rows(c):
        return pl.ds(pl.multiple_of(c * C, C), C)

    def intra(c, carry):
        rows = chunk_rows(c)
        q = q_ref[0, rows, :] * qscale
        k = k_ref[0, rows, :]
        vc = v_ref[0, rows, :].astype(BF16)
        o = None
        for d in range(2):
            la = la_ref[rows, d * DK_C:(d + 1) * DK_C]
            hi = la.astype(BF16)
            r1 = la - hi.astype(F32)
            mid = r1.astype(BF16)
            lo = (r1 - mid.astype(F32)).astype(BF16)
            b3 = _dot(tri[d], jnp.concatenate([hi, mid, lo], axis=1))
            b = b3[:, :DK_C] + b3[:, DK_C:2 * DK_C] + b3[:, 2 * DK_C:]
            btot = jnp.sum(la, axis=0, keepdims=True)
            mref = 0.5 * btot
            e_up = jnp.exp(b - mref)
            e_dn = jnp.exp(mref - b)
            e_half = jnp.exp(mref)
            qe = q * e_up
            ke = k * e_dn
            qb_ref[d, rows, :] = (qe * e_half).astype(BF16)
            kl_ref[d, rows, :] = (ke * e_half).astype(BF16)
            dec_ref[d, pl.ds(pl.multiple_of(c * SUBLANES, SUBLANES), SUBLANES), :] = jnp.broadcast_to(
                jnp.exp(btot), (SUBLANES, DK_C))
            att = lax.dot_general(qe.astype(BF16), ke.astype(BF16), contract_last,
                                  preferred_element_type=F32) * tri_f[d]
            od = _dot(att.astype(BF16), vc)
            o = od if o is None else o + od
        acc_ref[rows, :] = o
        return carry

    lax.fori_loop(0, n_c, intra, 0, unroll=4)

    st_ref[0] = s0f_ref[0, 0, 0]
    st_ref[1] = s0b_ref[0, 0, 0]

    def inter(ci, carry):
        for d in range(2):
            c = ci if d == 0 else n_c - 1 - ci
            rows = chunk_rows(c)
            st = st_ref[d]
            acc_ref[rows, :] = acc_ref[rows, :] + lax.dot_general(
                qb_ref[d, rows, :], st.astype(BF16), contract_last, preferred_element_type=F32)
            dec = dec_ref[d, pl.ds(pl.multiple_of(c * SUBLANES, SUBLANES), 1), :]
            st_ref[d] = st * dec + lax.dot_general(v_ref[0, rows, :].astype(BF16), kl_ref[d, rows, :],
                                                   contract_first, preferred_element_type=F32)
        return carry

    lax.fori_loop(0, n_c, inter, 0, unroll=2)
    sf_ref[0, 0] = st_ref[0]
    sb_ref[0, 0] = st_ref[1]

    o = acc_ref[...]
    o = o * lax.rsqrt(jnp.mean(o * o, axis=-1, keepdims=True) + EPS) * lng_ref[...]
    o_ref[0] = o * _silu(g_ref[0])


def _gla_mixer(q, k, v, g, gl, gw2, gbias, ln_g, s_init, layer_j):
    B, T, _ = q.shape
    C = GLA_CHUNK
    idx = np.arange(C)
    tri = jnp.asarray(np.stack([idx[:, None] >= idx[None, :], idx[:, None] <= idx[None, :]]).astype(np.float32),
                      dtype=BF16)
    w2 = jnp.stack([_block_diag2(gw2[0][:, h * DK_C:(h + 1) * DK_C], gw2[1][:, h * DK_C:(h + 1) * DK_C])
                    for h in range(H_C)]).astype(BF16)
    gb = jnp.stack([jnp.concatenate([gbias[0][h * DK_C:(h + 1) * DK_C], gbias[1][h * DK_C:(h + 1) * DK_C]])
                    for h in range(H_C)]).reshape(H_C, 1, 2 * DK_C)
    if s_init is None:
        zeros = jnp.zeros((1, 1, 1, DV_C, DK_C), F32)
        s0f = s0b = zeros
        s_spec = pl.BlockSpec((1, 1, 1, DV_C, DK_C), lambda b, h: (0, 0, 0, 0, 0))
    else:
        s0f, s0b = (jnp.swapaxes(s, -1, -2) for s in s_init)
        s_spec = pl.BlockSpec((1, 1, 1, DV_C, DK_C), lambda b, h: (b, layer_j, h, 0, 0))
    o, sf, sb = pl.pallas_call(
        functools.partial(_gla_kernel, T),
        grid=(B, H_C),
        in_specs=[
            pl.BlockSpec((1, T, DK_C), lambda b, h: (b, 0, h)),
            pl.BlockSpec((1, T, DK_C), lambda b, h: (b, 0, h)),
            pl.BlockSpec((1, T, DV_C), lambda b, h: (b, 0, h)),
            pl.BlockSpec((1, T, DV_C), lambda b, h: (b, 0, h)),
            pl.BlockSpec((1, T, 2 * G_LORA), lambda b, h: (b, 0, 0)),
            pl.BlockSpec((1, 2 * G_LORA, 2 * DK_C), lambda b, h: (h, 0, 0)),
            pl.BlockSpec((1, 1, 2 * DK_C), lambda b, h: (h, 0, 0)),
            pl.BlockSpec((1, DV_C), lambda b, h: (0, 0)),
            pl.BlockSpec((2, C, C), lambda b, h: (0, 0, 0)),
            s_spec, s_spec,
        ],
        out_specs=[
            pl.BlockSpec((1, T, DV_C), lambda b, h: (b, 0, h)),
            pl.BlockSpec((1, 1, DV_C, DK_C), lambda b, h: (b, h, 0, 0)),
            pl.BlockSpec((1, 1, DV_C, DK_C), lambda b, h: (b, h, 0, 0)),
        ],
        out_shape=[
            jax.ShapeDtypeStruct((B, T, H_C * DV_C), F32),
            jax.ShapeDtypeStruct((B, H_C, DV_C, DK_C), F32),
            jax.ShapeDtypeStruct((B, H_C, DV_C, DK_C), F32),
        ],
        scratch_shapes=[
            pltpu.VMEM((T, 2 * DK_C), F32), pltpu.VMEM((T, DV_C), F32),
            pltpu.VMEM((2, T, DK_C), BF16), pltpu.VMEM((2, T, DK_C), BF16),
            pltpu.VMEM((2, T // C * SUBLANES, DK_C), F32), pltpu.VMEM((2, DV_C, DK_C), F32),
        ],
        compiler_params=_params("parallel", "parallel"),
    )(q, k, v, g, gl, w2, gb, ln_g.reshape(1, -1), tri, s0f, s0b)
    return o, (jnp.swapaxes(sf, -1, -2), jnp.swapaxes(sb, -1, -2))


def kernel(x_prompt, x_sample, c, cache_attn_k, cache_attn_v, state_rwkv_fwd, state_rwkv_bwd, state_gla_fwd, state_gla_bwd, c_ctx, norm_g, mod_w, mod_b, ev_w_in, ev_w_out, ev_qn_g, ev_kn_g, ev_shift_mu, rw_w0, rw_w2, rw_a0, rw_a2, rw_kk, rw_ka, rw_rk, rw_ln_g, rw_ln_b, od_w_in, od_w_out, gla_w2, gla_b, gla_ln_g, final_g):
    n_dec = c.shape[0]
    cond = jnp.concatenate([c_ctx[None], c, jnp.zeros((SUBLANES - 1 - n_dec, D_MODEL), F32)], axis=0)
    mod = _modulation(cond, mod_w, mod_b)

    ev_in = ev_w_in.astype(BF16)
    ev_out = ev_w_out.astype(BF16)
    od_in = od_w_in.astype(BF16)
    od_out = od_w_out.astype(BF16)
    kw = KV_A * HD_A
    ck = cache_attn_k.reshape(cache_attn_k.shape[:3] + (kw,))
    cv = cache_attn_v.reshape(cache_attn_v.shape[:3] + (kw,))
    gla_f_t = state_gla_fwd
    gla_b_t = state_gla_bwd

    def trunk(x, latent):
        new = {"k": [], "v": [], "rf": [], "rb": [], "gf": [], "gb": []}
        for i in range(DEPTH):
            j = i // 2
            fg = final_g if i == DEPTH - 1 else None
            if i % 2 == 0:
                qa, ka, va, ga, zb, gb = _inproj(x, norm_g[i], mod, i, ev_in[j], EV_SPLITS, latent)
                if latent:
                    o_a, _ = _attention(qa, ka, va, ga, ev_qn_g[j], ev_kn_g[j], ck, cv, j)
                    s_init = (state_rwkv_fwd[:, j], state_rwkv_bwd[:, j])
                else:
                    o_a, kn = _attention(qa, ka, va, ga, ev_qn_g[j], ev_kn_g[j])
                    s_init = None
                    new["k"].append(kn)
                    new["v"].append(va)
                p = dict(shift_mu=ev_shift_mu[j], w0=rw_w0[j], w2=rw_w2[j], a0=rw_a0[j], a2=rw_a2[j],
                         k_k=rw_kk[j], k_a=rw_ka[j], r_k=rw_rk[j], ln_g=rw_ln_g[j], ln_b=rw_ln_b[j])
                o_b, fin = _rwkv_mixer(zb, gb, p, s_init)
                new["rf"].append(fin[0])
                new["rb"].append(fin[1])
                x = _outproj([o_a, o_b], [ev_out[j, :D_HALF], ev_out[j, D_HALF:]], x, mod, i, latent, fg)
            else:
                q, k, v, g, gl = _inproj(x, norm_g[i], mod, i, od_in[j], OD_SPLITS, latent)
                s_init = (gla_f_t, gla_b_t) if latent else None
                o, fin = _gla_mixer(q, k, v, g, gl, gla_w2[j], gla_b[j], gla_ln_g[j], s_init, j)
                new["gf"].append(fin[0])
                new["gb"].append(fin[1])
                x = _outproj([o], [od_out[j]], x, mod, i, latent, fg)
        return x, new

    y_prompt, new = trunk(x_prompt, False)
    y_sample, _ = trunk(x_sample, True)
    B, T = x_prompt.shape[:2]
    heads = lambda t: t.reshape(B, T, KV_A, HD_A)
    return (y_prompt, y_sample,
            jnp.stack([heads(t) for t in new["k"]], axis=1), jnp.stack([heads(t) for t in new["v"]], axis=1),
            jnp.stack(new["rf"], axis=1), jnp.stack(new["rb"], axis=1),
            jnp.stack(new["gf"], axis=1), jnp.stack(new["gb"], axis=1))
```

```python
import functools

import numpy as np
import jax
import jax.numpy as jnp
from jax import lax
from jax.experimental import pallas as pl
from jax.experimental.pallas import tpu as pltpu

F32 = jnp.float32
BF16 = jnp.bfloat16

D_MODEL = 1024
DEPTH = 4
GRID_W = 64
D_HALF = D_MODEL // 2
HD_A = 64
H_A = D_HALF // HD_A
KV_A = H_A // 4
ROPE_BASE = 10000.0
HS_B = 64
H_B = D_HALF // HS_B
W_LORA = 64
A_LORA = 64
RWKV_DECAY_SCALE = 0.606531
GN_EPS = 64e-5
B_SHIFT = 3 * D_HALF + 2 * W_LORA + 2 * A_LORA
H_C = 4
DK_C = D_MODEL // 2 // H_C
DV_C = D_MODEL // H_C
G_LORA = 16
GLA_TAU = 16.0
EPS = 1e-6

EV_SPLITS = (H_A * HD_A, KV_A * HD_A, KV_A * HD_A, D_HALF, B_SHIFT, D_HALF)
OD_SPLITS = (H_C * DK_C, H_C * DK_C, D_MODEL, D_MODEL, 2 * G_LORA)

LANES = 128
SUBLANES = 8
VMEM_LIMIT_BYTES = 56 * 1024 * 1024

ROW_TILE = 256
Q_TILE = 256
GLA_CHUNK = 64
SCAN_T = 8


def _params(*sem):
    return pltpu.CompilerParams(dimension_semantics=sem, vmem_limit_bytes=VMEM_LIMIT_BYTES)


def _silu(x):
    return x * jax.nn.sigmoid(x)


def _dot(a, b):
    return jnp.dot(a, b, preferred_element_type=F32)


def _seg_sum(x, ones_bd):
    hi = x.astype(BF16)
    r1 = x - hi.astype(F32)
    mid = r1.astype(BF16)
    lo = (r1 - mid.astype(F32)).astype(BF16)
    return _dot(hi, ones_bd) + _dot(mid, ones_bd) + _dot(lo, ones_bd)


def _get(ref):
    return ref[...].reshape(ref.shape[-2:])


def _put(ref, val):
    ref[...] = val.reshape(ref.shape)


def _block_diag_ones(n, blk):
    i = np.arange(n) // blk
    return jnp.asarray((i[:, None] == i[None, :]).astype(np.float32), dtype=BF16)


def _mod_kernel(cond_ref, w_ref, b_ref, o_ref):
    s = _silu(cond_ref[...])
    o_ref[0] = _dot(s.astype(BF16), w_ref[0].astype(BF16)) + b_ref[0]


def _modulation(cond, mod_w, mod_b):
    n = cond.shape[0]
    return pl.pallas_call(
        _mod_kernel,
        grid=(DEPTH, 3),
        in_specs=[
            pl.BlockSpec((n, D_MODEL), lambda i, j: (0, 0)),
            pl.BlockSpec((1, D_MODEL, D_MODEL), lambda i, j: (i, 0, j)),
            pl.BlockSpec((1, 1, D_MODEL), lambda i, j: (i, 0, j)),
        ],
        out_specs=pl.BlockSpec((1, n, D_MODEL), lambda i, j: (i, 0, j)),
        out_shape=jax.ShapeDtypeStruct((DEPTH, n, 3 * D_MODEL), F32),
        compiler_params=_params("parallel", "parallel"),
    )(cond, mod_w, mod_b.reshape(DEPTH, 1, 3 * D_MODEL))


def _mod_row(latent):
    return (1 + pl.program_id(0)) if latent else 0


def _inproj_kernel(latent, splits, x_ref, g_ref, sh_ref, sc_ref, w_ref, *out_refs):
    r = _mod_row(latent)
    x = x_ref[0]
    y = x * lax.rsqrt(jnp.mean(x * x, axis=-1, keepdims=True) + EPS) * g_ref[...]
    shift = sh_ref[0, pl.ds(r, 1), :]
    scale = sc_ref[0, pl.ds(r, 1), :]
    h = (y * (1.0 + scale) + shift).astype(BF16)
    off = 0
    for o_ref, n in zip(out_refs, splits):
        o_ref[0] = _dot(h, w_ref[:, off:off + n])
        off += n


def _inproj(x, norm_g, mod, layer, w_bf16, splits, latent):
    B, T, _ = x.shape
    cols = w_bf16.shape[1]
    nrow = mod.shape[1]
    return pl.pallas_call(
        functools.partial(_inproj_kernel, latent, splits),
        grid=(B, T // ROW_TILE),
        in_specs=[
            pl.BlockSpec((1, ROW_TILE, D_MODEL), lambda b, i: (b, i, 0)),
            pl.BlockSpec((1, D_MODEL), lambda b, i: (0, 0)),
            pl.BlockSpec((1, nrow, D_MODEL), lambda b, i: (layer, 0, 0)),
            pl.BlockSpec((1, nrow, D_MODEL), lambda b, i: (layer, 0, 1)),
            pl.BlockSpec((D_MODEL, cols), lambda b, i: (0, 0)),
        ],
        out_specs=[pl.BlockSpec((1, ROW_TILE, n), lambda b, i: (b, i, 0)) for n in splits],
        out_shape=[jax.ShapeDtypeStruct((B, T, n), F32) for n in splits],
        compiler_params=_params("parallel", "parallel"),
    )(x, norm_g.reshape(1, D_MODEL), mod, mod, w_bf16)


def _outproj_kernel(latent, final, n_in, *refs):
    o_refs = refs[:n_in]
    w_refs = refs[n_in:2 * n_in]
    x_ref, gate_ref = refs[2 * n_in], refs[2 * n_in + 1]
    rest = refs[2 * n_in + 2:]
    r = _mod_row(latent)
    acc = _dot(o_refs[0][0].astype(BF16), w_refs[0][...])
    for o_ref, w_ref in zip(o_refs[1:], w_refs[1:]):
        acc = acc + _dot(o_ref[0].astype(BF16), w_ref[...])
    y = x_ref[0] + gate_ref[0, pl.ds(r, 1), :] * acc
    if final:
        fg_ref, out_ref = rest
        y = y * lax.rsqrt(jnp.mean(y * y, axis=-1, keepdims=True) + EPS) * fg_ref[...]
    else:
        (out_ref,) = rest
    out_ref[0] = y


def _outproj(outs, ws_bf16, x, mod, layer, latent, final_g=None):
    B, T, _ = x.shape
    nrow = mod.shape[1]
    n_in = len(outs)
    final = final_g is not None
    in_specs = [pl.BlockSpec((1, ROW_TILE, o.shape[-1]), lambda b, i: (b, i, 0)) for o in outs]
    in_specs += [pl.BlockSpec(w.shape, lambda b, i: (0, 0)) for w in ws_bf16]
    in_specs += [
        pl.BlockSpec((1, ROW_TILE, D_MODEL), lambda b, i: (b, i, 0)),
        pl.BlockSpec((1, nrow, D_MODEL), lambda b, i: (layer, 0, 2)),
    ]
    args = list(outs) + list(ws_bf16) + [x, mod]
    if final:
        in_specs.append(pl.BlockSpec((1, D_MODEL), lambda b, i: (0, 0)))
        args.append(final_g.reshape(1, D_MODEL))
    return pl.pallas_call(
        functools.partial(_outproj_kernel, latent, final, n_in),
        grid=(B, T // ROW_TILE),
        in_specs=in_specs,
        out_specs=pl.BlockSpec((1, ROW_TILE, D_MODEL), lambda b, i: (b, i, 0)),
        out_shape=jax.ShapeDtypeStruct((B, T, D_MODEL), F32),
        compiler_params=_params("parallel", "parallel"),
    )(*args)


def _rope_tables(T):
    n_rows = T // GRID_W
    row = jnp.repeat(jnp.arange(n_rows), GRID_W).astype(F32)
    col = jnp.tile(jnp.arange(GRID_W), n_rows).astype(F32)
    n_freq = HD_A // 4
    inv = ROPE_BASE ** (-jnp.arange(n_freq, dtype=F32) / n_freq)
    ang_r = row[:, None] * inv
    ang_c = col[:, None] * inv
    zero = jnp.zeros_like(ang_r)
    cos = jnp.concatenate([jnp.cos(ang_r), jnp.cos(ang_r), jnp.cos(ang_c), jnp.cos(ang_c)], axis=1)
    s1 = jnp.concatenate([-jnp.sin(ang_r), zero, -jnp.sin(ang_c), zero], axis=1)
    s2 = jnp.concatenate([zero, jnp.sin(ang_r), zero, jnp.sin(ang_c)], axis=1)
    return cos, s1, s2


def _rope(x, cos, s1, s2):
    n = x.shape[-1]
    q = HD_A // 4
    return x * cos + pltpu.roll(x, n - q, 1) * s1 + pltpu.roll(x, q, 1) * s2


def _attn_kernel(latent, S, *refs):
    if latent:
        (q_ref, k_ref, v_ref, ga_ref, qg_ref, kg_ref, bd_ref, cos_ref, s1_ref, s2_ref,
         cosk_ref, s1k_ref, s2k_ref, ck_ref, cv_ref, o_ref, km_ref, vm_ref) = refs
    else:
        (q_ref, k_ref, v_ref, ga_ref, qg_ref, kg_ref, bd_ref, o_ref, kn_ref, km_ref, vm_ref) = refs
    bd = bd_ref[...]
    inv_d = 1.0 / HD_A
    kw = KV_A * HD_A

    @pl.when(pl.program_id(1) == 0)
    def _():
        k = k_ref[0]
        kn = k * lax.rsqrt(_seg_sum(k * k, bd[:kw, :kw]) * inv_d + EPS) * kg_ref[...]
        v = v_ref[0]
        if latent:
            kn = _rope(kn, cosk_ref[...], s1k_ref[...], s2k_ref[...])
            k_all = jnp.concatenate([ck_ref[0, 0], kn], axis=0)
            v_all = jnp.concatenate([cv_ref[0, 0], v], axis=0)
        else:
            kn_ref[0] = kn
            k_all, v_all = kn, v
        lane = lax.broadcasted_iota(jnp.int32, (S, kw), 1)
        k_sw = pltpu.roll(k_all, HD_A, 1)
        v_sw = pltpu.roll(v_all, HD_A, 1)
        for j in range(KV_A):
            for half in range(2):
                keep = (lane < HD_A) if half == 0 else (lane >= HD_A)
                src_k, src_v = (k_all, v_all) if j == half else (k_sw, v_sw)
                km_ref[2 * j + half] = jnp.where(keep, src_k, 0.0).astype(BF16)
                vm_ref[2 * j + half] = jnp.where(keep, src_v, 0.0).astype(BF16)

    q = q_ref[0]
    qn = q * lax.rsqrt(_seg_sum(q * q, bd) * inv_d + EPS) * qg_ref[...]
    if latent:
        qn = _rope(qn, cos_ref[...], s1_ref[...], s2_ref[...])
    qb = (qn * HD_A ** -0.5).astype(BF16)
    for m in range(H_A // 2):
        blk = slice(m * LANES, (m + 1) * LANES)
        qs = qb[:, blk]
        acc = None
        for half in range(2):
            j = (2 * m + half) // (H_A // KV_A)
            s = lax.dot_general(qs, km_ref[2 * j + half], (((1,), (1,)), ((), ())),
                                preferred_element_type=F32)
            e = jnp.exp(s - jnp.max(s, axis=-1, keepdims=True))
            l = jnp.sum(e, axis=-1, keepdims=True)
            pv = _dot(e.astype(BF16), vm_ref[2 * j + half]) * (1.0 / l)
            acc = pv if acc is None else acc + pv
        o_ref[0, :, blk] = acc * _silu(ga_ref[0, :, blk])


def _attention(qa, ka, va, ga, qn_g, kn_g, ctx_k=None, ctx_v=None, layer_j=0):
    B, T, _ = qa.shape
    latent = ctx_k is not None
    S = T + (ctx_k.shape[2] if latent else 0)
    kw = KV_A * HD_A
    qw = H_A * HD_A
    bd = _block_diag_ones(qw, HD_A)
    qblk = pl.BlockSpec((1, Q_TILE, qw), lambda b, i: (b, i, 0))
    kblk = pl.BlockSpec((1, T, kw), lambda b, i: (b, 0, 0))
    in_specs = [qblk, kblk, kblk, qblk,
                pl.BlockSpec((1, qw), lambda b, i: (0, 0)),
                pl.BlockSpec((1, kw), lambda b, i: (0, 0)),
                pl.BlockSpec(bd.shape, lambda b, i: (0, 0))]
    args = [qa, ka, va, ga, jnp.tile(qn_g, H_A).reshape(1, -1), jnp.tile(kn_g, KV_A).reshape(1, -1), bd]
    out_specs = [qblk]
    out_shape = [jax.ShapeDtypeStruct((B, T, qw), F32)]
    if latent:
        tabs = _rope_tables(T)
        P = ctx_k.shape[2]
        in_specs += [pl.BlockSpec((Q_TILE, qw), lambda b, i: (i, 0))] * 3
        in_specs += [pl.BlockSpec((T, kw), lambda b, i: (0, 0))] * 3
        in_specs += [pl.BlockSpec((1, 1, P, kw), lambda b, i: (b, layer_j, 0, 0))] * 2
        args += [jnp.tile(t, (1, H_A)) for t in tabs] + [jnp.tile(t, (1, KV_A)) for t in tabs] + [ctx_k, ctx_v]
    else:
        out_specs.append(kblk)
        out_shape.append(jax.ShapeDtypeStruct((B, T, kw), F32))
    res = pl.pallas_call(
        functools.partial(_attn_kernel, latent, S),
        grid=(B, T // Q_TILE),
        in_specs=in_specs,
        out_specs=out_specs,
        out_shape=out_shape,
        scratch_shapes=[pltpu.VMEM((2 * KV_A, S, kw), BF16), pltpu.VMEM((2 * KV_A, S, kw), BF16)],
        compiler_params=_params("parallel", "arbitrary"),
    )(*args)
    return res if not latent else (res[0], None)


def _rwkv_prep_kernel(n_t, z_ref, zp_ref, zn_ref, mu_ref, w0_ref, a0_ref, w2_ref, a2_ref,
                      kkg_ref, ka_ref, rk_ref, bd_ref,
                      wf_ref, wb_ref, kdf_ref, kdb_ref, kaf_ref, kab_ref, nkk_ref, r_ref, v_ref, bonus_ref):
    i = pl.program_id(1)
    z = z_ref[0]
    n = z.shape[0]
    prev_row = jnp.where(i > 0, zp_ref[0, SUBLANES - 1:SUBLANES, :], 0.0)
    next_row = jnp.where(i < n_t - 1, zn_ref[0, 0:1, :], 0.0)
    rows = lax.broadcasted_iota(jnp.int32, (n, 1), 0)
    zp = jnp.where(rows == 0, prev_row, pltpu.roll(z, 1, 0))
    zn = jnp.where(rows == n - 1, next_row, pltpu.roll(z, n - 1, 0))
    zs = z + mu_ref[...] * (0.5 * (zp + zn) - z)

    rb = zs[:, 0:D_HALF]
    kb = zs[:, D_HALF:2 * D_HALF]
    vb = zs[:, 2 * D_HALF:3 * D_HALF]
    lw = zs[:, 3 * D_HALF:3 * D_HALF + 2 * W_LORA]
    la = zs[:, 3 * D_HALF + 2 * W_LORA:]
    w = jnp.exp(-RWKV_DECAY_SCALE * jax.nn.sigmoid(w0_ref[...] + _dot(jnp.tanh(lw).astype(BF16), w2_ref[...])))
    a = jax.nn.sigmoid(a0_ref[...] + _dot(la.astype(BF16), a2_ref[...]))
    bd = bd_ref[...]
    kk = kb * kkg_ref[...]
    kk = kk * lax.rsqrt(_seg_sum(kk * kk, bd) + 1e-12)
    ka = ka_ref[...]
    a_f, a_b = a[:, :D_HALF], a[:, D_HALF:]
    kd_f = kb * (1.0 + (a_f - 1.0) * ka)
    kd_b = kb * (1.0 + (a_b - 1.0) * ka)
    _put(wf_ref, w[:, :D_HALF])
    _put(wb_ref, w[:, D_HALF:])
    _put(kdf_ref, kd_f)
    _put(kdb_ref, kd_b)
    _put(kaf_ref, kk * a_f)
    _put(kab_ref, kk * a_b)
    _put(nkk_ref, -kk)
    _put(r_ref, rb)
    _put(v_ref, vb)
    _put(bonus_ref, _seg_sum(rb * rk_ref[...] * (kd_f + kd_b), bd) * vb)


def _block_diag2(m0, m1):
    z = jnp.zeros_like(m0)
    return jnp.concatenate([jnp.concatenate([m0, z], axis=1), jnp.concatenate([z, m1], axis=1)], axis=0)


def _rwkv_prep(zb, shift_mu, w0, w2, a0, a2, k_k, k_a, r_k):
    B, T, _ = zb.shape
    n_t = T // ROW_TILE
    per_tile = ROW_TILE // SUBLANES
    bd = _block_diag_ones(D_HALF, HS_B)
    row = lambda x: x.reshape(1, -1)
    vec = pl.BlockSpec((1, D_HALF), lambda b, i: (0, 0))
    vec2 = pl.BlockSpec((1, 2 * D_HALF), lambda b, i: (0, 0))
    out = pl.BlockSpec((1, ROW_TILE, D_HALF), lambda b, i: (b, i, 0))
    return pl.pallas_call(
        functools.partial(_rwkv_prep_kernel, n_t),
        grid=(B, n_t),
        in_specs=[
            pl.BlockSpec((1, ROW_TILE, B_SHIFT), lambda b, i: (b, i, 0)),
            pl.BlockSpec((1, SUBLANES, B_SHIFT), lambda b, i: (b, jnp.maximum(i * per_tile - 1, 0), 0)),
            pl.BlockSpec((1, SUBLANES, B_SHIFT),
                         lambda b, i: (b, jnp.minimum((i + 1) * per_tile, n_t * per_tile - 1), 0)),
            pl.BlockSpec((1, B_SHIFT), lambda b, i: (0, 0)),
            vec2, vec2,
            pl.BlockSpec((2 * W_LORA, 2 * D_HALF), lambda b, i: (0, 0)),
            pl.BlockSpec((2 * A_LORA, 2 * D_HALF), lambda b, i: (0, 0)),
            vec, vec, vec,
            pl.BlockSpec(bd.shape, lambda b, i: (0, 0)),
        ],
        out_specs=[out] * 10,
        out_shape=[jax.ShapeDtypeStruct((B, T, D_HALF), F32)] * 10,
        compiler_params=_params("parallel", "parallel"),
    )(zb, zb, zb, row(shift_mu), row(w0), row(a0),
      _block_diag2(w2[0], w2[1]).astype(BF16), _block_diag2(a2[0], a2[1]).astype(BF16),
      row(k_k), row(k_a), row(r_k), bd)


def _scan_steps(vh, n_steps, reverse, kvecs, vblock, yput, s_ref):
    def t_step(i, carry):
        t = (n_steps - 1 - i) if reverse else i
        w, nkk, kka, kd, r = kvecs(t)

        def v_step(g, c):
            base = pl.multiple_of(g * SUBLANES, SUBLANES)
            vblk = vblock(t, base)
            ys = []
            for u in range(SUBLANES):
                s = s_ref[base + u]
                sa = jnp.sum(s * nkk, axis=0, keepdims=True)
                s = s * w + sa * kka + vblk[u:u + 1, :] * kd
                s_ref[base + u] = s
                ys.append(jnp.sum(s * r, axis=0, keepdims=True))
            yput(t, base, jnp.concatenate(ys, axis=0))
            return c

        lax.fori_loop(0, vh // SUBLANES, v_step, 0)
        return carry

    lax.fori_loop(0, n_steps, t_step, 0)


def _rwkv_scan_kernel(vh, reverse, w_ref, nkk_ref, kka_ref, kd_ref, r_ref, v_ref, s0_ref, y_ref, s_ref):
    @pl.when(pl.program_id(1) == 0)
    def _():
        s_ref[...] = s0_ref[...]

    def yput(t, base, val):
        y_ref[t, pl.ds(base, SUBLANES), :] = val

    _scan_steps(vh, SCAN_T, reverse,
                lambda t: (w_ref[t], nkk_ref[t], kka_ref[t], kd_ref[t], r_ref[t]),
                lambda t, base: v_ref[t, pl.ds(base, SUBLANES), :], yput, s_ref)


def _rwkv_scan(w, nkk, kka, kd, r, v, s0, reverse):
    T, _, C = w.shape
    vh = v.shape[1]
    n_t = T // SCAN_T
    tblk = (lambda t: n_t - 1 - t) if reverse else (lambda t: t)
    kvec = pl.BlockSpec((SCAN_T, HS_B, LANES), lambda g, t: (tblk(t), 0, g))
    vvec = pl.BlockSpec((SCAN_T, vh, LANES), lambda g, t: (tblk(t), 0, g))
    state = pl.BlockSpec((vh, HS_B, LANES), lambda g, t: (0, 0, g))
    return pl.pallas_call(
        functools.partial(_rwkv_scan_kernel, vh, reverse),
        grid=(C // LANES, n_t),
        in_specs=[kvec] * 5 + [vvec, state],
        out_specs=[vvec, state],
        out_shape=[jax.ShapeDtypeStruct((T, vh, C), F32), jax.ShapeDtypeStruct((vh, HS_B, C), F32)],
        compiler_params=_params("parallel", "arbitrary"),
    )(w, nkk, kka, kd, r, v, s0)


HEADS_PER_GROUP = LANES // 32


def _chain_pair(x1, x2):
    lo = lax.broadcasted_iota(jnp.int32, (x1.shape[0], LANES), 1) < HS_B
    pieces = []
    for h in range(HEADS_PER_GROUP):
        blk = slice((h // 2) * LANES, (h // 2 + 1) * LANES)
        a, b = x1[:, blk], x2[:, blk]
        if h % 2 == 0:
            pieces.append(jnp.where(lo, a, pltpu.roll(b, HS_B, 1)))
        else:
            pieces.append(jnp.where(lo, pltpu.roll(a, HS_B, 1), b))
    return jnp.concatenate(pieces, axis=0).T


def _unchain_pair(y0, y1):
    yt = jnp.concatenate([y0, y1], axis=0).T
    nb = yt.shape[0] // HEADS_PER_GROUP
    lo = lax.broadcasted_iota(jnp.int32, (nb, LANES), 1) < HS_B
    rows = [yt[h * nb:(h + 1) * nb] for h in range(HEADS_PER_GROUP)]
    out0, out1 = [], []
    for m in range(HEADS_PER_GROUP // 2):
        even, odd = rows[2 * m], rows[2 * m + 1]
        out0.append(jnp.where(lo, even, pltpu.roll(odd, HS_B, 1)))
        out1.append(jnp.where(lo, pltpu.roll(even, HS_B, 1), odd))
    return jnp.concatenate(out0, axis=1), jnp.concatenate(out1, axis=1)


def _rwkv_scan_tm_kernel(reverse, w_ref, nkk_ref, kka_ref, kd_ref, r_ref, v_ref, s0_ref, y_ref, s_ref,
                         kbuf, ybuf):
    @pl.when(pl.program_id(1) == 0)
    def _():
        s_ref[...] = s0_ref[...]

    pairs = ((w_ref, nkk_ref), (kka_ref, kd_ref), (r_ref, v_ref))
    for t in range(SCAN_T):
        for p, (a_ref, b_ref) in enumerate(pairs):
            both = _chain_pair(a_ref[:, t, :], b_ref[:, t, :])
            kbuf[2 * p, t] = both[:HS_B]
            kbuf[2 * p + 1, t] = both[HS_B:]

    def yput(t, base, val):
        ybuf[t, pl.ds(base, SUBLANES), :] = val

    _scan_steps(HS_B, SCAN_T, reverse,
                lambda t: (kbuf[0, t], kbuf[1, t], kbuf[2, t], kbuf[3, t], kbuf[4, t]),
                lambda t, base: kbuf[5, t, pl.ds(base, SUBLANES), :], yput, s_ref)

    for t in range(0, SCAN_T, 2):
        y_ref[:, t, :], y_ref[:, t + 1, :] = _unchain_pair(ybuf[t], ybuf[t + 1])


def _rwkv_scan_tm(w, nkk, kka, kd, r, v, s0, reverse):
    B, T, W = w.shape
    gw = HEADS_PER_GROUP * HS_B
    n_t = T // SCAN_T
    tblk = (lambda t: n_t - 1 - t) if reverse else (lambda t: t)
    vec = pl.BlockSpec((B, SCAN_T, gw), lambda g, t: (0, tblk(t), g))
    state = pl.BlockSpec((HS_B, HS_B, LANES), lambda g, t: (0, 0, g))
    return pl.pallas_call(
        functools.partial(_rwkv_scan_tm_kernel, reverse),
        grid=(W // gw, n_t),
        in_specs=[vec] * 6 + [state],
        out_specs=[vec, state],
        out_shape=[jax.ShapeDtypeStruct((B, T, W), F32), jax.ShapeDtypeStruct((HS_B, HS_B, W // gw * LANES), F32)],
        scratch_shapes=[pltpu.VMEM((6, SCAN_T, HS_B, LANES), F32), pltpu.VMEM((SCAN_T, HS_B, LANES), F32)],
        compiler_params=_params("parallel", "arbitrary"),
    )(w, nkk, kka, kd, r, v, s0)


def _rwkv_post_kernel(yf_ref, yb_ref, bonus_ref, gb_ref, lng_ref, lnb_ref, bd_ref, o_ref):
    bd = bd_ref[...]
    y = _get(yf_ref) + _get(yb_ref)
    inv_n = 1.0 / HS_B
    d = y - _seg_sum(y, bd) * inv_n
    var = _seg_sum(d * d, bd) * inv_n
    yn = d * lax.rsqrt(var + GN_EPS) * lng_ref[...] + lnb_ref[...]
    o_ref[0] = (yn + bonus_ref[0]) * _silu(gb_ref[0])


def _rwkv_post(y_f, y_b, bonus, gb, ln_g, ln_b):
    B, T, _ = y_f.shape
    bd = _block_diag_ones(D_HALF, HS_B)
    blk = pl.BlockSpec((1, ROW_TILE, D_HALF), lambda b, i: (b, i, 0))
    vec = pl.BlockSpec((1, D_HALF), lambda b, i: (0, 0))
    return pl.pallas_call(
        _rwkv_post_kernel,
        grid=(B, T // ROW_TILE),
        in_specs=[blk, blk, blk, blk, vec, vec, pl.BlockSpec(bd.shape, lambda b, i: (0, 0))],
        out_specs=blk,
        out_shape=jax.ShapeDtypeStruct((B, T, D_HALF), F32),
        compiler_params=_params("parallel", "parallel"),
    )(y_f, y_b, bonus, gb, ln_g.reshape(1, -1), ln_b.reshape(1, -1), bd)


def _rwkv_mixer(zb, gb, p, s_init):
    B, T, _ = zb.shape
    nbh = B * H_B
    (w_f, w_b, kd_f, kd_b, ka_f, ka_b, nkk, r, v, bonus) = _rwkv_prep(
        zb, p["shift_mu"], p["w0"], p["w2"], p["a0"], p["a2"], p["k_k"], p["k_a"], p["r_k"])
    ys, fin = [], []
    if B * HEADS_PER_GROUP == LANES:
        for d, (w_d, ka_d, kd_d) in enumerate(((w_f, ka_f, kd_f), (w_b, ka_b, kd_b))):
            if s_init is None:
                s0 = jnp.zeros((HS_B, HS_B, nbh), F32)
            else:
                s0 = s_init[d].transpose(2, 3, 1, 0).reshape(HS_B, HS_B, nbh)
            y, s_fin = _rwkv_scan_tm(w_d, nkk, ka_d, kd_d, r, v, s0, reverse=(d == 1))
            ys.append(y)
            fin.append(s_fin.reshape(HS_B, HS_B, H_B, B).transpose(3, 2, 0, 1))
        return _rwkv_post(ys[0], ys[1], bonus, gb, p["ln_g"], p["ln_b"]), fin

    vs = max(LANES // nbh, 1)
    vh = HS_B // vs

    def kchains(x):
        x = x.reshape(B, T, H_B, HS_B).transpose(1, 3, 0, 2).reshape(T, HS_B, 1, nbh)
        return jnp.broadcast_to(x, (T, HS_B, vs, nbh)).reshape(T, HS_B, vs * nbh)

    def vchains(x):
        x = x.reshape(B, T, H_B, vs, vh).transpose(1, 4, 3, 0, 2)
        return x.reshape(T, vh, vs * nbh)

    def unchain(y):
        return y.reshape(T, vh, vs, B, H_B).transpose(3, 0, 4, 2, 1).reshape(B, T, H_B * HS_B)

    nkk_c, r_c, v_c = kchains(nkk), kchains(r), vchains(v)
    for d, (w_d, ka_d, kd_d) in enumerate(((w_f, ka_f, kd_f), (w_b, ka_b, kd_b))):
        if s_init is None:
            s0 = jnp.zeros((vh, HS_B, vs * nbh), F32)
        else:
            s0 = s_init[d].reshape(B, H_B, vs, vh, HS_B).transpose(3, 4, 2, 0, 1).reshape(vh, HS_B, vs * nbh)
        y, s_fin = _rwkv_scan(kchains(w_d), nkk_c, kchains(ka_d), kchains(kd_d), r_c, v_c, s0, reverse=(d == 1))
        ys.append(unchain(y))
        fin.append(s_fin.reshape(vh, HS_B, vs, B, H_B).transpose(3, 4, 2, 0, 1).reshape(B, H_B, HS_B, HS_B))
    o_b = _rwkv_post(ys[0], ys[1], bonus, gb, p["ln_g"], p["ln_b"])
    return o_b, fin


def _chunk_cumsum(x, chunk, suffix):
    T = x.shape[0]
    pos = lax.broadcasted_iota(jnp.int32, (T, 1), 0) % chunk
    step = 1
    while step < chunk:
        if suffix:
            x = x + jnp.where(pos < chunk - step, pltpu.roll(x, T - step, 0), 0.0)
        else:
            x = x + jnp.where(pos >= step, pltpu.roll(x, step, 0), 0.0)
        step *= 2
    return x


def _gla_kernel(T, q_ref, k_ref, v_ref, g_ref, gl_ref, w2_ref, gb_ref, lng_ref, tri_ref, s0f_ref, s0b_ref,
                o_ref, sf_ref, sb_ref, b_ref, acc_ref, qb_ref, dec_ref, u_ref, sst_ref, st_ref):
    C = GLA_CHUNK
    n_c = T // C
    pre = _dot(gl_ref[0].astype(BF16), w2_ref[0]) + gb_ref[0]
    la = jax.nn.log_sigmoid(pre) * (1.0 / GLA_TAU)
    b_ref[:, :DK_C] = _chunk_cumsum(la[:, :DK_C], C, suffix=False)
    b_ref[:, DK_C:] = _chunk_cumsum(la[:, DK_C:], C, suffix=True)
    qscale = DK_C ** -0.5
    contract_last = (((1,), (1,)), ((), ()))
    contract_first = (((0,), (0,)), ((), ()))
    vis_f = tri_ref[0]
    vis_b = tri_ref[1]

    def chunk_rows(c):
        return pl.ds(pl.multiple_of(c * C, C), C)

    def dec_rows(c):
        return pl.ds(pl.multiple_of(c * SUBLANES, SUBLANES), SUBLANES)

    def intra(c, carry):
        rows = chunk_rows(c)
        q = q_ref[0, rows, :] * qscale
        k = k_ref[0, rows, :]
        vc = v_ref[0, rows, :].astype(BF16)
        b = b_ref[rows, :]
        btot_f = b[C - 1:C, :DK_C]
        btot_b = b[0:1, DK_C:]
        qe, ke, qb, kl = [], [], [], []
        for bd, btot in ((b[:, :DK_C], btot_f), (b[:, DK_C:], btot_b)):
            mref = 0.5 * btot
            e_half = jnp.exp(mref)
            q_up = q * jnp.exp(bd - mref)
            k_dn = k * jnp.exp(mref - bd)
            qe.append(q_up.astype(BF16))
            ke.append(k_dn.astype(BF16))
            qb.append((q_up * e_half).astype(BF16))
            kl.append((k_dn * e_half).astype(BF16))
        sc = lax.dot_general(jnp.concatenate(qe, axis=0), jnp.concatenate(ke, axis=0), contract_last,
                             preferred_element_type=F32)
        att = sc[:C] * vis_f + pltpu.roll(sc[C:], C, 1) * vis_b
        acc_ref[rows, :] = _dot(att[:, :C].astype(BF16), vc)
        u_ref[c] = lax.dot_general(vc, jnp.concatenate(kl, axis=1), contract_first, preferred_element_type=F32)
        qb_ref[rows, :] = jnp.concatenate(qb, axis=1)
        dec_ref[dec_rows(c), :] = jnp.broadcast_to(
            jnp.exp(jnp.concatenate([btot_f, btot_b], axis=1)), (SUBLANES, 2 * DK_C))
        return carry

    lax.fori_loop(0, n_c, intra, 0, unroll=4)

    st_ref[0] = s0f_ref[0, 0, 0]
    st_ref[1] = s0b_ref[0, 0, 0]

    def states(i, carry):
        for d, c in ((0, i), (1, n_c - 1 - i)):
            lanes = slice(d * DK_C, (d + 1) * DK_C)
            st = st_ref[d]
            sst_ref[c, :, lanes] = st.astype(BF16)
            st_ref[d] = st * dec_ref[pl.ds(pl.multiple_of(c * SUBLANES, SUBLANES), 1), lanes] + u_ref[c, :, lanes]
        return carry

    lax.fori_loop(0, n_c, states, 0)
    sf_ref[0, 0] = st_ref[0]
    sb_ref[0, 0] = st_ref[1]

    def inter(c, carry):
        rows = chunk_rows(c)
        acc_ref[rows, :] = acc_ref[rows, :] + lax.dot_general(qb_ref[rows, :], sst_ref[c], contract_last,
                                                              preferred_element_type=F32)
        return carry

    lax.fori_loop(0, n_c, inter, 0, unroll=4)

    o = acc_ref[...]
    o = o * lax.rsqrt(jnp.mean(o * o, axis=-1, keepdims=True) + EPS) * lng_ref[...]
    o_ref[0] = o * _silu(g_ref[0])


def _gla_mixer(q, k, v, g, gl, gw2, gbias, ln_g, s_init, layer_j):
    B, T, _ = q.shape
    C = GLA_CHUNK
    idx = np.arange(C)
    tri = np.zeros((2, C, LANES), np.float32)
    tri[0, :, :C] = idx[:, None] >= idx[None, :]
    tri[1, :, :C] = idx[:, None] <= idx[None, :]
    tri = jnp.asarray(tri)
    w2 = jnp.stack([_block_diag2(gw2[0][:, h * DK_C:(h + 1) * DK_C], gw2[1][:, h * DK_C:(h + 1) * DK_C])
                    for h in range(H_C)]).astype(BF16)
    gb = jnp.stack([jnp.concatenate([gbias[0][h * DK_C:(h + 1) * DK_C], gbias[1][h * DK_C:(h + 1) * DK_C]])
                    for h in range(H_C)]).reshape(H_C, 1, 2 * DK_C)
    if s_init is None:
        zeros = jnp.zeros((1, 1, 1, DV_C, DK_C), F32)
        s0f = s0b = zeros
        s_spec = pl.BlockSpec((1, 1, 1, DV_C, DK_C), lambda b, h: (0, 0, 0, 0, 0))
    else:
        s0f, s0b = (jnp.swapaxes(s, -1, -2) for s in s_init)
        s_spec = pl.BlockSpec((1, 1, 1, DV_C, DK_C), lambda b, h: (b, layer_j, h, 0, 0))
    o, sf, sb = pl.pallas_call(
        functools.partial(_gla_kernel, T),
        grid=(B, H_C),
        in_specs=[
            pl.BlockSpec((1, T, DK_C), lambda b, h: (b, 0, h)),
            pl.BlockSpec((1, T, DK_C), lambda b, h: (b, 0, h)),
            pl.BlockSpec((1, T, DV_C), lambda b, h: (b, 0, h)),
            pl.BlockSpec((1, T, DV_C), lambda b, h: (b, 0, h)),
            pl.BlockSpec((1, T, 2 * G_LORA), lambda b, h: (b, 0, 0)),
            pl.BlockSpec((1, 2 * G_LORA, 2 * DK_C), lambda b, h: (h, 0, 0)),
            pl.BlockSpec((1, 1, 2 * DK_C), lambda b, h: (h, 0, 0)),
            pl.BlockSpec((1, DV_C), lambda b, h: (0, 0)),
            pl.BlockSpec((2, C, LANES), lambda b, h: (0, 0, 0)),
            s_spec, s_spec,
        ],
        out_specs=[
            pl.BlockSpec((1, T, DV_C), lambda b, h: (b, 0, h)),
            pl.BlockSpec((1, 1, DV_C, DK_C), lambda b, h: (b, h, 0, 0)),
            pl.BlockSpec((1, 1, DV_C, DK_C), lambda b, h: (b, h, 0, 0)),
        ],
        out_shape=[
            jax.ShapeDtypeStruct((B, T, H_C * DV_C), F32),
            jax.ShapeDtypeStruct((B, H_C, DV_C, DK_C), F32),
            jax.ShapeDtypeStruct((B, H_C, DV_C, DK_C), F32),
        ],
        scratch_shapes=[
            pltpu.VMEM((T, 2 * DK_C), F32), pltpu.VMEM((T, DV_C), F32),
            pltpu.VMEM((T, 2 * DK_C), BF16), pltpu.VMEM((T // C * SUBLANES, 2 * DK_C), F32),
            pltpu.VMEM((T // C, DV_C, 2 * DK_C), F32), pltpu.VMEM((T // C, DV_C, 2 * DK_C), BF16),
            pltpu.VMEM((2, DV_C, DK_C), F32),
        ],
        compiler_params=_params("parallel", "parallel"),
    )(q, k, v, g, gl, w2, gb, ln_g.reshape(1, -1), tri, s0f, s0b)
    return o, (jnp.swapaxes(sf, -1, -2), jnp.swapaxes(sb, -1, -2))


def kernel(x_prompt, x_sample, c, cache_attn_k, cache_attn_v, state_rwkv_fwd, state_rwkv_bwd, state_gla_fwd, state_gla_bwd, c_ctx, norm_g, mod_w, mod_b, ev_w_in, ev_w_out, ev_qn_g, ev_kn_g, ev_shift_mu, rw_w0, rw_w2, rw_a0, rw_a2, rw_kk, rw_ka, rw_rk, rw_ln_g, rw_ln_b, od_w_in, od_w_out, gla_w2, gla_b, gla_ln_g, final_g):
    n_dec = c.shape[0]
    cond = jnp.concatenate([c_ctx[None], c, jnp.zeros((SUBLANES - 1 - n_dec, D_MODEL), F32)], axis=0)
    mod = _modulation(cond, mod_w, mod_b)

    ev_in = ev_w_in.astype(BF16)
    ev_out = ev_w_out.astype(BF16)
    od_in = od_w_in.astype(BF16)
    od_out = od_w_out.astype(BF16)
    kw = KV_A * HD_A
    ck = cache_attn_k.reshape(cache_attn_k.shape[:3] + (kw,))
    cv = cache_attn_v.reshape(cache_attn_v.shape[:3] + (kw,))
    gla_f_t = state_gla_fwd
    gla_b_t = state_gla_bwd

    def trunk(x, latent):
        new = {"k": [], "v": [], "rf": [], "rb": [], "gf": [], "gb": []}
        for i in range(DEPTH):
            j = i // 2
            fg = final_g if i == DEPTH - 1 else None
            if i % 2 == 0:
                qa, ka, va, ga, zb, gb = _inproj(x, norm_g[i], mod, i, ev_in[j], EV_SPLITS, latent)
                if latent:
                    o_a, _ = _attention(qa, ka, va, ga, ev_qn_g[j], ev_kn_g[j], ck, cv, j)
                    s_init = (state_rwkv_fwd[:, j], state_rwkv_bwd[:, j])
                else:
                    o_a, kn = _attention(qa, ka, va, ga, ev_qn_g[j], ev_kn_g[j])
                    s_init = None
                    new["k"].append(kn)
                    new["v"].append(va)
                p = dict(shift_mu=ev_shift_mu[j], w0=rw_w0[j], w2=rw_w2[j], a0=rw_a0[j], a2=rw_a2[j],
                         k_k=rw_kk[j], k_a=rw_ka[j], r_k=rw_rk[j], ln_g=rw_ln_g[j], ln_b=rw_ln_b[j])
                o_b, fin = _rwkv_mixer(zb, gb, p, s_init)
                new["rf"].append(fin[0])
                new["rb"].append(fin[1])
                x = _outproj([o_a, o_b], [ev_out[j, :D_HALF], ev_out[j, D_HALF:]], x, mod, i, latent, fg)
            else:
                q, k, v, g, gl = _inproj(x, norm_g[i], mod, i, od_in[j], OD_SPLITS, latent)
                s_init = (gla_f_t, gla_b_t) if latent else None
                o, fin = _gla_mixer(q, k, v, g, gl, gla_w2[j], gla_b[j], gla_ln_g[j], s_init, j)
                new["gf"].append(fin[0])
                new["gb"].append(fin[1])
                x = _outproj([o], [od_out[j]], x, mod, i, latent, fg)
        return x, new

    y_prompt, new = trunk(x_prompt, False)
    y_sample, _ = trunk(x_sample, True)
    B, T = x_prompt.shape[:2]
    heads = lambda t: t.reshape(B, T, KV_A, HD_A)
    return (y_prompt, y_sample,
            jnp.stack([heads(t) for t in new["k"]], axis=1), jnp.stack([heads(t) for t in new["v"]], axis=1),
            jnp.stack(new["rf"], axis=1), jnp.stack(new["rb"], axis=1),
            jnp.stack(new["gf"], axis=1), jnp.stack(new["gb"], axis=1))
```

```python
import functools

import numpy as np
import jax
import jax.numpy as jnp
from jax import lax
from jax.experimental import pallas as pl
from jax.experimental.pallas import tpu as pltpu

F32 = jnp.float32
BF16 = jnp.bfloat16

D_MODEL = 1024
DEPTH = 4
GRID_W = 64
D_HALF = D_MODEL // 2
HD_A = 64
H_A = D_HALF // HD_A
KV_A = H_A // 4
ROPE_BASE = 10000.0
HS_B = 64
H_B = D_HALF // HS_B
W_LORA = 64
A_LORA = 64
RWKV_DECAY_SCALE = 0.606531
GN_EPS = 64e-5
B_SHIFT = 3 * D_HALF + 2 * W_LORA + 2 * A_LORA
H_C = 4
DK_C = D_MODEL // 2 // H_C
DV_C = D_MODEL // H_C
G_LORA = 16
GLA_TAU = 16.0
EPS = 1e-6

EV_SPLITS = (H_A * HD_A, KV_A * HD_A, KV_A * HD_A, D_HALF, B_SHIFT, D_HALF)
OD_SPLITS = (H_C * DK_C, H_C * DK_C, D_MODEL, D_MODEL, 2 * G_LORA)

LANES = 128
SUBLANES = 8
VMEM_LIMIT_BYTES = 56 * 1024 * 1024

ROW_TILE = 256
Q_TILE = 256
GLA_CHUNK = 64
SCAN_T = 8


def _params(*sem):
    return pltpu.CompilerParams(dimension_semantics=sem, vmem_limit_bytes=VMEM_LIMIT_BYTES)


def _silu(x):
    return x * jax.nn.sigmoid(x)


def _dot(a, b):
    return jnp.dot(a, b, preferred_element_type=F32)


def _seg_sum(x, ones_bd):
    hi = x.astype(BF16)
    r1 = x - hi.astype(F32)
    mid = r1.astype(BF16)
    lo = (r1 - mid.astype(F32)).astype(BF16)
    return _dot(hi, ones_bd) + _dot(mid, ones_bd) + _dot(lo, ones_bd)


def _get(ref):
    return ref[...].reshape(ref.shape[-2:])


def _put(ref, val):
    ref[...] = val.reshape(ref.shape)


def _block_diag_ones(n, blk):
    i = np.arange(n) // blk
    return jnp.asarray((i[:, None] == i[None, :]).astype(np.float32), dtype=BF16)


def _mod_kernel(cond_ref, w_ref, b_ref, o_ref):
    s = _silu(cond_ref[...])
    o_ref[0] = _dot(s.astype(BF16), w_ref[0].astype(BF16)) + b_ref[0]


def _modulation(cond, mod_w, mod_b):
    n = cond.shape[0]
    return pl.pallas_call(
        _mod_kernel,
        grid=(DEPTH, 3),
        in_specs=[
            pl.BlockSpec((n, D_MODEL), lambda i, j: (0, 0)),
            pl.BlockSpec((1, D_MODEL, D_MODEL), lambda i, j: (i, 0, j)),
            pl.BlockSpec((1, 1, D_MODEL), lambda i, j: (i, 0, j)),
        ],
        out_specs=pl.BlockSpec((1, n, D_MODEL), lambda i, j: (i, 0, j)),
        out_shape=jax.ShapeDtypeStruct((DEPTH, n, 3 * D_MODEL), F32),
        compiler_params=_params("parallel", "parallel"),
    )(cond, mod_w, mod_b.reshape(DEPTH, 1, 3 * D_MODEL))


def _mod_row(latent):
    return (1 + pl.program_id(0)) if latent else 0


def _inproj_kernel(latent, splits, x_ref, g_ref, sh_ref, sc_ref, w_ref, *out_refs):
    r = _mod_row(latent)
    x = x_ref[0]
    y = x * lax.rsqrt(jnp.mean(x * x, axis=-1, keepdims=True) + EPS) * g_ref[...]
    shift = sh_ref[0, pl.ds(r, 1), :]
    scale = sc_ref[0, pl.ds(r, 1), :]
    h = (y * (1.0 + scale) + shift).astype(BF16)
    off = 0
    for o_ref, n in zip(out_refs, splits):
        o_ref[0] = _dot(h, w_ref[:, off:off + n])
        off += n


def _inproj(x, norm_g, mod, layer, w_bf16, splits, latent):
    B, T, _ = x.shape
    cols = w_bf16.shape[1]
    nrow = mod.shape[1]
    return pl.pallas_call(
        functools.partial(_inproj_kernel, latent, splits),
        grid=(B, T // ROW_TILE),
        in_specs=[
            pl.BlockSpec((1, ROW_TILE, D_MODEL), lambda b, i: (b, i, 0)),
            pl.BlockSpec((1, D_MODEL), lambda b, i: (0, 0)),
            pl.BlockSpec((1, nrow, D_MODEL), lambda b, i: (layer, 0, 0)),
            pl.BlockSpec((1, nrow, D_MODEL), lambda b, i: (layer, 0, 1)),
            pl.BlockSpec((D_MODEL, cols), lambda b, i: (0, 0)),
        ],
        out_specs=[pl.BlockSpec((1, ROW_TILE, n), lambda b, i: (b, i, 0)) for n in splits],
        out_shape=[jax.ShapeDtypeStruct((B, T, n), F32) for n in splits],
        compiler_params=_params("parallel", "parallel"),
    )(x, norm_g.reshape(1, D_MODEL), mod, mod, w_bf16)


def _outproj_kernel(latent, final, n_in, *refs):
    o_refs = refs[:n_in]
    w_refs = refs[n_in:2 * n_in]
    x_ref, gate_ref = refs[2 * n_in], refs[2 * n_in + 1]
    rest = refs[2 * n_in + 2:]
    r = _mod_row(latent)
    acc = _dot(o_refs[0][0].astype(BF16), w_refs[0][...])
    for o_ref, w_ref in zip(o_refs[1:], w_refs[1:]):
        acc = acc + _dot(o_ref[0].astype(BF16), w_ref[...])
    y = x_ref[0] + gate_ref[0, pl.ds(r, 1), :] * acc
    if final:
        fg_ref, out_ref = rest
        y = y * lax.rsqrt(jnp.mean(y * y, axis=-1, keepdims=True) + EPS) * fg_ref[...]
    else:
        (out_ref,) = rest
    out_ref[0] = y


def _outproj(outs, ws_bf16, x, mod, layer, latent, final_g=None):
    B, T, _ = x.shape
    nrow = mod.shape[1]
    n_in = len(outs)
    final = final_g is not None
    in_specs = [pl.BlockSpec((1, ROW_TILE, o.shape[-1]), lambda b, i: (b, i, 0)) for o in outs]
    in_specs += [pl.BlockSpec(w.shape, lambda b, i: (0, 0)) for w in ws_bf16]
    in_specs += [
        pl.BlockSpec((1, ROW_TILE, D_MODEL), lambda b, i: (b, i, 0)),
        pl.BlockSpec((1, nrow, D_MODEL), lambda b, i: (layer, 0, 2)),
    ]
    args = list(outs) + list(ws_bf16) + [x, mod]
    if final:
        in_specs.append(pl.BlockSpec((1, D_MODEL), lambda b, i: (0, 0)))
        args.append(final_g.reshape(1, D_MODEL))
    return pl.pallas_call(
        functools.partial(_outproj_kernel, latent, final, n_in),
        grid=(B, T // ROW_TILE),
        in_specs=in_specs,
        out_specs=pl.BlockSpec((1, ROW_TILE, D_MODEL), lambda b, i: (b, i, 0)),
        out_shape=jax.ShapeDtypeStruct((B, T, D_MODEL), F32),
        compiler_params=_params("parallel", "parallel"),
    )(*args)


def _rope_tables(T):
    n_rows = T // GRID_W
    row = jnp.repeat(jnp.arange(n_rows), GRID_W).astype(F32)
    col = jnp.tile(jnp.arange(GRID_W), n_rows).astype(F32)
    n_freq = HD_A // 4
    inv = ROPE_BASE ** (-jnp.arange(n_freq, dtype=F32) / n_freq)
    ang_r = row[:, None] * inv
    ang_c = col[:, None] * inv
    zero = jnp.zeros_like(ang_r)
    cos = jnp.concatenate([jnp.cos(ang_r), jnp.cos(ang_r), jnp.cos(ang_c), jnp.cos(ang_c)], axis=1)
    s1 = jnp.concatenate([-jnp.sin(ang_r), zero, -jnp.sin(ang_c), zero], axis=1)
    s2 = jnp.concatenate([zero, jnp.sin(ang_r), zero, jnp.sin(ang_c)], axis=1)
    return cos, s1, s2


def _rope(x, cos, s1, s2):
    n = x.shape[-1]
    q = HD_A // 4
    return x * cos + pltpu.roll(x, n - q, 1) * s1 + pltpu.roll(x, q, 1) * s2


def _attn_kernel(latent, S, *refs):
    if latent:
        (q_ref, k_ref, v_ref, ga_ref, qg_ref, kg_ref, bd_ref, cos_ref, s1_ref, s2_ref,
         cosk_ref, s1k_ref, s2k_ref, ck_ref, cv_ref, o_ref, km_ref, vm_ref) = refs
    else:
        (q_ref, k_ref, v_ref, ga_ref, qg_ref, kg_ref, bd_ref, o_ref, kn_ref, km_ref, vm_ref) = refs
    bd = bd_ref[...]
    inv_d = 1.0 / HD_A
    kw = KV_A * HD_A

    @pl.when(pl.program_id(1) == 0)
    def _():
        k = k_ref[0]
        kn = k * lax.rsqrt(_seg_sum(k * k, bd[:kw, :kw]) * inv_d + EPS) * kg_ref[...]
        v = v_ref[0]
        if latent:
            kn = _rope(kn, cosk_ref[...], s1k_ref[...], s2k_ref[...])
            k_all = jnp.concatenate([ck_ref[0, 0], kn], axis=0)
            v_all = jnp.concatenate([cv_ref[0, 0], v], axis=0)
        else:
            kn_ref[0] = kn
            k_all, v_all = kn, v
        lane = lax.broadcasted_iota(jnp.int32, (S, kw), 1)
        k_sw = pltpu.roll(k_all, HD_A, 1)
        v_sw = pltpu.roll(v_all, HD_A, 1)
        for j in range(KV_A):
            for half in range(2):
                keep = (lane < HD_A) if half == 0 else (lane >= HD_A)
                src_k, src_v = (k_all, v_all) if j == half else (k_sw, v_sw)
                km_ref[2 * j + half] = jnp.where(keep, src_k, 0.0).astype(BF16)
                vm_ref[2 * j + half] = jnp.where(keep, src_v, 0.0).astype(BF16)

    q = q_ref[0]
    qn = q * lax.rsqrt(_seg_sum(q * q, bd) * inv_d + EPS) * qg_ref[...]
    if latent:
        qn = _rope(qn, cos_ref[...], s1_ref[...], s2_ref[...])
    qb = (qn * HD_A ** -0.5).astype(BF16)
    for m in range(H_A // 2):
        blk = slice(m * LANES, (m + 1) * LANES)
        qs = qb[:, blk]
        acc = None
        for half in range(2):
            j = (2 * m + half) // (H_A // KV_A)
            s = lax.dot_general(qs, km_ref[2 * j + half], (((1,), (1,)), ((), ())),
                                preferred_element_type=F32)
            e = jnp.exp(s - jnp.max(s, axis=-1, keepdims=True))
            l = jnp.sum(e, axis=-1, keepdims=True)
            pv = _dot(e.astype(BF16), vm_ref[2 * j + half]) * (1.0 / l)
            acc = pv if acc is None else acc + pv
        o_ref[0, :, blk] = acc * _silu(ga_ref[0, :, blk])


def _attention(qa, ka, va, ga, qn_g, kn_g, ctx_k=None, ctx_v=None, layer_j=0):
    B, T, _ = qa.shape
    latent = ctx_k is not None
    S = T + (ctx_k.shape[2] if latent else 0)
    kw = KV_A * HD_A
    qw = H_A * HD_A
    bd = _block_diag_ones(qw, HD_A)
    qblk = pl.BlockSpec((1, Q_TILE, qw), lambda b, i: (b, i, 0))
    kblk = pl.BlockSpec((1, T, kw), lambda b, i: (b, 0, 0))
    in_specs = [qblk, kblk, kblk, qblk,
                pl.BlockSpec((1, qw), lambda b, i: (0, 0)),
                pl.BlockSpec((1, kw), lambda b, i: (0, 0)),
                pl.BlockSpec(bd.shape, lambda b, i: (0, 0))]
    args = [qa, ka, va, ga, jnp.tile(qn_g, H_A).reshape(1, -1), jnp.tile(kn_g, KV_A).reshape(1, -1), bd]
    out_specs = [qblk]
    out_shape = [jax.ShapeDtypeStruct((B, T, qw), F32)]
    if latent:
        tabs = _rope_tables(T)
        P = ctx_k.shape[2]
        in_specs += [pl.BlockSpec((Q_TILE, qw), lambda b, i: (i, 0))] * 3
        in_specs += [pl.BlockSpec((T, kw), lambda b, i: (0, 0))] * 3
        in_specs += [pl.BlockSpec((1, 1, P, kw), lambda b, i: (b, layer_j, 0, 0))] * 2
        args += [jnp.tile(t, (1, H_A)) for t in tabs] + [jnp.tile(t, (1, KV_A)) for t in tabs] + [ctx_k, ctx_v]
    else:
        out_specs.append(kblk)
        out_shape.append(jax.ShapeDtypeStruct((B, T, kw), F32))
    res = pl.pallas_call(
        functools.partial(_attn_kernel, latent, S),
        grid=(B, T // Q_TILE),
        in_specs=in_specs,
        out_specs=out_specs,
        out_shape=out_shape,
        scratch_shapes=[pltpu.VMEM((2 * KV_A, S, kw), BF16), pltpu.VMEM((2 * KV_A, S, kw), BF16)],
        compiler_params=_params("parallel", "arbitrary"),
    )(*args)
    return res if not latent else (res[0], None)


def _rwkv_prep_kernel(n_t, z_ref, zp_ref, zn_ref, mu_ref, w0_ref, a0_ref, w2_ref, a2_ref,
                      kkg_ref, ka_ref, rk_ref, bd_ref,
                      wf_ref, wb_ref, kdf_ref, kdb_ref, kaf_ref, kab_ref, nkk_ref, r_ref, v_ref, bonus_ref):
    i = pl.program_id(1)
    z = z_ref[0]
    n = z.shape[0]
    prev_row = jnp.where(i > 0, zp_ref[0, SUBLANES - 1:SUBLANES, :], 0.0)
    next_row = jnp.where(i < n_t - 1, zn_ref[0, 0:1, :], 0.0)
    rows = lax.broadcasted_iota(jnp.int32, (n, 1), 0)
    zp = jnp.where(rows == 0, prev_row, pltpu.roll(z, 1, 0))
    zn = jnp.where(rows == n - 1, next_row, pltpu.roll(z, n - 1, 0))
    zs = z + mu_ref[...] * (0.5 * (zp + zn) - z)

    rb = zs[:, 0:D_HALF]
    kb = zs[:, D_HALF:2 * D_HALF]
    vb = zs[:, 2 * D_HALF:3 * D_HALF]
    lw = zs[:, 3 * D_HALF:3 * D_HALF + 2 * W_LORA]
    la = zs[:, 3 * D_HALF + 2 * W_LORA:]
    w = jnp.exp(-RWKV_DECAY_SCALE * jax.nn.sigmoid(w0_ref[...] + _dot(jnp.tanh(lw).astype(BF16), w2_ref[...])))
    a = jax.nn.sigmoid(a0_ref[...] + _dot(la.astype(BF16), a2_ref[...]))
    bd = bd_ref[...]
    kk = kb * kkg_ref[...]
    kk = kk * lax.rsqrt(_seg_sum(kk * kk, bd) + 1e-12)
    ka = ka_ref[...]
    a_f, a_b = a[:, :D_HALF], a[:, D_HALF:]
    kd_f = kb * (1.0 + (a_f - 1.0) * ka)
    kd_b = kb * (1.0 + (a_b - 1.0) * ka)
    _put(wf_ref, w[:, :D_HALF])
    _put(wb_ref, w[:, D_HALF:])
    _put(kdf_ref, kd_f)
    _put(kdb_ref, kd_b)
    _put(kaf_ref, kk * a_f)
    _put(kab_ref, kk * a_b)
    _put(nkk_ref, -kk)
    _put(r_ref, rb)
    _put(v_ref, vb)
    _put(bonus_ref, _seg_sum(rb * rk_ref[...] * (kd_f + kd_b), bd) * vb)


def _block_diag2(m0, m1):
    z = jnp.zeros_like(m0)
    return jnp.concatenate([jnp.concatenate([m0, z], axis=1), jnp.concatenate([z, m1], axis=1)], axis=0)


def _rwkv_prep(zb, shift_mu, w0, w2, a0, a2, k_k, k_a, r_k):
    B, T, _ = zb.shape
    n_t = T // ROW_TILE
    per_tile = ROW_TILE // SUBLANES
    bd = _block_diag_ones(D_HALF, HS_B)
    row = lambda x: x.reshape(1, -1)
    vec = pl.BlockSpec((1, D_HALF), lambda b, i: (0, 0))
    vec2 = pl.BlockSpec((1, 2 * D_HALF), lambda b, i: (0, 0))
    out = pl.BlockSpec((1, ROW_TILE, D_HALF), lambda b, i: (b, i, 0))
    return pl.pallas_call(
        functools.partial(_rwkv_prep_kernel, n_t),
        grid=(B, n_t),
        in_specs=[
            pl.BlockSpec((1, ROW_TILE, B_SHIFT), lambda b, i: (b, i, 0)),
            pl.BlockSpec((1, SUBLANES, B_SHIFT), lambda b, i: (b, jnp.maximum(i * per_tile - 1, 0), 0)),
            pl.BlockSpec((1, SUBLANES, B_SHIFT),
                         lambda b, i: (b, jnp.minimum((i + 1) * per_tile, n_t * per_tile - 1), 0)),
            pl.BlockSpec((1, B_SHIFT), lambda b, i: (0, 0)),
            vec2, vec2,
            pl.BlockSpec((2 * W_LORA, 2 * D_HALF), lambda b, i: (0, 0)),
            pl.BlockSpec((2 * A_LORA, 2 * D_HALF), lambda b, i: (0, 0)),
            vec, vec, vec,
            pl.BlockSpec(bd.shape, lambda b, i: (0, 0)),
        ],
        out_specs=[out] * 10,
        out_shape=[jax.ShapeDtypeStruct((B, T, D_HALF), F32)] * 10,
        compiler_params=_params("parallel", "parallel"),
    )(zb, zb, zb, row(shift_mu), row(w0), row(a0),
      _block_diag2(w2[0], w2[1]).astype(BF16), _block_diag2(a2[0], a2[1]).astype(BF16),
      row(k_k), row(k_a), row(r_k), bd)


def _scan_steps(vh, n_steps, reverse, kvecs, vblock, yput, s_ref):
    def t_step(i, carry):
        t = (n_steps - 1 - i) if reverse else i
        w, nkk, kka, kd, r = kvecs(t)

        def v_step(g, c):
            base = pl.multiple_of(g * SUBLANES, SUBLANES)
            vblk = vblock(t, base)
            ys = []
            for u in range(SUBLANES):
                s = s_ref[base + u]
                sa = jnp.sum(s * nkk, axis=0, keepdims=True)
                s = s * w + sa * kka + vblk[u:u + 1, :] * kd
                s_ref[base + u] = s
                ys.append(jnp.sum(s * r, axis=0, keepdims=True))
            yput(t, base, jnp.concatenate(ys, axis=0))
            return c

        lax.fori_loop(0, vh // SUBLANES, v_step, 0)
        return carry

    lax.fori_loop(0, n_steps, t_step, 0)


CHAIN_ROWS = 32
HEADS_PER_GROUP = LANES // CHAIN_ROWS


def _step_rows(ref, t, vs):
    if vs == 1:
        return ref[:, t, :]
    return jnp.concatenate([jnp.broadcast_to(ref[b, t:t + 1, :], (vs, ref.shape[2])) for b in range(ref.shape[0])],
                           axis=0)


def _chain_pair(x1, x2):
    lo = lax.broadcasted_iota(jnp.int32, (x1.shape[0], LANES), 1) < HS_B
    pieces = []
    for h in range(HEADS_PER_GROUP):
        blk = slice((h // 2) * LANES, (h // 2 + 1) * LANES)
        a, b = x1[:, blk], x2[:, blk]
        if h % 2 == 0:
            pieces.append(jnp.where(lo, a, pltpu.roll(b, HS_B, 1)))
        else:
            pieces.append(jnp.where(lo, pltpu.roll(a, HS_B, 1), b))
    return jnp.concatenate(pieces, axis=0).T


def _unchain_pair(y0, y1):
    yt = jnp.concatenate([y0, y1], axis=0).T
    lo = lax.broadcasted_iota(jnp.int32, (CHAIN_ROWS, LANES), 1) < HS_B
    rows = [yt[h * CHAIN_ROWS:(h + 1) * CHAIN_ROWS] for h in range(HEADS_PER_GROUP)]
    out0, out1 = [], []
    for m in range(HEADS_PER_GROUP // 2):
        even, odd = rows[2 * m], rows[2 * m + 1]
        out0.append(jnp.where(lo, even, pltpu.roll(odd, HS_B, 1)))
        out1.append(jnp.where(lo, pltpu.roll(even, HS_B, 1), odd))
    return jnp.concatenate(out0, axis=1), jnp.concatenate(out1, axis=1)


def _rwkv_scan_kernel(reverse, vs, w_ref, nkk_ref, kka_ref, kd_ref, r_ref, v_ref, s0_ref, y_ref, s_ref,
                      kbuf, vbuf, ybuf):
    vh = HS_B // vs

    @pl.when(pl.program_id(1) == 0)
    def _():
        s_ref[...] = s0_ref[...]

    split = lax.broadcasted_iota(jnp.int32, (vh, LANES), 1) % vs
    pairs = ((w_ref, nkk_ref), (kka_ref, kd_ref), (r_ref, v_ref))
    for t in range(SCAN_T):
        for p, (a_ref, b_ref) in enumerate(pairs):
            both = _chain_pair(_step_rows(a_ref, t, vs), _step_rows(b_ref, t, vs))
            kbuf[2 * p, t] = both[:HS_B]
            if p < 2:
                kbuf[2 * p + 1, t] = both[HS_B:]
            else:
                v_all = both[HS_B:]
                v_own = v_all[:vh]
                for q in range(1, vs):
                    v_own = jnp.where(split == q, v_all[q * vh:(q + 1) * vh], v_own)
                vbuf[t] = v_own

    def yput(t, base, val):
        ybuf[t, pl.ds(base, SUBLANES), :] = val

    _scan_steps(vh, SCAN_T, reverse,
                lambda t: (kbuf[0, t], kbuf[1, t], kbuf[2, t], kbuf[3, t], kbuf[4, t]),
                lambda t, base: vbuf[t, pl.ds(base, SUBLANES), :], yput, s_ref)

    def y_all(t):
        y = ybuf[t]
        if vs == 1:
            return y
        return jnp.concatenate([jnp.where(split == q, y, 0.0) for q in range(vs)], axis=0)

    for t in range(0, SCAN_T, 2):
        for tt, out in zip((t, t + 1), _unchain_pair(y_all(t), y_all(t + 1))):
            if vs > 1:
                out = jnp.sum(out.reshape(CHAIN_ROWS // vs, vs, out.shape[1]), axis=1)
            y_ref[:, tt, :] = out


def _rwkv_scan(w, nkk, kka, kd, r, v, s0, reverse):
    B, T, W = w.shape
    vs = CHAIN_ROWS // B
    vh = HS_B // vs
    gw = HEADS_PER_GROUP * HS_B
    n_t = T // SCAN_T
    tblk = (lambda t: n_t - 1 - t) if reverse else (lambda t: t)
    vec = pl.BlockSpec((B, SCAN_T, gw), lambda g, t: (0, tblk(t), g))
    state = pl.BlockSpec((vh, HS_B, LANES), lambda g, t: (0, 0, g))
    return pl.pallas_call(
        functools.partial(_rwkv_scan_kernel, reverse, vs),
        grid=(W // gw, n_t),
        in_specs=[vec] * 6 + [state],
        out_specs=[vec, state],
        out_shape=[jax.ShapeDtypeStruct((B, T, W), F32), jax.ShapeDtypeStruct((vh, HS_B, W // gw * LANES), F32)],
        scratch_shapes=[pltpu.VMEM((5, SCAN_T, HS_B, LANES), F32), pltpu.VMEM((SCAN_T, vh, LANES), F32),
                        pltpu.VMEM((SCAN_T, vh, LANES), F32)],
        compiler_params=_params("parallel", "arbitrary"),
    )(w, nkk, kka, kd, r, v, s0)


def _rwkv_post_kernel(yf_ref, yb_ref, bonus_ref, gb_ref, lng_ref, lnb_ref, bd_ref, o_ref):
    bd = bd_ref[...]
    y = _get(yf_ref) + _get(yb_ref)
    inv_n = 1.0 / HS_B
    d = y - _seg_sum(y, bd) * inv_n
    var = _seg_sum(d * d, bd) * inv_n
    yn = d * lax.rsqrt(var + GN_EPS) * lng_ref[...] + lnb_ref[...]
    o_ref[0] = (yn + bonus_ref[0]) * _silu(gb_ref[0])


def _rwkv_post(y_f, y_b, bonus, gb, ln_g, ln_b):
    B, T, _ = y_f.shape
    bd = _block_diag_ones(D_HALF, HS_B)
    blk = pl.BlockSpec((1, ROW_TILE, D_HALF), lambda b, i: (b, i, 0))
    vec = pl.BlockSpec((1, D_HALF), lambda b, i: (0, 0))
    return pl.pallas_call(
        _rwkv_post_kernel,
        grid=(B, T // ROW_TILE),
        in_specs=[blk, blk, blk, blk, vec, vec, pl.BlockSpec(bd.shape, lambda b, i: (0, 0))],
        out_specs=blk,
        out_shape=jax.ShapeDtypeStruct((B, T, D_HALF), F32),
        compiler_params=_params("parallel", "parallel"),
    )(y_f, y_b, bonus, gb, ln_g.reshape(1, -1), ln_b.reshape(1, -1), bd)


def _rwkv_mixer(zb, gb, p, s_init):
    B, T, _ = zb.shape
    (w_f, w_b, kd_f, kd_b, ka_f, ka_b, nkk, r, v, bonus) = _rwkv_prep(
        zb, p["shift_mu"], p["w0"], p["w2"], p["a0"], p["a2"], p["k_k"], p["k_a"], p["r_k"])
    vs = CHAIN_ROWS // B
    vh = HS_B // vs
    chains = H_B * B * vs
    ys, fin = [], []
    for d, (w_d, ka_d, kd_d) in enumerate(((w_f, ka_f, kd_f), (w_b, ka_b, kd_b))):
        if s_init is None:
            s0 = jnp.zeros((vh, HS_B, chains), F32)
        else:
            s0 = s_init[d].reshape(B, H_B, vs, vh, HS_B).transpose(3, 4, 1, 0, 2).reshape(vh, HS_B, chains)
        y, s_fin = _rwkv_scan(w_d, nkk, ka_d, kd_d, r, v, s0, reverse=(d == 1))
        ys.append(y)
        fin.append(s_fin.reshape(vh, HS_B, H_B, B, vs).transpose(3, 2, 4, 0, 1).reshape(B, H_B, HS_B, HS_B))
    return _rwkv_post(ys[0], ys[1], bonus, gb, p["ln_g"], p["ln_b"]), fin


def _chunk_cumsum(x, chunk, suffix):
    T = x.shape[0]
    pos = lax.broadcasted_iota(jnp.int32, (T, 1), 0) % chunk
    step = 1
    while step < chunk:
        if suffix:
            x = x + jnp.where(pos < chunk - step, pltpu.roll(x, T - step, 0), 0.0)
        else:
            x = x + jnp.where(pos >= step, pltpu.roll(x, step, 0), 0.0)
        step *= 2
    return x


def _gla_kernel(T, q_ref, k_ref, v_ref, g_ref, gl_ref, w2_ref, gb_ref, lng_ref, tri_ref, s0f_ref, s0b_ref,
                o_ref, sf_ref, sb_ref, b_ref, acc_ref, qb_ref, dec_ref, u_ref, sst_ref, st_ref):
    C = GLA_CHUNK
    n_c = T // C
    pre = _dot(gl_ref[0].astype(BF16), w2_ref[0]) + gb_ref[0]
    la = jax.nn.log_sigmoid(pre) * (1.0 / GLA_TAU)
    b_ref[:, :DK_C] = _chunk_cumsum(la[:, :DK_C], C, suffix=False)
    b_ref[:, DK_C:] = _chunk_cumsum(la[:, DK_C:], C, suffix=True)
    qscale = DK_C ** -0.5
    contract_last = (((1,), (1,)), ((), ()))
    contract_first = (((0,), (0,)), ((), ()))
    vis_f = tri_ref[0]
    vis_b = tri_ref[1]

    def chunk_rows(c):
        return pl.ds(pl.multiple_of(c * C, C), C)

    def dec_rows(c):
        return pl.ds(pl.multiple_of(c * SUBLANES, SUBLANES), SUBLANES)

    def intra(c, carry):
        rows = chunk_rows(c)
        q = q_ref[0, rows, :] * qscale
        k = k_ref[0, rows, :]
        vc = v_ref[0, rows, :].astype(BF16)
        b = b_ref[rows, :]
        btot_f = b[C - 1:C, :DK_C]
        btot_b = b[0:1, DK_C:]
        qe, ke, qb, kl = [], [], [], []
        for bd, btot in ((b[:, :DK_C], btot_f), (b[:, DK_C:], btot_b)):
            mref = 0.5 * btot
            e_half = jnp.exp(mref)
            q_up = q * jnp.exp(bd - mref)
            k_dn = k * jnp.exp(mref - bd)
            qe.append(q_up.astype(BF16))
            ke.append(k_dn.astype(BF16))
            qb.append((q_up * e_half).astype(BF16))
            kl.append((k_dn * e_half).astype(BF16))
        sc = lax.dot_general(jnp.concatenate(qe, axis=0), jnp.concatenate(ke, axis=0), contract_last,
                             preferred_element_type=F32)
        att = sc[:C] * vis_f + pltpu.roll(sc[C:], C, 1) * vis_b
        acc_ref[rows, :] = _dot(att[:, :C].astype(BF16), vc)
        u_ref[c] = lax.dot_general(vc, jnp.concatenate(kl, axis=1), contract_first, preferred_element_type=F32)
        qb_ref[rows, :] = jnp.concatenate(qb, axis=1)
        dec_ref[dec_rows(c), :] = jnp.broadcast_to(
            jnp.exp(jnp.concatenate([btot_f, btot_b], axis=1)), (SUBLANES, 2 * DK_C))
        return carry

    lax.fori_loop(0, n_c, intra, 0, unroll=4)

    st_ref[0] = s0f_ref[0, 0, 0]
    st_ref[1] = s0b_ref[0, 0, 0]

    def states(i, carry):
        for d, c in ((0, i), (1, n_c - 1 - i)):
            lanes = slice(d * DK_C, (d + 1) * DK_C)
            st = st_ref[d]
            sst_ref[c, :, lanes] = st.astype(BF16)
            st_ref[d] = st * dec_ref[pl.ds(pl.multiple_of(c * SUBLANES, SUBLANES), 1), lanes] + u_ref[c, :, lanes]
        return carry

    lax.fori_loop(0, n_c, states, 0)
    sf_ref[0, 0] = st_ref[0]
    sb_ref[0, 0] = st_ref[1]

    def inter(c, carry):
        rows = chunk_rows(c)
        acc_ref[rows, :] = acc_ref[rows, :] + lax.dot_general(qb_ref[rows, :], sst_ref[c], contract_last,
                                                              preferred_element_type=F32)
        return carry

    lax.fori_loop(0, n_c, inter, 0, unroll=4)

    o = acc_ref[...]
    o = o * lax.rsqrt(jnp.mean(o * o, axis=-1, keepdims=True) + EPS) * lng_ref[...]
    o_ref[0] = o * _silu(g_ref[0])


def _gla_mixer(q, k, v, g, gl, gw2, gbias, ln_g, s_init, layer_j):
    B, T, _ = q.shape
    C = GLA_CHUNK
    idx = np.arange(C)
    tri = np.zeros((2, C, LANES), np.float32)
    tri[0, :, :C] = idx[:, None] >= idx[None, :]
    tri[1, :, :C] = idx[:, None] <= idx[None, :]
    tri = jnp.asarray(tri)
    w2 = jnp.stack([_block_diag2(gw2[0][:, h * DK_C:(h + 1) * DK_C], gw2[1][:, h * DK_C:(h + 1) * DK_C])
                    for h in range(H_C)]).astype(BF16)
    gb = jnp.stack([jnp.concatenate([gbias[0][h * DK_C:(h + 1) * DK_C], gbias[1][h * DK_C:(h + 1) * DK_C]])
                    for h in range(H_C)]).reshape(H_C, 1, 2 * DK_C)
    if s_init is None:
        zeros = jnp.zeros((1, 1, 1, DV_C, DK_C), F32)
        s0f = s0b = zeros
        s_spec = pl.BlockSpec((1, 1, 1, DV_C, DK_C), lambda b, h: (0, 0, 0, 0, 0))
    else:
        s0f, s0b = (jnp.swapaxes(s, -1, -2) for s in s_init)
        s_spec = pl.BlockSpec((1, 1, 1, DV_C, DK_C), lambda b, h: (b, layer_j, h, 0, 0))
    o, sf, sb = pl.pallas_call(
        functools.partial(_gla_kernel, T),
        grid=(B, H_C),
        in_specs=[
            pl.BlockSpec((1, T, DK_C), lambda b, h: (b, 0, h)),
            pl.BlockSpec((1, T, DK_C), lambda b, h: (b, 0, h)),
            pl.BlockSpec((1, T, DV_C), lambda b, h: (b, 0, h)),
            pl.BlockSpec((1, T, DV_C), lambda b, h: (b, 0, h)),
            pl.BlockSpec((1, T, 2 * G_LORA), lambda b, h: (b, 0, 0)),
            pl.BlockSpec((1, 2 * G_LORA, 2 * DK_C), lambda b, h: (h, 0, 0)),
            pl.BlockSpec((1, 1, 2 * DK_C), lambda b, h: (h, 0, 0)),
            pl.BlockSpec((1, DV_C), lambda b, h: (0, 0)),
            pl.BlockSpec((2, C, LANES), lambda b, h: (0, 0, 0)),
            s_spec, s_spec,
        ],
        out_specs=[
            pl.BlockSpec((1, T, DV_C), lambda b, h: (b, 0, h)),
            pl.BlockSpec((1, 1, DV_C, DK_C), lambda b, h: (b, h, 0, 0)),
            pl.BlockSpec((1, 1, DV_C, DK_C), lambda b, h: (b, h, 0, 0)),
        ],
        out_shape=[
            jax.ShapeDtypeStruct((B, T, H_C * DV_C), F32),
            jax.ShapeDtypeStruct((B, H_C, DV_C, DK_C), F32),
            jax.ShapeDtypeStruct((B, H_C, DV_C, DK_C), F32),
        ],
        scratch_shapes=[
            pltpu.VMEM((T, 2 * DK_C), F32), pltpu.VMEM((T, DV_C), F32),
            pltpu.VMEM((T, 2 * DK_C), BF16), pltpu.VMEM((T // C * SUBLANES, 2 * DK_C), F32),
            pltpu.VMEM((T // C, DV_C, 2 * DK_C), F32), pltpu.VMEM((T // C, DV_C, 2 * DK_C), BF16),
            pltpu.VMEM((2, DV_C, DK_C), F32),
        ],
        compiler_params=_params("parallel", "parallel"),
    )(q, k, v, g, gl, w2, gb, ln_g.reshape(1, -1), tri, s0f, s0b)
    return o, (jnp.swapaxes(sf, -1, -2), jnp.swapaxes(sb, -1, -2))


def kernel(x_prompt, x_sample, c, cache_attn_k, cache_attn_v, state_rwkv_fwd, state_rwkv_bwd, state_gla_fwd, state_gla_bwd, c_ctx, norm_g, mod_w, mod_b, ev_w_in, ev_w_out, ev_qn_g, ev_kn_g, ev_shift_mu, rw_w0, rw_w2, rw_a0, rw_a2, rw_kk, rw_ka, rw_rk, rw_ln_g, rw_ln_b, od_w_in, od_w_out, gla_w2, gla_b, gla_ln_g, final_g):
    n_dec = c.shape[0]
    cond = jnp.concatenate([c_ctx[None], c, jnp.zeros((SUBLANES - 1 - n_dec, D_MODEL), F32)], axis=0)
    mod = _modulation(cond, mod_w, mod_b)

    ev_in = ev_w_in.astype(BF16)
    ev_out = ev_w_out.astype(BF16)
    od_in = od_w_in.astype(BF16)
    od_out = od_w_out.astype(BF16)
    kw = KV_A * HD_A
    ck = cache_attn_k.reshape(cache_attn_k.shape[:3] + (kw,))
    cv = cache_attn_v.reshape(cache_attn_v.shape[:3] + (kw,))
    gla_f_t = state_gla_fwd
    gla_b_t = state_gla_bwd

    def trunk(x, latent):
        new = {"k": [], "v": [], "rf": [], "rb": [], "gf": [], "gb": []}
        for i in range(DEPTH):
            j = i // 2
            fg = final_g if i == DEPTH - 1 else None
            if i % 2 == 0:
                qa, ka, va, ga, zb, gb = _inproj(x, norm_g[i], mod, i, ev_in[j], EV_SPLITS, latent)
                if latent:
                    o_a, _ = _attention(qa, ka, va, ga, ev_qn_g[j], ev_kn_g[j], ck, cv, j)
                    s_init = (state_rwkv_fwd[:, j], state_rwkv_bwd[:, j])
                else:
                    o_a, kn = _attention(qa, ka, va, ga, ev_qn_g[j], ev_kn_g[j])
                    s_init = None
                    new["k"].append(kn)
                    new["v"].append(va)
                p = dict(shift_mu=ev_shift_mu[j], w0=rw_w0[j], w2=rw_w2[j], a0=rw_a0[j], a2=rw_a2[j],
                         k_k=rw_kk[j], k_a=rw_ka[j], r_k=rw_rk[j], ln_g=rw_ln_g[j], ln_b=rw_ln_b[j])
                o_b, fin = _rwkv_mixer(zb, gb, p, s_init)
                new["rf"].append(fin[0])
                new["rb"].append(fin[1])
                x = _outproj([o_a, o_b], [ev_out[j, :D_HALF], ev_out[j, D_HALF:]], x, mod, i, latent, fg)
            else:
                q, k, v, g, gl = _inproj(x, norm_g[i], mod, i, od_in[j], OD_SPLITS, latent)
                s_init = (gla_f_t, gla_b_t) if latent else None
                o, fin = _gla_mixer(q, k, v, g, gl, gla_w2[j], gla_b[j], gla_ln_g[j], s_init, j)
                new["gf"].append(fin[0])
                new["gb"].append(fin[1])
                x = _outproj([o], [od_out[j]], x, mod, i, latent, fg)
        return x, new

    y_prompt, new = trunk(x_prompt, False)
    y_sample, _ = trunk(x_sample, True)
    B, T = x_prompt.shape[:2]
    heads = lambda t: t.reshape(B, T, KV_A, HD_A)
    return (y_prompt, y_sample,
            jnp.stack([heads(t) for t in new["k"]], axis=1), jnp.stack([heads(t) for t in new["v"]], axis=1),
            jnp.stack(new["rf"], axis=1), jnp.stack(new["rb"], axis=1),
            jnp.stack(new["gf"], axis=1), jnp.stack(new["gb"], axis=1))
```

```python
import functools

import numpy as np
import jax
import jax.numpy as jnp
from jax import lax
from jax.experimental import pallas as pl
from jax.experimental.pallas import tpu as pltpu

F32 = jnp.float32
BF16 = jnp.bfloat16

D_MODEL = 1024
DEPTH = 4
GRID_W = 64
D_HALF = D_MODEL // 2
HD_A = 64
H_A = D_HALF // HD_A
KV_A = H_A // 4
ROPE_BASE = 10000.0
HS_B = 64
H_B = D_HALF // HS_B
W_LORA = 64
A_LORA = 64
RWKV_DECAY_SCALE = 0.606531
GN_EPS = 64e-5
B_SHIFT = 3 * D_HALF + 2 * W_LORA + 2 * A_LORA
H_C = 4
DK_C = D_MODEL // 2 // H_C
DV_C = D_MODEL // H_C
G_LORA = 16
GLA_TAU = 16.0
EPS = 1e-6

EV_SPLITS = (H_A * HD_A, KV_A * HD_A, KV_A * HD_A, D_HALF, B_SHIFT, D_HALF)
OD_SPLITS = (H_C * DK_C, H_C * DK_C, D_MODEL, D_MODEL, 2 * G_LORA)

LANES = 128
SUBLANES = 8
VMEM_LIMIT_BYTES = 56 * 1024 * 1024

ROW_TILE = 256
Q_TILE = 256
GLA_CHUNK = 64
SCAN_T = 8


def _params(*sem):
    return pltpu.CompilerParams(dimension_semantics=sem, vmem_limit_bytes=VMEM_LIMIT_BYTES)


def _silu(x):
    return x * jax.nn.sigmoid(x)


def _dot(a, b):
    return jnp.dot(a, b, preferred_element_type=F32)


def _seg_sum(x, ones_bd):
    hi = x.astype(BF16)
    r1 = x - hi.astype(F32)
    mid = r1.astype(BF16)
    lo = (r1 - mid.astype(F32)).astype(BF16)
    return _dot(hi, ones_bd) + _dot(mid, ones_bd) + _dot(lo, ones_bd)


def _get(ref):
    return ref[...].reshape(ref.shape[-2:])


def _put(ref, val):
    ref[...] = val.reshape(ref.shape)


def _block_diag_ones(n, blk):
    i = np.arange(n) // blk
    return jnp.asarray((i[:, None] == i[None, :]).astype(np.float32), dtype=BF16)


def _mod_kernel(cond_ref, w_ref, b_ref, o_ref):
    s = _silu(cond_ref[...])
    o_ref[0] = _dot(s.astype(BF16), w_ref[0].astype(BF16)) + b_ref[0]


def _modulation(cond, mod_w, mod_b):
    n = cond.shape[0]
    return pl.pallas_call(
        _mod_kernel,
        grid=(DEPTH, 3),
        in_specs=[
            pl.BlockSpec((n, D_MODEL), lambda i, j: (0, 0)),
            pl.BlockSpec((1, D_MODEL, D_MODEL), lambda i, j: (i, 0, j)),
            pl.BlockSpec((1, 1, D_MODEL), lambda i, j: (i, 0, j)),
        ],
        out_specs=pl.BlockSpec((1, n, D_MODEL), lambda i, j: (i, 0, j)),
        out_shape=jax.ShapeDtypeStruct((DEPTH, n, 3 * D_MODEL), F32),
        compiler_params=_params("parallel", "parallel"),
    )(cond, mod_w, mod_b.reshape(DEPTH, 1, 3 * D_MODEL))


def _mod_row(latent):
    return (1 + pl.program_id(0)) if latent else 0


def _inproj_kernel(latent, splits, x_ref, g_ref, sh_ref, sc_ref, w_ref, *out_refs):
    r = _mod_row(latent)
    x = x_ref[0]
    y = x * lax.rsqrt(jnp.mean(x * x, axis=-1, keepdims=True) + EPS) * g_ref[...]
    shift = sh_ref[0, pl.ds(r, 1), :]
    scale = sc_ref[0, pl.ds(r, 1), :]
    h = (y * (1.0 + scale) + shift).astype(BF16)
    off = 0
    for o_ref, n in zip(out_refs, splits):
        o_ref[0] = _dot(h, w_ref[:, off:off + n])
        off += n


def _inproj(x, norm_g, mod, layer, w_bf16, splits, latent):
    B, T, _ = x.shape
    cols = w_bf16.shape[1]
    nrow = mod.shape[1]
    return pl.pallas_call(
        functools.partial(_inproj_kernel, latent, splits),
        grid=(B, T // ROW_TILE),
        in_specs=[
            pl.BlockSpec((1, ROW_TILE, D_MODEL), lambda b, i: (b, i, 0)),
            pl.BlockSpec((1, D_MODEL), lambda b, i: (0, 0)),
            pl.BlockSpec((1, nrow, D_MODEL), lambda b, i: (layer, 0, 0)),
            pl.BlockSpec((1, nrow, D_MODEL), lambda b, i: (layer, 0, 1)),
            pl.BlockSpec((D_MODEL, cols), lambda b, i: (0, 0)),
        ],
        out_specs=[pl.BlockSpec((1, ROW_TILE, n), lambda b, i: (b, i, 0)) for n in splits],
        out_shape=[jax.ShapeDtypeStruct((B, T, n), F32) for n in splits],
        compiler_params=_params("parallel", "parallel"),
    )(x, norm_g.reshape(1, D_MODEL), mod, mod, w_bf16)


def _outproj_kernel(latent, final, n_in, *refs):
    o_refs = refs[:n_in]
    w_refs = refs[n_in:2 * n_in]
    x_ref, gate_ref = refs[2 * n_in], refs[2 * n_in + 1]
    rest = refs[2 * n_in + 2:]
    r = _mod_row(latent)
    acc = _dot(o_refs[0][0].astype(BF16), w_refs[0][...])
    for o_ref, w_ref in zip(o_refs[1:], w_refs[1:]):
        acc = acc + _dot(o_ref[0].astype(BF16), w_ref[...])
    y = x_ref[0] + gate_ref[0, pl.ds(r, 1), :] * acc
    if final:
        fg_ref, out_ref = rest
        y = y * lax.rsqrt(jnp.mean(y * y, axis=-1, keepdims=True) + EPS) * fg_ref[...]
    else:
        (out_ref,) = rest
    out_ref[0] = y


def _outproj(outs, ws_bf16, x, mod, layer, latent, final_g=None):
    B, T, _ = x.shape
    nrow = mod.shape[1]
    n_in = len(outs)
    final = final_g is not None
    in_specs = [pl.BlockSpec((1, ROW_TILE, o.shape[-1]), lambda b, i: (b, i, 0)) for o in outs]
    in_specs += [pl.BlockSpec(w.shape, lambda b, i: (0, 0)) for w in ws_bf16]
    in_specs += [
        pl.BlockSpec((1, ROW_TILE, D_MODEL), lambda b, i: (b, i, 0)),
        pl.BlockSpec((1, nrow, D_MODEL), lambda b, i: (layer, 0, 2)),
    ]
    args = list(outs) + list(ws_bf16) + [x, mod]
    if final:
        in_specs.append(pl.BlockSpec((1, D_MODEL), lambda b, i: (0, 0)))
        args.append(final_g.reshape(1, D_MODEL))
    return pl.pallas_call(
        functools.partial(_outproj_kernel, latent, final, n_in),
        grid=(B, T // ROW_TILE),
        in_specs=in_specs,
        out_specs=pl.BlockSpec((1, ROW_TILE, D_MODEL), lambda b, i: (b, i, 0)),
        out_shape=jax.ShapeDtypeStruct((B, T, D_MODEL), F32),
        compiler_params=_params("parallel", "parallel"),
    )(*args)


def _rope_tables(T):
    n_rows = T // GRID_W
    row = jnp.repeat(jnp.arange(n_rows), GRID_W).astype(F32)
    col = jnp.tile(jnp.arange(GRID_W), n_rows).astype(F32)
    n_freq = HD_A // 4
    inv = ROPE_BASE ** (-jnp.arange(n_freq, dtype=F32) / n_freq)
    ang_r = row[:, None] * inv
    ang_c = col[:, None] * inv
    zero = jnp.zeros_like(ang_r)
    cos = jnp.concatenate([jnp.cos(ang_r), jnp.cos(ang_r), jnp.cos(ang_c), jnp.cos(ang_c)], axis=1)
    s1 = jnp.concatenate([-jnp.sin(ang_r), zero, -jnp.sin(ang_c), zero], axis=1)
    s2 = jnp.concatenate([zero, jnp.sin(ang_r), zero, jnp.sin(ang_c)], axis=1)
    return cos, s1, s2


def _rope(x, cos, s1, s2):
    n = x.shape[-1]
    q = HD_A // 4
    return x * cos + pltpu.roll(x, n - q, 1) * s1 + pltpu.roll(x, q, 1) * s2


def _attn_kernel(latent, S, *refs):
    if latent:
        (q_ref, k_ref, v_ref, ga_ref, qg_ref, kg_ref, bd_ref, cos_ref, s1_ref, s2_ref,
         cosk_ref, s1k_ref, s2k_ref, ck_ref, cv_ref, o_ref, km_ref, vm_ref) = refs
    else:
        (q_ref, k_ref, v_ref, ga_ref, qg_ref, kg_ref, bd_ref, o_ref, kn_ref, km_ref, vm_ref) = refs
    bd = bd_ref[...]
    inv_d = 1.0 / HD_A
    kw = KV_A * HD_A

    @pl.when(pl.program_id(1) == 0)
    def _():
        k = k_ref[0]
        kn = k * lax.rsqrt(_seg_sum(k * k, bd[:kw, :kw]) * inv_d + EPS) * kg_ref[...]
        v = v_ref[0]
        if latent:
            kn = _rope(kn, cosk_ref[...], s1k_ref[...], s2k_ref[...])
            k_all = jnp.concatenate([ck_ref[0, 0], kn], axis=0)
            v_all = jnp.concatenate([cv_ref[0, 0], v], axis=0)
        else:
            kn_ref[0] = kn
            k_all, v_all = kn, v
        lane = lax.broadcasted_iota(jnp.int32, (S, kw), 1)
        k_sw = pltpu.roll(k_all, HD_A, 1)
        v_sw = pltpu.roll(v_all, HD_A, 1)
        for j in range(KV_A):
            for half in range(2):
                keep = (lane < HD_A) if half == 0 else (lane >= HD_A)
                src_k, src_v = (k_all, v_all) if j == half else (k_sw, v_sw)
                km_ref[2 * j + half] = jnp.where(keep, src_k, 0.0).astype(BF16)
                vm_ref[2 * j + half] = jnp.where(keep, src_v, 0.0).astype(BF16)

    q = q_ref[0]
    qn = q * lax.rsqrt(_seg_sum(q * q, bd) * inv_d + EPS) * qg_ref[...]
    if latent:
        qn = _rope(qn, cos_ref[...], s1_ref[...], s2_ref[...])
    qb = (qn * HD_A ** -0.5).astype(BF16)
    for m in range(H_A // 2):
        blk = slice(m * LANES, (m + 1) * LANES)
        qs = qb[:, blk]
        acc = None
        for half in range(2):
            j = (2 * m + half) // (H_A // KV_A)
            s = lax.dot_general(qs, km_ref[2 * j + half], (((1,), (1,)), ((), ())),
                                preferred_element_type=F32)
            e = jnp.exp(s - jnp.max(s, axis=-1, keepdims=True))
            l = jnp.sum(e, axis=-1, keepdims=True)
            pv = _dot(e.astype(BF16), vm_ref[2 * j + half]) * (1.0 / l)
            acc = pv if acc is None else acc + pv
        o_ref[0, :, blk] = acc * _silu(ga_ref[0, :, blk])


def _attention(qa, ka, va, ga, qn_g, kn_g, ctx_k=None, ctx_v=None, layer_j=0):
    B, T, _ = qa.shape
    latent = ctx_k is not None
    S = T + (ctx_k.shape[2] if latent else 0)
    kw = KV_A * HD_A
    qw = H_A * HD_A
    bd = _block_diag_ones(qw, HD_A)
    qblk = pl.BlockSpec((1, Q_TILE, qw), lambda b, i: (b, i, 0))
    kblk = pl.BlockSpec((1, T, kw), lambda b, i: (b, 0, 0))
    in_specs = [qblk, kblk, kblk, qblk,
                pl.BlockSpec((1, qw), lambda b, i: (0, 0)),
                pl.BlockSpec((1, kw), lambda b, i: (0, 0)),
                pl.BlockSpec(bd.shape, lambda b, i: (0, 0))]
    args = [qa, ka, va, ga, jnp.tile(qn_g, H_A).reshape(1, -1), jnp.tile(kn_g, KV_A).reshape(1, -1), bd]
    out_specs = [qblk]
    out_shape = [jax.ShapeDtypeStruct((B, T, qw), F32)]
    if latent:
        tabs = _rope_tables(T)
        P = ctx_k.shape[2]
        in_specs += [pl.BlockSpec((Q_TILE, qw), lambda b, i: (i, 0))] * 3
        in_specs += [pl.BlockSpec((T, kw), lambda b, i: (0, 0))] * 3
        in_specs += [pl.BlockSpec((1, 1, P, kw), lambda b, i: (b, layer_j, 0, 0))] * 2
        args += [jnp.tile(t, (1, H_A)) for t in tabs] + [jnp.tile(t, (1, KV_A)) for t in tabs] + [ctx_k, ctx_v]
    else:
        out_specs.append(kblk)
        out_shape.append(jax.ShapeDtypeStruct((B, T, kw), F32))
    res = pl.pallas_call(
        functools.partial(_attn_kernel, latent, S),
        grid=(B, T // Q_TILE),
        in_specs=in_specs,
        out_specs=out_specs,
        out_shape=out_shape,
        scratch_shapes=[pltpu.VMEM((2 * KV_A, S, kw), BF16), pltpu.VMEM((2 * KV_A, S, kw), BF16)],
        compiler_params=_params("parallel", "arbitrary"),
    )(*args)
    return res if not latent else (res[0], None)


def _rwkv_prep_kernel(n_t, z_ref, zp_ref, zn_ref, mu_ref, w0_ref, a0_ref, w2_ref, a2_ref,
                      kkg_ref, ka_ref, rk_ref, bd_ref,
                      wf_ref, wb_ref, kdf_ref, kdb_ref, kaf_ref, kab_ref, nkk_ref, r_ref, v_ref, bonus_ref):
    i = pl.program_id(1)
    z = z_ref[0]
    n = z.shape[0]
    prev_row = jnp.where(i > 0, zp_ref[0, SUBLANES - 1:SUBLANES, :], 0.0)
    next_row = jnp.where(i < n_t - 1, zn_ref[0, 0:1, :], 0.0)
    rows = lax.broadcasted_iota(jnp.int32, (n, 1), 0)
    zp = jnp.where(rows == 0, prev_row, pltpu.roll(z, 1, 0))
    zn = jnp.where(rows == n - 1, next_row, pltpu.roll(z, n - 1, 0))
    zs = z + mu_ref[...] * (0.5 * (zp + zn) - z)

    rb = zs[:, 0:D_HALF]
    kb = zs[:, D_HALF:2 * D_HALF]
    vb = zs[:, 2 * D_HALF:3 * D_HALF]
    lw = zs[:, 3 * D_HALF:3 * D_HALF + 2 * W_LORA]
    la = zs[:, 3 * D_HALF + 2 * W_LORA:]
    w = jnp.exp(-RWKV_DECAY_SCALE * jax.nn.sigmoid(w0_ref[...] + _dot(jnp.tanh(lw).astype(BF16), w2_ref[...])))
    a = jax.nn.sigmoid(a0_ref[...] + _dot(la.astype(BF16), a2_ref[...]))
    bd = bd_ref[...]
    kk = kb * kkg_ref[...]
    kk = kk * lax.rsqrt(_seg_sum(kk * kk, bd) + 1e-12)
    ka = ka_ref[...]
    a_f, a_b = a[:, :D_HALF], a[:, D_HALF:]
    kd_f = kb * (1.0 + (a_f - 1.0) * ka)
    kd_b = kb * (1.0 + (a_b - 1.0) * ka)
    _put(wf_ref, w[:, :D_HALF])
    _put(wb_ref, w[:, D_HALF:])
    _put(kdf_ref, kd_f)
    _put(kdb_ref, kd_b)
    _put(kaf_ref, kk * a_f)
    _put(kab_ref, kk * a_b)
    _put(nkk_ref, -kk)
    _put(r_ref, rb)
    _put(v_ref, vb)
    _put(bonus_ref, _seg_sum(rb * rk_ref[...] * (kd_f + kd_b), bd) * vb)


def _block_diag2(m0, m1):
    z = jnp.zeros_like(m0)
    return jnp.concatenate([jnp.concatenate([m0, z], axis=1), jnp.concatenate([z, m1], axis=1)], axis=0)


def _rwkv_prep(zb, shift_mu, w0, w2, a0, a2, k_k, k_a, r_k):
    B, T, _ = zb.shape
    n_t = T // ROW_TILE
    per_tile = ROW_TILE // SUBLANES
    bd = _block_diag_ones(D_HALF, HS_B)
    row = lambda x: x.reshape(1, -1)
    vec = pl.BlockSpec((1, D_HALF), lambda b, i: (0, 0))
    vec2 = pl.BlockSpec((1, 2 * D_HALF), lambda b, i: (0, 0))
    out = pl.BlockSpec((1, ROW_TILE, D_HALF), lambda b, i: (b, i, 0))
    return pl.pallas_call(
        functools.partial(_rwkv_prep_kernel, n_t),
        grid=(B, n_t),
        in_specs=[
            pl.BlockSpec((1, ROW_TILE, B_SHIFT), lambda b, i: (b, i, 0)),
            pl.BlockSpec((1, SUBLANES, B_SHIFT), lambda b, i: (b, jnp.maximum(i * per_tile - 1, 0), 0)),
            pl.BlockSpec((1, SUBLANES, B_SHIFT),
                         lambda b, i: (b, jnp.minimum((i + 1) * per_tile, n_t * per_tile - 1), 0)),
            pl.BlockSpec((1, B_SHIFT), lambda b, i: (0, 0)),
            vec2, vec2,
            pl.BlockSpec((2 * W_LORA, 2 * D_HALF), lambda b, i: (0, 0)),
            pl.BlockSpec((2 * A_LORA, 2 * D_HALF), lambda b, i: (0, 0)),
            vec, vec, vec,
            pl.BlockSpec(bd.shape, lambda b, i: (0, 0)),
        ],
        out_specs=[out] * 10,
        out_shape=[jax.ShapeDtypeStruct((B, T, D_HALF), F32)] * 10,
        compiler_params=_params("parallel", "parallel"),
    )(zb, zb, zb, row(shift_mu), row(w0), row(a0),
      _block_diag2(w2[0], w2[1]).astype(BF16), _block_diag2(a2[0], a2[1]).astype(BF16),
      row(k_k), row(k_a), row(r_k), bd)


CHAIN_ROWS = 32
HEAD_PAIRS = H_B // 2


def _step_rows(ref, t, vs):
    if vs == 1:
        return ref[:, t, :]
    return jnp.concatenate([jnp.broadcast_to(ref[b, t:t + 1, :], (vs, ref.shape[2])) for b in range(ref.shape[0])],
                           axis=0)


def _to_chains(x):
    return jnp.concatenate([x[:, j * LANES:(j + 1) * LANES] for j in range(HEAD_PAIRS)], axis=0).T


def _rwkv_scan_kernel(reverse, vs, w_ref, nkk_ref, kka_ref, kd_ref, r_ref, v_ref, s0_ref, y_ref, s_ref,
                      vbuf, ybuf):
    vh = HS_B // vs

    @pl.when(pl.program_id(0) == 0)
    def _():
        s_ref[...] = s0_ref[...]

    split = lax.broadcasted_iota(jnp.int32, (vh, LANES), 1) % vs
    steps = range(SCAN_T - 1, -1, -1) if reverse else range(SCAN_T)
    for t in steps:
        w, nkk, kka, kd, r, v_all = (_to_chains(_step_rows(ref, t, vs))
                                     for ref in (w_ref, nkk_ref, kka_ref, kd_ref, r_ref, v_ref))
        for par in range(2):
            rows = slice(par * HS_B, (par + 1) * HS_B)
            lanes = slice(par * LANES, (par + 1) * LANES)
            v_par = v_all[rows]
            v_own = v_par[:vh]
            for q in range(1, vs):
                v_own = jnp.where(split == q, v_par[q * vh:(q + 1) * vh], v_own)
            vbuf[par] = v_own
            kvec = (w[rows], nkk[rows], kka[rows], kd[rows], r[rows])

            def v_step(g, c, par=par, lanes=lanes, kvec=kvec):
                w_, nkk_, kka_, kd_, r_ = kvec
                base = pl.multiple_of(g * SUBLANES, SUBLANES)
                vblk = vbuf[par, pl.ds(base, SUBLANES), :]
                ys = []
                for u in range(SUBLANES):
                    s = s_ref[base + u, :, lanes]
                    sa = jnp.sum(s * nkk_, axis=0, keepdims=True)
                    s = s * w_ + sa * kka_ + vblk[u:u + 1, :] * kd_
                    s_ref[base + u, :, lanes] = s
                    ys.append(jnp.sum(s * r_, axis=0, keepdims=True))
                ybuf[par, pl.ds(base, SUBLANES), :] = jnp.concatenate(ys, axis=0)
                return c

            lax.fori_loop(0, vh // SUBLANES, v_step, 0)

        halves = []
        for par in range(2):
            y = ybuf[par]
            if vs > 1:
                y = jnp.concatenate([jnp.where(split == q, y, 0.0) for q in range(vs)], axis=0)
            halves.append(y)
        yt = jnp.concatenate(halves, axis=0).T
        for j in range(HEAD_PAIRS):
            blk = yt[j * CHAIN_ROWS:(j + 1) * CHAIN_ROWS]
            if vs > 1:
                blk = jnp.sum(blk.reshape(CHAIN_ROWS // vs, vs, LANES), axis=1)
            y_ref[:, t, j * LANES:(j + 1) * LANES] = blk


def _rwkv_scan(w, nkk, kka, kd, r, v, s0, reverse):
    B, T, W = w.shape
    vs = CHAIN_ROWS // B
    vh = HS_B // vs
    n_t = T // SCAN_T
    tblk = (lambda t: n_t - 1 - t) if reverse else (lambda t: t)
    vec = pl.BlockSpec((B, SCAN_T, W), lambda t: (0, tblk(t), 0))
    state = pl.BlockSpec((vh, HS_B, 2 * LANES), lambda t: (0, 0, 0))
    return pl.pallas_call(
        functools.partial(_rwkv_scan_kernel, reverse, vs),
        grid=(n_t,),
        in_specs=[vec] * 6 + [state],
        out_specs=[vec, state],
        out_shape=[jax.ShapeDtypeStruct((B, T, W), F32), jax.ShapeDtypeStruct((vh, HS_B, 2 * LANES), F32)],
        scratch_shapes=[pltpu.VMEM((2, vh, LANES), F32), pltpu.VMEM((2, vh, LANES), F32)],
        compiler_params=_params("arbitrary"),
    )(w, nkk, kka, kd, r, v, s0)


def _rwkv_post_kernel(yf_ref, yb_ref, bonus_ref, gb_ref, lng_ref, lnb_ref, bd_ref, o_ref):
    bd = bd_ref[...]
    y = _get(yf_ref) + _get(yb_ref)
    inv_n = 1.0 / HS_B
    d = y - _seg_sum(y, bd) * inv_n
    var = _seg_sum(d * d, bd) * inv_n
    yn = d * lax.rsqrt(var + GN_EPS) * lng_ref[...] + lnb_ref[...]
    o_ref[0] = (yn + bonus_ref[0]) * _silu(gb_ref[0])


def _rwkv_post(y_f, y_b, bonus, gb, ln_g, ln_b):
    B, T, _ = y_f.shape
    bd = _block_diag_ones(D_HALF, HS_B)
    blk = pl.BlockSpec((1, ROW_TILE, D_HALF), lambda b, i: (b, i, 0))
    vec = pl.BlockSpec((1, D_HALF), lambda b, i: (0, 0))
    return pl.pallas_call(
        _rwkv_post_kernel,
        grid=(B, T // ROW_TILE),
        in_specs=[blk, blk, blk, blk, vec, vec, pl.BlockSpec(bd.shape, lambda b, i: (0, 0))],
        out_specs=blk,
        out_shape=jax.ShapeDtypeStruct((B, T, D_HALF), F32),
        compiler_params=_params("parallel", "parallel"),
    )(y_f, y_b, bonus, gb, ln_g.reshape(1, -1), ln_b.reshape(1, -1), bd)


def _rwkv_mixer(zb, gb, p, s_init):
    B, T, _ = zb.shape
    (w_f, w_b, kd_f, kd_b, ka_f, ka_b, nkk, r, v, bonus) = _rwkv_prep(
        zb, p["shift_mu"], p["w0"], p["w2"], p["a0"], p["a2"], p["k_k"], p["k_a"], p["r_k"])
    vs = CHAIN_ROWS // B
    vh = HS_B // vs
    chains = H_B * B * vs
    ys, fin = [], []
    for d, (w_d, ka_d, kd_d) in enumerate(((w_f, ka_f, kd_f), (w_b, ka_b, kd_b))):
        if s_init is None:
            s0 = jnp.zeros((vh, HS_B, chains), F32)
        else:
            s0 = s_init[d].reshape(B, HEAD_PAIRS, 2, vs, vh, HS_B).transpose(4, 5, 2, 1, 0, 3)
            s0 = s0.reshape(vh, HS_B, chains)
        y, s_fin = _rwkv_scan(w_d, nkk, ka_d, kd_d, r, v, s0, reverse=(d == 1))
        ys.append(y)
        fin.append(s_fin.reshape(vh, HS_B, 2, HEAD_PAIRS, B, vs).transpose(4, 3, 2, 5, 0, 1)
                   .reshape(B, H_B, HS_B, HS_B))
    return _rwkv_post(ys[0], ys[1], bonus, gb, p["ln_g"], p["ln_b"]), fin


def _chunk_cumsum(x, chunk, suffix):
    T = x.shape[0]
    pos = lax.broadcasted_iota(jnp.int32, (T, 1), 0) % chunk
    step = 1
    while step < chunk:
        if suffix:
            x = x + jnp.where(pos < chunk - step, pltpu.roll(x, T - step, 0), 0.0)
        else:
            x = x + jnp.where(pos >= step, pltpu.roll(x, step, 0), 0.0)
        step *= 2
    return x


def _gla_kernel(T, q_ref, k_ref, v_ref, g_ref, gl_ref, w2_ref, gb_ref, lng_ref, tri_ref, s0f_ref, s0b_ref,
                o_ref, sf_ref, sb_ref, b_ref, acc_ref, qb_ref, dec_ref, u_ref, sst_ref, st_ref):
    C = GLA_CHUNK
    n_c = T // C
    pre = _dot(gl_ref[0].astype(BF16), w2_ref[0]) + gb_ref[0]
    la = jax.nn.log_sigmoid(pre) * (1.0 / GLA_TAU)
    b_ref[:, :DK_C] = _chunk_cumsum(la[:, :DK_C], C, suffix=False)
    b_ref[:, DK_C:] = _chunk_cumsum(la[:, DK_C:], C, suffix=True)
    qscale = DK_C ** -0.5
    contract_last = (((1,), (1,)), ((), ()))
    contract_first = (((0,), (0,)), ((), ()))
    vis_f = tri_ref[0]
    vis_b = tri_ref[1]

    def chunk_rows(c):
        return pl.ds(pl.multiple_of(c * C, C), C)

    def dec_rows(c):
        return pl.ds(pl.multiple_of(c * SUBLANES, SUBLANES), SUBLANES)

    def intra(c, carry):
        rows = chunk_rows(c)
        q = q_ref[0, rows, :] * qscale
        k = k_ref[0, rows, :]
        vc = v_ref[0, rows, :].astype(BF16)
        b = b_ref[rows, :]
        btot_f = b[C - 1:C, :DK_C]
        btot_b = b[0:1, DK_C:]
        qe, ke, qb, kl = [], [], [], []
        for bd, btot in ((b[:, :DK_C], btot_f), (b[:, DK_C:], btot_b)):
            mref = 0.5 * btot
            e_half = jnp.exp(mref)
            q_up = q * jnp.exp(bd - mref)
            k_dn = k * jnp.exp(mref - bd)
            qe.append(q_up.astype(BF16))
            ke.append(k_dn.astype(BF16))
            qb.append((q_up * e_half).astype(BF16))
            kl.append((k_dn * e_half).astype(BF16))
        sc = lax.dot_general(jnp.concatenate(qe, axis=0), jnp.concatenate(ke, axis=0), contract_last,
                             preferred_element_type=F32)
        att = sc[:C] * vis_f + pltpu.roll(sc[C:], C, 1) * vis_b
        acc_ref[rows, :] = _dot(att[:, :C].astype(BF16), vc)
        u_ref[c] = lax.dot_general(vc, jnp.concatenate(kl, axis=1), contract_first, preferred_element_type=F32)
        qb_ref[rows, :] = jnp.concatenate(qb, axis=1)
        dec_ref[dec_rows(c), :] = jnp.broadcast_to(
            jnp.exp(jnp.concatenate([btot_f, btot_b], axis=1)), (SUBLANES, 2 * DK_C))
        return carry

    lax.fori_loop(0, n_c, intra, 0, unroll=4)

    st_ref[0] = s0f_ref[0, 0, 0]
    st_ref[1] = s0b_ref[0, 0, 0]

    def states(i, carry):
        for d, c in ((0, i), (1, n_c - 1 - i)):
            lanes = slice(d * DK_C, (d + 1) * DK_C)
            st = st_ref[d]
            sst_ref[c, :, lanes] = st.astype(BF16)
            st_ref[d] = st * dec_ref[pl.ds(pl.multiple_of(c * SUBLANES, SUBLANES), 1), lanes] + u_ref[c, :, lanes]
        return carry

    lax.fori_loop(0, n_c, states, 0)
    sf_ref[0, 0] = st_ref[0]
    sb_ref[0, 0] = st_ref[1]

    def inter(c, carry):
        rows = chunk_rows(c)
        acc_ref[rows, :] = acc_ref[rows, :] + lax.dot_general(qb_ref[rows, :], sst_ref[c], contract_last,
                                                              preferred_element_type=F32)
        return carry

    lax.fori_loop(0, n_c, inter, 0, unroll=4)

    o = acc_ref[...]
    o = o * lax.rsqrt(jnp.mean(o * o, axis=-1, keepdims=True) + EPS) * lng_ref[...]
    o_ref[0] = o * _silu(g_ref[0])


def _gla_mixer(q, k, v, g, gl, gw2, gbias, ln_g, s_init, layer_j):
    B, T, _ = q.shape
    C = GLA_CHUNK
    idx = np.arange(C)
    tri = np.zeros((2, C, LANES), np.float32)
    tri[0, :, :C] = idx[:, None] >= idx[None, :]
    tri[1, :, :C] = idx[:, None] <= idx[None, :]
    tri = jnp.asarray(tri)
    w2 = jnp.stack([_block_diag2(gw2[0][:, h * DK_C:(h + 1) * DK_C], gw2[1][:, h * DK_C:(h + 1) * DK_C])
                    for h in range(H_C)]).astype(BF16)
    gb = jnp.stack([jnp.concatenate([gbias[0][h * DK_C:(h + 1) * DK_C], gbias[1][h * DK_C:(h + 1) * DK_C]])
                    for h in range(H_C)]).reshape(H_C, 1, 2 * DK_C)
    if s_init is None:
        zeros = jnp.zeros((1, 1, 1, DV_C, DK_C), F32)
        s0f = s0b = zeros
        s_spec = pl.BlockSpec((1, 1, 1, DV_C, DK_C), lambda b, h: (0, 0, 0, 0, 0))
    else:
        s0f, s0b = (jnp.swapaxes(s, -1, -2) for s in s_init)
        s_spec = pl.BlockSpec((1, 1, 1, DV_C, DK_C), lambda b, h: (b, layer_j, h, 0, 0))
    o, sf, sb = pl.pallas_call(
        functools.partial(_gla_kernel, T),
        grid=(B, H_C),
        in_specs=[
            pl.BlockSpec((1, T, DK_C), lambda b, h: (b, 0, h)),
            pl.BlockSpec((1, T, DK_C), lambda b, h: (b, 0, h)),
            pl.BlockSpec((1, T, DV_C), lambda b, h: (b, 0, h)),
            pl.BlockSpec((1, T, DV_C), lambda b, h: (b, 0, h)),
            pl.BlockSpec((1, T, 2 * G_LORA), lambda b, h: (b, 0, 0)),
            pl.BlockSpec((1, 2 * G_LORA, 2 * DK_C), lambda b, h: (h, 0, 0)),
            pl.BlockSpec((1, 1, 2 * DK_C), lambda b, h: (h, 0, 0)),
            pl.BlockSpec((1, DV_C), lambda b, h: (0, 0)),
            pl.BlockSpec((2, C, LANES), lambda b, h: (0, 0, 0)),
            s_spec, s_spec,
        ],
        out_specs=[
            pl.BlockSpec((1, T, DV_C), lambda b, h: (b, 0, h)),
            pl.BlockSpec((1, 1, DV_C, DK_C), lambda b, h: (b, h, 0, 0)),
            pl.BlockSpec((1, 1, DV_C, DK_C), lambda b, h: (b, h, 0, 0)),
        ],
        out_shape=[
            jax.ShapeDtypeStruct((B, T, H_C * DV_C), F32),
            jax.ShapeDtypeStruct((B, H_C, DV_C, DK_C), F32),
            jax.ShapeDtypeStruct((B, H_C, DV_C, DK_C), F32),
        ],
        scratch_shapes=[
            pltpu.VMEM((T, 2 * DK_C), F32), pltpu.VMEM((T, DV_C), F32),
            pltpu.VMEM((T, 2 * DK_C), BF16), pltpu.VMEM((T // C * SUBLANES, 2 * DK_C), F32),
            pltpu.VMEM((T // C, DV_C, 2 * DK_C), F32), pltpu.VMEM((T // C, DV_C, 2 * DK_C), BF16),
            pltpu.VMEM((2, DV_C, DK_C), F32),
        ],
        compiler_params=_params("parallel", "parallel"),
    )(q, k, v, g, gl, w2, gb, ln_g.reshape(1, -1), tri, s0f, s0b)
    return o, (jnp.swapaxes(sf, -1, -2), jnp.swapaxes(sb, -1, -2))


def kernel(x_prompt, x_sample, c, cache_attn_k, cache_attn_v, state_rwkv_fwd, state_rwkv_bwd, state_gla_fwd, state_gla_bwd, c_ctx, norm_g, mod_w, mod_b, ev_w_in, ev_w_out, ev_qn_g, ev_kn_g, ev_shift_mu, rw_w0, rw_w2, rw_a0, rw_a2, rw_kk, rw_ka, rw_rk, rw_ln_g, rw_ln_b, od_w_in, od_w_out, gla_w2, gla_b, gla_ln_g, final_g):
    n_dec = c.shape[0]
    cond = jnp.concatenate([c_ctx[None], c, jnp.zeros((SUBLANES - 1 - n_dec, D_MODEL), F32)], axis=0)
    mod = _modulation(cond, mod_w, mod_b)

    ev_in = ev_w_in.astype(BF16)
    ev_out = ev_w_out.astype(BF16)
    od_in = od_w_in.astype(BF16)
    od_out = od_w_out.astype(BF16)
    kw = KV_A * HD_A
    ck = cache_attn_k.reshape(cache_attn_k.shape[:3] + (kw,))
    cv = cache_attn_v.reshape(cache_attn_v.shape[:3] + (kw,))
    gla_f_t = state_gla_fwd
    gla_b_t = state_gla_bwd

    def trunk(x, latent):
        new = {"k": [], "v": [], "rf": [], "rb": [], "gf": [], "gb": []}
        for i in range(DEPTH):
            j = i // 2
            fg = final_g if i == DEPTH - 1 else None
            if i % 2 == 0:
                qa, ka, va, ga, zb, gb = _inproj(x, norm_g[i], mod, i, ev_in[j], EV_SPLITS, latent)
                if latent:
                    o_a, _ = _attention(qa, ka, va, ga, ev_qn_g[j], ev_kn_g[j], ck, cv, j)
                    s_init = (state_rwkv_fwd[:, j], state_rwkv_bwd[:, j])
                else:
                    o_a, kn = _attention(qa, ka, va, ga, ev_qn_g[j], ev_kn_g[j])
                    s_init = None
                    new["k"].append(kn)
                    new["v"].append(va)
                p = dict(shift_mu=ev_shift_mu[j], w0=rw_w0[j], w2=rw_w2[j], a0=rw_a0[j], a2=rw_a2[j],
                         k_k=rw_kk[j], k_a=rw_ka[j], r_k=rw_rk[j], ln_g=rw_ln_g[j], ln_b=rw_ln_b[j])
                o_b, fin = _rwkv_mixer(zb, gb, p, s_init)
                new["rf"].append(fin[0])
                new["rb"].append(fin[1])
                x = _outproj([o_a, o_b], [ev_out[j, :D_HALF], ev_out[j, D_HALF:]], x, mod, i, latent, fg)
            else:
                q, k, v, g, gl = _inproj(x, norm_g[i], mod, i, od_in[j], OD_SPLITS, latent)
                s_init = (gla_f_t, gla_b_t) if latent else None
                o, fin = _gla_mixer(q, k, v, g, gl, gla_w2[j], gla_b[j], gla_ln_g[j], s_init, j)
                new["gf"].append(fin[0])
                new["gb"].append(fin[1])
                x = _outproj([o], [od_out[j]], x, mod, i, latent, fg)
        return x, new

    y_prompt, new = trunk(x_prompt, False)
    y_sample, _ = trunk(x_sample, True)
    B, T = x_prompt.shape[:2]
    heads = lambda t: t.reshape(B, T, KV_A, HD_A)
    return (y_prompt, y_sample,
            jnp.stack([heads(t) for t in new["k"]], axis=1), jnp.stack([heads(t) for t in new["v"]], axis=1),
            jnp.stack(new["rf"], axis=1), jnp.stack(new["rb"], axis=1),
            jnp.stack(new["gf"], axis=1), jnp.stack(new["gb"], axis=1))
```

```python
import functools

import numpy as np
import jax
import jax.numpy as jnp
from jax import lax
from jax.experimental import pallas as pl
from jax.experimental.pallas import tpu as pltpu

F32 = jnp.float32
BF16 = jnp.bfloat16

D_MODEL = 1024
DEPTH = 4
GRID_W = 64
D_HALF = D_MODEL // 2
HD_A = 64
H_A = D_HALF // HD_A
KV_A = H_A // 4
ROPE_BASE = 10000.0
HS_B = 64
H_B = D_HALF // HS_B
W_LORA = 64
A_LORA = 64
RWKV_DECAY_SCALE = 0.606531
GN_EPS = 64e-5
B_SHIFT = 3 * D_HALF + 2 * W_LORA + 2 * A_LORA
H_C = 4
DK_C = D_MODEL // 2 // H_C
DV_C = D_MODEL // H_C
G_LORA = 16
GLA_TAU = 16.0
EPS = 1e-6

EV_SPLITS = (H_A * HD_A, KV_A * HD_A, KV_A * HD_A, D_HALF, B_SHIFT, D_HALF)
OD_SPLITS = (H_C * DK_C, H_C * DK_C, D_MODEL, D_MODEL, 2 * G_LORA)

LANES = 128
SUBLANES = 8
VMEM_LIMIT_BYTES = 56 * 1024 * 1024

ROW_TILE = 256
Q_TILE = 256
GLA_CHUNK = 64
SCAN_T = 8


def _params(*sem):
    return pltpu.CompilerParams(dimension_semantics=sem, vmem_limit_bytes=VMEM_LIMIT_BYTES)


def _silu(x):
    return x * jax.nn.sigmoid(x)


def _dot(a, b):
    return jnp.dot(a, b, preferred_element_type=F32)


def _seg_sum(x, ones_bd):
    hi = x.astype(BF16)
    r1 = x - hi.astype(F32)
    mid = r1.astype(BF16)
    lo = (r1 - mid.astype(F32)).astype(BF16)
    return _dot(hi, ones_bd) + _dot(mid, ones_bd) + _dot(lo, ones_bd)


def _get(ref):
    return ref[...].reshape(ref.shape[-2:])


def _put(ref, val):
    ref[...] = val.reshape(ref.shape)


def _block_diag_ones(n, blk):
    i = np.arange(n) // blk
    return jnp.asarray((i[:, None] == i[None, :]).astype(np.float32), dtype=BF16)


def _mod_kernel(cond_ref, w_ref, b_ref, o_ref):
    s = _silu(cond_ref[...])
    o_ref[0] = _dot(s.astype(BF16), w_ref[0].astype(BF16)) + b_ref[0]


def _modulation(cond, mod_w, mod_b):
    n = cond.shape[0]
    return pl.pallas_call(
        _mod_kernel,
        grid=(DEPTH, 3),
        in_specs=[
            pl.BlockSpec((n, D_MODEL), lambda i, j: (0, 0)),
            pl.BlockSpec((1, D_MODEL, D_MODEL), lambda i, j: (i, 0, j)),
            pl.BlockSpec((1, 1, D_MODEL), lambda i, j: (i, 0, j)),
        ],
        out_specs=pl.BlockSpec((1, n, D_MODEL), lambda i, j: (i, 0, j)),
        out_shape=jax.ShapeDtypeStruct((DEPTH, n, 3 * D_MODEL), F32),
        compiler_params=_params("parallel", "parallel"),
    )(cond, mod_w, mod_b.reshape(DEPTH, 1, 3 * D_MODEL))


def _mod_row(latent):
    return (1 + pl.program_id(0)) if latent else 0


def _inproj_kernel(latent, splits, x_ref, g_ref, sh_ref, sc_ref, w_ref, *out_refs):
    r = _mod_row(latent)
    x = x_ref[0]
    y = x * lax.rsqrt(jnp.mean(x * x, axis=-1, keepdims=True) + EPS) * g_ref[...]
    shift = sh_ref[0, pl.ds(r, 1), :]
    scale = sc_ref[0, pl.ds(r, 1), :]
    h = (y * (1.0 + scale) + shift).astype(BF16)
    off = 0
    for o_ref, n in zip(out_refs, splits):
        o_ref[0] = _dot(h, w_ref[:, off:off + n])
        off += n


def _inproj(x, norm_g, mod, layer, w_bf16, splits, latent):
    B, T, _ = x.shape
    cols = w_bf16.shape[1]
    nrow = mod.shape[1]
    return pl.pallas_call(
        functools.partial(_inproj_kernel, latent, splits),
        grid=(B, T // ROW_TILE),
        in_specs=[
            pl.BlockSpec((1, ROW_TILE, D_MODEL), lambda b, i: (b, i, 0)),
            pl.BlockSpec((1, D_MODEL), lambda b, i: (0, 0)),
            pl.BlockSpec((1, nrow, D_MODEL), lambda b, i: (layer, 0, 0)),
            pl.BlockSpec((1, nrow, D_MODEL), lambda b, i: (layer, 0, 1)),
            pl.BlockSpec((D_MODEL, cols), lambda b, i: (0, 0)),
        ],
        out_specs=[pl.BlockSpec((1, ROW_TILE, n), lambda b, i: (b, i, 0)) for n in splits],
        out_shape=[jax.ShapeDtypeStruct((B, T, n), F32) for n in splits],
        compiler_params=_params("parallel", "parallel"),
    )(x, norm_g.reshape(1, D_MODEL), mod, mod, w_bf16)


def _outproj_kernel(latent, final, n_in, *refs):
    o_refs = refs[:n_in]
    w_refs = refs[n_in:2 * n_in]
    x_ref, gate_ref = refs[2 * n_in], refs[2 * n_in + 1]
    rest = refs[2 * n_in + 2:]
    r = _mod_row(latent)
    acc = _dot(o_refs[0][0].astype(BF16), w_refs[0][...])
    for o_ref, w_ref in zip(o_refs[1:], w_refs[1:]):
        acc = acc + _dot(o_ref[0].astype(BF16), w_ref[...])
    y = x_ref[0] + gate_ref[0, pl.ds(r, 1), :] * acc
    if final:
        fg_ref, out_ref = rest
        y = y * lax.rsqrt(jnp.mean(y * y, axis=-1, keepdims=True) + EPS) * fg_ref[...]
    else:
        (out_ref,) = rest
    out_ref[0] = y


def _outproj(outs, ws_bf16, x, mod, layer, latent, final_g=None):
    B, T, _ = x.shape
    nrow = mod.shape[1]
    n_in = len(outs)
    final = final_g is not None
    in_specs = [pl.BlockSpec((1, ROW_TILE, o.shape[-1]), lambda b, i: (b, i, 0)) for o in outs]
    in_specs += [pl.BlockSpec(w.shape, lambda b, i: (0, 0)) for w in ws_bf16]
    in_specs += [
        pl.BlockSpec((1, ROW_TILE, D_MODEL), lambda b, i: (b, i, 0)),
        pl.BlockSpec((1, nrow, D_MODEL), lambda b, i: (layer, 0, 2)),
    ]
    args = list(outs) + list(ws_bf16) + [x, mod]
    if final:
        in_specs.append(pl.BlockSpec((1, D_MODEL), lambda b, i: (0, 0)))
        args.append(final_g.reshape(1, D_MODEL))
    return pl.pallas_call(
        functools.partial(_outproj_kernel, latent, final, n_in),
        grid=(B, T // ROW_TILE),
        in_specs=in_specs,
        out_specs=pl.BlockSpec((1, ROW_TILE, D_MODEL), lambda b, i: (b, i, 0)),
        out_shape=jax.ShapeDtypeStruct((B, T, D_MODEL), F32),
        compiler_params=_params("parallel", "parallel"),
    )(*args)


def _rope_tables(T):
    n_rows = T // GRID_W
    row = jnp.repeat(jnp.arange(n_rows), GRID_W).astype(F32)
    col = jnp.tile(jnp.arange(GRID_W), n_rows).astype(F32)
    n_freq = HD_A // 4
    inv = ROPE_BASE ** (-jnp.arange(n_freq, dtype=F32) / n_freq)
    ang_r = row[:, None] * inv
    ang_c = col[:, None] * inv
    zero = jnp.zeros_like(ang_r)
    cos = jnp.concatenate([jnp.cos(ang_r), jnp.cos(ang_r), jnp.cos(ang_c), jnp.cos(ang_c)], axis=1)
    s1 = jnp.concatenate([-jnp.sin(ang_r), zero, -jnp.sin(ang_c), zero], axis=1)
    s2 = jnp.concatenate([zero, jnp.sin(ang_r), zero, jnp.sin(ang_c)], axis=1)
    return cos, s1, s2


def _rope(x, cos, s1, s2):
    n = x.shape[-1]
    q = HD_A // 4
    return x * cos + pltpu.roll(x, n - q, 1) * s1 + pltpu.roll(x, q, 1) * s2


def _attn_kernel(latent, S, *refs):
    if latent:
        (q_ref, k_ref, v_ref, ga_ref, qg_ref, kg_ref, bd_ref, cos_ref, s1_ref, s2_ref,
         cosk_ref, s1k_ref, s2k_ref, ck_ref, cv_ref, o_ref, km_ref, vm_ref) = refs
    else:
        (q_ref, k_ref, v_ref, ga_ref, qg_ref, kg_ref, bd_ref, o_ref, kn_ref, km_ref, vm_ref) = refs
    bd = bd_ref[...]
    inv_d = 1.0 / HD_A
    kw = KV_A * HD_A

    @pl.when(pl.program_id(1) == 0)
    def _():
        k = k_ref[0]
        kn = k * lax.rsqrt(_seg_sum(k * k, bd[:kw, :kw]) * inv_d + EPS) * kg_ref[...]
        v = v_ref[0]
        if latent:
            kn = _rope(kn, cosk_ref[...], s1k_ref[...], s2k_ref[...])
            k_all = jnp.concatenate([ck_ref[0, 0], kn], axis=0)
            v_all = jnp.concatenate([cv_ref[0, 0], v], axis=0)
        else:
            kn_ref[0] = kn
            k_all, v_all = kn, v
        lane = lax.broadcasted_iota(jnp.int32, (S, kw), 1)
        k_sw = pltpu.roll(k_all, HD_A, 1)
        v_sw = pltpu.roll(v_all, HD_A, 1)
        for j in range(KV_A):
            for half in range(2):
                keep = (lane < HD_A) if half == 0 else (lane >= HD_A)
                src_k, src_v = (k_all, v_all) if j == half else (k_sw, v_sw)
                km_ref[2 * j + half] = jnp.where(keep, src_k, 0.0).astype(BF16)
                vm_ref[2 * j + half] = jnp.where(keep, src_v, 0.0).astype(BF16)

    q = q_ref[0]
    qn = q * lax.rsqrt(_seg_sum(q * q, bd) * inv_d + EPS) * qg_ref[...]
    if latent:
        qn = _rope(qn, cos_ref[...], s1_ref[...], s2_ref[...])
    qb = (qn * HD_A ** -0.5).astype(BF16)
    for m in range(H_A // 2):
        blk = slice(m * LANES, (m + 1) * LANES)
        qs = qb[:, blk]
        acc = None
        for half in range(2):
            j = (2 * m + half) // (H_A // KV_A)
            s = lax.dot_general(qs, km_ref[2 * j + half], (((1,), (1,)), ((), ())),
                                preferred_element_type=F32)
            e = jnp.exp(s - jnp.max(s, axis=-1, keepdims=True))
            l = jnp.sum(e, axis=-1, keepdims=True)
            pv = _dot(e.astype(BF16), vm_ref[2 * j + half]) * (1.0 / l)
            acc = pv if acc is None else acc + pv
        o_ref[0, :, blk] = acc * _silu(ga_ref[0, :, blk])


def _attention(qa, ka, va, ga, qn_g, kn_g, ctx_k=None, ctx_v=None, layer_j=0):
    B, T, _ = qa.shape
    latent = ctx_k is not None
    S = T + (ctx_k.shape[2] if latent else 0)
    kw = KV_A * HD_A
    qw = H_A * HD_A
    bd = _block_diag_ones(qw, HD_A)
    qblk = pl.BlockSpec((1, Q_TILE, qw), lambda b, i: (b, i, 0))
    kblk = pl.BlockSpec((1, T, kw), lambda b, i: (b, 0, 0))
    in_specs = [qblk, kblk, kblk, qblk,
                pl.BlockSpec((1, qw), lambda b, i: (0, 0)),
                pl.BlockSpec((1, kw), lambda b, i: (0, 0)),
                pl.BlockSpec(bd.shape, lambda b, i: (0, 0))]
    args = [qa, ka, va, ga, jnp.tile(qn_g, H_A).reshape(1, -1), jnp.tile(kn_g, KV_A).reshape(1, -1), bd]
    out_specs = [qblk]
    out_shape = [jax.ShapeDtypeStruct((B, T, qw), F32)]
    if latent:
        tabs = _rope_tables(T)
        P = ctx_k.shape[2]
        in_specs += [pl.BlockSpec((Q_TILE, qw), lambda b, i: (i, 0))] * 3
        in_specs += [pl.BlockSpec((T, kw), lambda b, i: (0, 0))] * 3
        in_specs += [pl.BlockSpec((1, 1, P, kw), lambda b, i: (b, layer_j, 0, 0))] * 2
        args += [jnp.tile(t, (1, H_A)) for t in tabs] + [jnp.tile(t, (1, KV_A)) for t in tabs] + [ctx_k, ctx_v]
    else:
        out_specs.append(kblk)
        out_shape.append(jax.ShapeDtypeStruct((B, T, kw), F32))
    res = pl.pallas_call(
        functools.partial(_attn_kernel, latent, S),
        grid=(B, T // Q_TILE),
        in_specs=in_specs,
        out_specs=out_specs,
        out_shape=out_shape,
        scratch_shapes=[pltpu.VMEM((2 * KV_A, S, kw), BF16), pltpu.VMEM((2 * KV_A, S, kw), BF16)],
        compiler_params=_params("parallel", "arbitrary"),
    )(*args)
    return res if not latent else (res[0], None)


def _rwkv_prep_kernel(n_t, z_ref, zp_ref, zn_ref, mu_ref, w0_ref, a0_ref, w2_ref, a2_ref,
                      kkg_ref, ka_ref, rk_ref, bd_ref,
                      wf_ref, wb_ref, kdf_ref, kdb_ref, kaf_ref, kab_ref, nkk_ref, r_ref, v_ref, bonus_ref):
    i = pl.program_id(1)
    z = z_ref[0]
    n = z.shape[0]
    prev_row = jnp.where(i > 0, zp_ref[0, SUBLANES - 1:SUBLANES, :], 0.0)
    next_row = jnp.where(i < n_t - 1, zn_ref[0, 0:1, :], 0.0)
    rows = lax.broadcasted_iota(jnp.int32, (n, 1), 0)
    zp = jnp.where(rows == 0, prev_row, pltpu.roll(z, 1, 0))
    zn = jnp.where(rows == n - 1, next_row, pltpu.roll(z, n - 1, 0))
    zs = z + mu_ref[...] * (0.5 * (zp + zn) - z)

    rb = zs[:, 0:D_HALF]
    kb = zs[:, D_HALF:2 * D_HALF]
    vb = zs[:, 2 * D_HALF:3 * D_HALF]
    lw = zs[:, 3 * D_HALF:3 * D_HALF + 2 * W_LORA]
    la = zs[:, 3 * D_HALF + 2 * W_LORA:]
    w = jnp.exp(-RWKV_DECAY_SCALE * jax.nn.sigmoid(w0_ref[...] + _dot(jnp.tanh(lw).astype(BF16), w2_ref[...])))
    a = jax.nn.sigmoid(a0_ref[...] + _dot(la.astype(BF16), a2_ref[...]))
    bd = bd_ref[...]
    kk = kb * kkg_ref[...]
    kk = kk * lax.rsqrt(_seg_sum(kk * kk, bd) + 1e-12)
    ka = ka_ref[...]
    a_f, a_b = a[:, :D_HALF], a[:, D_HALF:]
    kd_f = kb * (1.0 + (a_f - 1.0) * ka)
    kd_b = kb * (1.0 + (a_b - 1.0) * ka)
    _put(wf_ref, w[:, :D_HALF])
    _put(wb_ref, w[:, D_HALF:])
    _put(kdf_ref, kd_f)
    _put(kdb_ref, kd_b)
    _put(kaf_ref, kk * a_f)
    _put(kab_ref, kk * a_b)
    _put(nkk_ref, -kk)
    _put(r_ref, rb)
    _put(v_ref, vb)
    _put(bonus_ref, _seg_sum(rb * rk_ref[...] * (kd_f + kd_b), bd) * vb)


def _block_diag2(m0, m1):
    z = jnp.zeros_like(m0)
    return jnp.concatenate([jnp.concatenate([m0, z], axis=1), jnp.concatenate([z, m1], axis=1)], axis=0)


def _rwkv_prep(zb, shift_mu, w0, w2, a0, a2, k_k, k_a, r_k):
    B, T, _ = zb.shape
    n_t = T // ROW_TILE
    per_tile = ROW_TILE // SUBLANES
    bd = _block_diag_ones(D_HALF, HS_B)
    row = lambda x: x.reshape(1, -1)
    vec = pl.BlockSpec((1, D_HALF), lambda b, i: (0, 0))
    vec2 = pl.BlockSpec((1, 2 * D_HALF), lambda b, i: (0, 0))
    out = pl.BlockSpec((1, ROW_TILE, D_HALF), lambda b, i: (b, i, 0))
    return pl.pallas_call(
        functools.partial(_rwkv_prep_kernel, n_t),
        grid=(B, n_t),
        in_specs=[
            pl.BlockSpec((1, ROW_TILE, B_SHIFT), lambda b, i: (b, i, 0)),
            pl.BlockSpec((1, SUBLANES, B_SHIFT), lambda b, i: (b, jnp.maximum(i * per_tile - 1, 0), 0)),
            pl.BlockSpec((1, SUBLANES, B_SHIFT),
                         lambda b, i: (b, jnp.minimum((i + 1) * per_tile, n_t * per_tile - 1), 0)),
            pl.BlockSpec((1, B_SHIFT), lambda b, i: (0, 0)),
            vec2, vec2,
            pl.BlockSpec((2 * W_LORA, 2 * D_HALF), lambda b, i: (0, 0)),
            pl.BlockSpec((2 * A_LORA, 2 * D_HALF), lambda b, i: (0, 0)),
            vec, vec, vec,
            pl.BlockSpec(bd.shape, lambda b, i: (0, 0)),
        ],
        out_specs=[out] * 10,
        out_shape=[jax.ShapeDtypeStruct((B, T, D_HALF), F32)] * 10,
        compiler_params=_params("parallel", "parallel"),
    )(zb, zb, zb, row(shift_mu), row(w0), row(a0),
      _block_diag2(w2[0], w2[1]).astype(BF16), _block_diag2(a2[0], a2[1]).astype(BF16),
      row(k_k), row(k_a), row(r_k), bd)


CHAIN_ROWS = 32
HEAD_PAIRS = H_B // 2


def _step_rows(ref, t, vs):
    if vs == 1:
        return ref[:, t, :]
    return jnp.concatenate([jnp.broadcast_to(ref[b, t:t + 1, :], (vs, ref.shape[2])) for b in range(ref.shape[0])],
                           axis=0)


def _to_chains(x):
    return jnp.concatenate([x[:, j * LANES:(j + 1) * LANES] for j in range(HEAD_PAIRS)], axis=0).T


def _rwkv_scan_kernel(reverse, vs, w_ref, nkk_ref, kka_ref, kd_ref, r_ref, v_ref, s0_ref, y_ref, s_ref,
                      vbuf, ybuf, *kbuf):
    vh = HS_B // vs

    @pl.when(pl.program_id(0) == 0)
    def _():
        s_ref[...] = s0_ref[...]

    split = lax.broadcasted_iota(jnp.int32, (vh, LANES), 1) % vs
    steps = range(SCAN_T - 1, -1, -1) if reverse else range(SCAN_T)

    def relayout(t):
        return tuple(_to_chains(_step_rows(ref, t, vs)) for ref in (w_ref, nkk_ref, kka_ref, kd_ref, r_ref, v_ref))

    def emit_y(t):
        halves = []
        for par in range(2):
            y = ybuf[t if kbuf else 0, par]
            if vs > 1:
                y = jnp.concatenate([jnp.where(split == q, y, 0.0) for q in range(vs)], axis=0)
            halves.append(y)
        yt = jnp.concatenate(halves, axis=0).T
        for j in range(HEAD_PAIRS):
            blk = yt[j * CHAIN_ROWS:(j + 1) * CHAIN_ROWS]
            if vs > 1:
                blk = jnp.sum(blk.reshape(CHAIN_ROWS // vs, vs, LANES), axis=1)
            y_ref[:, t, j * LANES:(j + 1) * LANES] = blk

    if kbuf:
        for t in steps:
            for a, x in enumerate(relayout(t)):
                kbuf[0][a, t] = x

    for t in steps:
        w, nkk, kka, kd, r, v_all = (kbuf[0][a, t] for a in range(6)) if kbuf else relayout(t)
        yslot = t if kbuf else 0
        for par in range(2):
            rows = slice(par * HS_B, (par + 1) * HS_B)
            lanes = slice(par * LANES, (par + 1) * LANES)
            v_par = v_all[rows]
            v_own = v_par[:vh]
            for q in range(1, vs):
                v_own = jnp.where(split == q, v_par[q * vh:(q + 1) * vh], v_own)
            vbuf[par] = v_own
            kvec = (w[rows], nkk[rows], kka[rows], kd[rows], r[rows])

            def v_step(g, c, par=par, lanes=lanes, kvec=kvec, yslot=yslot):
                w_, nkk_, kka_, kd_, r_ = kvec
                base = pl.multiple_of(g * SUBLANES, SUBLANES)
                vblk = vbuf[par, pl.ds(base, SUBLANES), :]
                ys = []
                for u in range(SUBLANES):
                    s = s_ref[base + u, :, lanes]
                    sa = jnp.sum(s * nkk_, axis=0, keepdims=True)
                    s = s * w_ + sa * kka_ + vblk[u:u + 1, :] * kd_
                    s_ref[base + u, :, lanes] = s
                    ys.append(jnp.sum(s * r_, axis=0, keepdims=True))
                ybuf[yslot, par, pl.ds(base, SUBLANES), :] = jnp.concatenate(ys, axis=0)
                return c

            lax.fori_loop(0, vh // SUBLANES, v_step, 0)
        if not kbuf:
            emit_y(t)

    if kbuf:
        for t in steps:
            emit_y(t)


def _rwkv_scan(w, nkk, kka, kd, r, v, s0, reverse):
    B, T, W = w.shape
    vs = CHAIN_ROWS // B
    vh = HS_B // vs
    n_t = T // SCAN_T
    hoist = vh > SUBLANES
    tblk = (lambda t: n_t - 1 - t) if reverse else (lambda t: t)
    vec = pl.BlockSpec((B, SCAN_T, W), lambda t: (0, tblk(t), 0))
    state = pl.BlockSpec((vh, HS_B, 2 * LANES), lambda t: (0, 0, 0))
    return pl.pallas_call(
        functools.partial(_rwkv_scan_kernel, reverse, vs),
        grid=(n_t,),
        in_specs=[vec] * 6 + [state],
        out_specs=[vec, state],
        out_shape=[jax.ShapeDtypeStruct((B, T, W), F32), jax.ShapeDtypeStruct((vh, HS_B, 2 * LANES), F32)],
        scratch_shapes=[pltpu.VMEM((2, vh, LANES), F32), pltpu.VMEM((SCAN_T if hoist else 1, 2, vh, LANES), F32)]
        + ([pltpu.VMEM((6, SCAN_T, LANES, LANES), F32)] if hoist else []),
        compiler_params=_params("arbitrary"),
    )(w, nkk, kka, kd, r, v, s0)


def _rwkv_post_kernel(yf_ref, yb_ref, bonus_ref, gb_ref, lng_ref, lnb_ref, bd_ref, o_ref):
    bd = bd_ref[...]
    y = _get(yf_ref) + _get(yb_ref)
    inv_n = 1.0 / HS_B
    d = y - _seg_sum(y, bd) * inv_n
    var = _seg_sum(d * d, bd) * inv_n
    yn = d * lax.rsqrt(var + GN_EPS) * lng_ref[...] + lnb_ref[...]
    o_ref[0] = (yn + bonus_ref[0]) * _silu(gb_ref[0])


def _rwkv_post(y_f, y_b, bonus, gb, ln_g, ln_b):
    B, T, _ = y_f.shape
    bd = _block_diag_ones(D_HALF, HS_B)
    blk = pl.BlockSpec((1, ROW_TILE, D_HALF), lambda b, i: (b, i, 0))
    vec = pl.BlockSpec((1, D_HALF), lambda b, i: (0, 0))
    return pl.pallas_call(
        _rwkv_post_kernel,
        grid=(B, T // ROW_TILE),
        in_specs=[blk, blk, blk, blk, vec, vec, pl.BlockSpec(bd.shape, lambda b, i: (0, 0))],
        out_specs=blk,
        out_shape=jax.ShapeDtypeStruct((B, T, D_HALF), F32),
        compiler_params=_params("parallel", "parallel"),
    )(y_f, y_b, bonus, gb, ln_g.reshape(1, -1), ln_b.reshape(1, -1), bd)


def _rwkv_mixer(zb, gb, p, s_init):
    B, T, _ = zb.shape
    (w_f, w_b, kd_f, kd_b, ka_f, ka_b, nkk, r, v, bonus) = _rwkv_prep(
        zb, p["shift_mu"], p["w0"], p["w2"], p["a0"], p["a2"], p["k_k"], p["k_a"], p["r_k"])
    vs = CHAIN_ROWS // B
    vh = HS_B // vs
    chains = H_B * B * vs
    ys, fin = [], []
    for d, (w_d, ka_d, kd_d) in enumerate(((w_f, ka_f, kd_f), (w_b, ka_b, kd_b))):
        if s_init is None:
            s0 = jnp.zeros((vh, HS_B, chains), F32)
        else:
            s0 = s_init[d].reshape(B, HEAD_PAIRS, 2, vs, vh, HS_B).transpose(4, 5, 2, 1, 0, 3)
            s0 = s0.reshape(vh, HS_B, chains)
        y, s_fin = _rwkv_scan(w_d, nkk, ka_d, kd_d, r, v, s0, reverse=(d == 1))
        ys.append(y)
        fin.append(s_fin.reshape(vh, HS_B, 2, HEAD_PAIRS, B, vs).transpose(4, 3, 2, 5, 0, 1)
                   .reshape(B, H_B, HS_B, HS_B))
    return _rwkv_post(ys[0], ys[1], bonus, gb, p["ln_g"], p["ln_b"]), fin


def _chunk_cumsum(x, chunk, suffix):
    T = x.shape[0]
    pos = lax.broadcasted_iota(jnp.int32, (T, 1), 0) % chunk
    step = 1
    while step < chunk:
        if suffix:
            x = x + jnp.where(pos < chunk - step, pltpu.roll(x, T - step, 0), 0.0)
        else:
            x = x + jnp.where(pos >= step, pltpu.roll(x, step, 0), 0.0)
        step *= 2
    return x


def _loop(n, body, static):
    if static:
        for i in range(n):
            body(i)
    else:
        lax.fori_loop(0, n, lambda i, c: (body(i), c)[1], 0, unroll=4)


def _gla_kernel(T, hp, q_ref, k_ref, v_ref, g_ref, gl_ref, w2_ref, gb_ref, lng_ref, tri_ref, s0f_ref, s0b_ref,
                o_ref, sf_ref, sb_ref, b_ref, acc_ref, qb_ref, dec_ref, u_ref, sst_ref, st_ref):
    C = GLA_CHUNK
    n_c = T // C
    static = n_c <= 4
    qscale = DK_C ** -0.5
    contract_last = (((1,), (1,)), ((), ()))
    contract_first = (((0,), (0,)), ((), ()))
    vis_f = tri_ref[0]
    vis_b = tri_ref[1]
    heads = range(hp)
    klanes = lambda h: slice(h * DK_C, (h + 1) * DK_C)
    vlanes = lambda h: slice(h * DV_C, (h + 1) * DV_C)

    for h in heads:
        pre = _dot(gl_ref[0].astype(BF16), w2_ref[h]) + gb_ref[h]
        la = jax.nn.log_sigmoid(pre) * (1.0 / GLA_TAU)
        b_ref[h, :, :DK_C] = _chunk_cumsum(la[:, :DK_C], C, suffix=False)
        b_ref[h, :, DK_C:] = _chunk_cumsum(la[:, DK_C:], C, suffix=True)
        st_ref[h, 0] = s0f_ref[0, 0, h]
        st_ref[h, 1] = s0b_ref[0, 0, h]

    def chunk_rows(c):
        return pl.ds(c * C, C) if static else pl.ds(pl.multiple_of(c * C, C), C)

    def dec_rows(c, n=SUBLANES):
        return pl.ds(c * SUBLANES, n) if static else pl.ds(pl.multiple_of(c * SUBLANES, SUBLANES), n)

    def intra(c):
        rows = chunk_rows(c)
        for h in heads:
            q = q_ref[0, rows, klanes(h)] * qscale
            k = k_ref[0, rows, klanes(h)]
            vc = v_ref[0, rows, vlanes(h)].astype(BF16)
            b = b_ref[h, rows, :]
            btot_f = b[C - 1:C, :DK_C]
            btot_b = b[0:1, DK_C:]
            qe, ke, qb, kl = [], [], [], []
            for bd, btot in ((b[:, :DK_C], btot_f), (b[:, DK_C:], btot_b)):
                mref = 0.5 * btot
                e_half = jnp.exp(mref)
                q_up = q * jnp.exp(bd - mref)
                k_dn = k * jnp.exp(mref - bd)
                qe.append(q_up.astype(BF16))
                ke.append(k_dn.astype(BF16))
                qb.append((q_up * e_half).astype(BF16))
                kl.append((k_dn * e_half).astype(BF16))
            sc = lax.dot_general(jnp.concatenate(qe, axis=0), jnp.concatenate(ke, axis=0), contract_last,
                                 preferred_element_type=F32)
            att = sc[:C] * vis_f + pltpu.roll(sc[C:], C, 1) * vis_b
            acc_ref[rows, vlanes(h)] = _dot(att[:, :C].astype(BF16), vc)
            u_ref[h, c] = lax.dot_general(vc, jnp.concatenate(kl, axis=1), contract_first,
                                          preferred_element_type=F32)
            qb_ref[h, rows, :] = jnp.concatenate(qb, axis=1)
            dec_ref[h, dec_rows(c), :] = jnp.broadcast_to(
                jnp.exp(jnp.concatenate([btot_f, btot_b], axis=1)), (SUBLANES, 2 * DK_C))

    _loop(n_c, intra, static)

    def states(i):
        for h in heads:
            for d, c in ((0, i), (1, n_c - 1 - i)):
                lanes = slice(d * DK_C, (d + 1) * DK_C)
                st = st_ref[h, d]
                sst_ref[h, c, :, lanes] = st.astype(BF16)
                st_ref[h, d] = st * dec_ref[h, dec_rows(c, 1), lanes] + u_ref[h, c, :, lanes]

    _loop(n_c, states, static)

    def inter(c):
        rows = chunk_rows(c)
        for h in heads:
            acc_ref[rows, vlanes(h)] = acc_ref[rows, vlanes(h)] + lax.dot_general(
                qb_ref[h, rows, :], sst_ref[h, c], contract_last, preferred_element_type=F32)

    _loop(n_c, inter, static)

    for h in heads:
        sf_ref[0, h] = st_ref[h, 0]
        sb_ref[0, h] = st_ref[h, 1]
        o = acc_ref[:, vlanes(h)]
        o = o * lax.rsqrt(jnp.mean(o * o, axis=-1, keepdims=True) + EPS) * lng_ref[...]
        o_ref[0, :, vlanes(h)] = o * _silu(g_ref[0, :, vlanes(h)])


def _gla_mixer(q, k, v, g, gl, gw2, gbias, ln_g, s_init, layer_j):
    B, T, _ = q.shape
    C = GLA_CHUNK
    n_c = T // C
    hp = H_C if n_c <= 4 else 1
    idx = np.arange(C)
    tri = np.zeros((2, C, LANES), np.float32)
    tri[0, :, :C] = idx[:, None] >= idx[None, :]
    tri[1, :, :C] = idx[:, None] <= idx[None, :]
    tri = jnp.asarray(tri)
    w2 = jnp.stack([_block_diag2(gw2[0][:, h * DK_C:(h + 1) * DK_C], gw2[1][:, h * DK_C:(h + 1) * DK_C])
                    for h in range(H_C)]).astype(BF16)
    gb = jnp.stack([jnp.concatenate([gbias[0][h * DK_C:(h + 1) * DK_C], gbias[1][h * DK_C:(h + 1) * DK_C]])
                    for h in range(H_C)]).reshape(H_C, 1, 2 * DK_C)
    if s_init is None:
        s0f = s0b = jnp.zeros((1, 1, hp, DV_C, DK_C), F32)
        s_spec = pl.BlockSpec((1, 1, hp, DV_C, DK_C), lambda b, h: (0, 0, 0, 0, 0))
    else:
        s0f, s0b = (jnp.swapaxes(s, -1, -2) for s in s_init)
        s_spec = pl.BlockSpec((1, 1, hp, DV_C, DK_C), lambda b, h: (b, layer_j, h, 0, 0))
    o, sf, sb = pl.pallas_call(
        functools.partial(_gla_kernel, T, hp),
        grid=(B, H_C // hp),
        in_specs=[
            pl.BlockSpec((1, T, hp * DK_C), lambda b, h: (b, 0, h)),
            pl.BlockSpec((1, T, hp * DK_C), lambda b, h: (b, 0, h)),
            pl.BlockSpec((1, T, hp * DV_C), lambda b, h: (b, 0, h)),
            pl.BlockSpec((1, T, hp * DV_C), lambda b, h: (b, 0, h)),
            pl.BlockSpec((1, T, 2 * G_LORA), lambda b, h: (b, 0, 0)),
            pl.BlockSpec((hp, 2 * G_LORA, 2 * DK_C), lambda b, h: (h, 0, 0)),
            pl.BlockSpec((hp, 1, 2 * DK_C), lambda b, h: (h, 0, 0)),
            pl.BlockSpec((1, DV_C), lambda b, h: (0, 0)),
            pl.BlockSpec((2, C, LANES), lambda b, h: (0, 0, 0)),
            s_spec, s_spec,
        ],
        out_specs=[
            pl.BlockSpec((1, T, hp * DV_C), lambda b, h: (b, 0, h)),
            pl.BlockSpec((1, hp, DV_C, DK_C), lambda b, h: (b, h, 0, 0)),
            pl.BlockSpec((1, hp, DV_C, DK_C), lambda b, h: (b, h, 0, 0)),
        ],
        out_shape=[
            jax.ShapeDtypeStruct((B, T, H_C * DV_C), F32),
            jax.ShapeDtypeStruct((B, H_C, DV_C, DK_C), F32),
            jax.ShapeDtypeStruct((B, H_C, DV_C, DK_C), F32),
        ],
        scratch_shapes=[
            pltpu.VMEM((hp, T, 2 * DK_C), F32), pltpu.VMEM((T, hp * DV_C), F32),
            pltpu.VMEM((hp, T, 2 * DK_C), BF16), pltpu.VMEM((hp, n_c * SUBLANES, 2 * DK_C), F32),
            pltpu.VMEM((hp, n_c, DV_C, 2 * DK_C), F32), pltpu.VMEM((hp, n_c, DV_C, 2 * DK_C), BF16),
            pltpu.VMEM((hp, 2, DV_C, DK_C), F32),
        ],
        compiler_params=_params("parallel", "parallel"),
    )(q, k, v, g, gl, w2, gb, ln_g.reshape(1, -1), tri, s0f, s0b)
    return o, (jnp.swapaxes(sf, -1, -2), jnp.swapaxes(sb, -1, -2))


def kernel(x_prompt, x_sample, c, cache_attn_k, cache_attn_v, state_rwkv_fwd, state_rwkv_bwd, state_gla_fwd, state_gla_bwd, c_ctx, norm_g, mod_w, mod_b, ev_w_in, ev_w_out, ev_qn_g, ev_kn_g, ev_shift_mu, rw_w0, rw_w2, rw_a0, rw_a2, rw_kk, rw_ka, rw_rk, rw_ln_g, rw_ln_b, od_w_in, od_w_out, gla_w2, gla_b, gla_ln_g, final_g):
    n_dec = c.shape[0]
    cond = jnp.concatenate([c_ctx[None], c, jnp.zeros((SUBLANES - 1 - n_dec, D_MODEL), F32)], axis=0)
    mod = _modulation(cond, mod_w, mod_b)

    ev_in = ev_w_in.astype(BF16)
    ev_out = ev_w_out.astype(BF16)
    od_in = od_w_in.astype(BF16)
    od_out = od_w_out.astype(BF16)
    kw = KV_A * HD_A
    ck = cache_attn_k.reshape(cache_attn_k.shape[:3] + (kw,))
    cv = cache_attn_v.reshape(cache_attn_v.shape[:3] + (kw,))
    gla_f_t = state_gla_fwd
    gla_b_t = state_gla_bwd

    def trunk(x, latent):
        new = {"k": [], "v": [], "rf": [], "rb": [], "gf": [], "gb": []}
        for i in range(DEPTH):
            j = i // 2
            fg = final_g if i == DEPTH - 1 else None
            if i % 2 == 0:
                qa, ka, va, ga, zb, gb = _inproj(x, norm_g[i], mod, i, ev_in[j], EV_SPLITS, latent)
                if latent:
                    o_a, _ = _attention(qa, ka, va, ga, ev_qn_g[j], ev_kn_g[j], ck, cv, j)
                    s_init = (state_rwkv_fwd[:, j], state_rwkv_bwd[:, j])
                else:
                    o_a, kn = _attention(qa, ka, va, ga, ev_qn_g[j], ev_kn_g[j])
                    s_init = None
                    new["k"].append(kn)
                    new["v"].append(va)
                p = dict(shift_mu=ev_shift_mu[j], w0=rw_w0[j], w2=rw_w2[j], a0=rw_a0[j], a2=rw_a2[j],
                         k_k=rw_kk[j], k_a=rw_ka[j], r_k=rw_rk[j], ln_g=rw_ln_g[j], ln_b=rw_ln_b[j])
                o_b, fin = _rwkv_mixer(zb, gb, p, s_init)
                new["rf"].append(fin[0])
                new["rb"].append(fin[1])
                x = _outproj([o_a, o_b], [ev_out[j, :D_HALF], ev_out[j, D_HALF:]], x, mod, i, latent, fg)
            else:
                q, k, v, g, gl = _inproj(x, norm_g[i], mod, i, od_in[j], OD_SPLITS, latent)
                s_init = (gla_f_t, gla_b_t) if latent else None
                o, fin = _gla_mixer(q, k, v, g, gl, gla_w2[j], gla_b[j], gla_ln_g[j], s_init, j)
                new["gf"].append(fin[0])
                new["gb"].append(fin[1])
                x = _outproj([o], [od_out[j]], x, mod, i, latent, fg)
        return x, new

    y_prompt, new = trunk(x_prompt, False)
    y_sample, _ = trunk(x_sample, True)
    B, T = x_prompt.shape[:2]
    heads = lambda t: t.reshape(B, T, KV_A, HD_A)
    return (y_prompt, y_sample,
            jnp.stack([heads(t) for t in new["k"]], axis=1), jnp.stack([heads(t) for t in new["v"]], axis=1),
            jnp.stack(new["rf"], axis=1), jnp.stack(new["rb"], axis=1),
            jnp.stack(new["gf"], axis=1), jnp.stack(new["gb"], axis=1))
```

```python
import functools

import numpy as np
import jax
import jax.numpy as jnp
from jax import lax
from jax.experimental import pallas as pl
from jax.experimental.pallas import tpu as pltpu

F32 = jnp.float32
BF16 = jnp.bfloat16

D_MODEL = 1024
DEPTH = 4
GRID_W = 64
D_HALF = D_MODEL // 2
HD_A = 64
H_A = D_HALF // HD_A
KV_A = H_A // 4
ROPE_BASE = 10000.0
HS_B = 64
H_B = D_HALF // HS_B
W_LORA = 64
A_LORA = 64
RWKV_DECAY_SCALE = 0.606531
GN_EPS = 64e-5
B_SHIFT = 3 * D_HALF + 2 * W_LORA + 2 * A_LORA
H_C = 4
DK_C = D_MODEL // 2 // H_C
DV_C = D_MODEL // H_C
G_LORA = 16
GLA_TAU = 16.0
EPS = 1e-6

EV_SPLITS = (H_A * HD_A, KV_A * HD_A, KV_A * HD_A, D_HALF, B_SHIFT, D_HALF)
OD_SPLITS = (H_C * DK_C, H_C * DK_C, D_MODEL, D_MODEL, 2 * G_LORA)

LANES = 128
SUBLANES = 8
VMEM_LIMIT_BYTES = 56 * 1024 * 1024

ROW_TILE = 256
Q_TILE = 256
GLA_CHUNK = 64
SCAN_T = 8


def _params(*sem):
    return pltpu.CompilerParams(dimension_semantics=sem, vmem_limit_bytes=VMEM_LIMIT_BYTES)


def _silu(x):
    return x * jax.nn.sigmoid(x)


def _dot(a, b):
    return jnp.dot(a, b, preferred_element_type=F32)


def _seg_sum(x, ones_bd):
    hi = x.astype(BF16)
    r1 = x - hi.astype(F32)
    mid = r1.astype(BF16)
    lo = (r1 - mid.astype(F32)).astype(BF16)
    return _dot(hi, ones_bd) + _dot(mid, ones_bd) + _dot(lo, ones_bd)


def _get(ref):
    return ref[...].reshape(ref.shape[-2:])


def _put(ref, val):
    ref[...] = val.reshape(ref.shape)


def _block_diag_ones(n, blk):
    i = np.arange(n) // blk
    return jnp.asarray((i[:, None] == i[None, :]).astype(np.float32), dtype=BF16)


def _mod_kernel(cond_ref, w_ref, b_ref, o_ref):
    s = _silu(cond_ref[...])
    o_ref[0] = _dot(s.astype(BF16), w_ref[0].astype(BF16)) + b_ref[0]


def _modulation(cond, mod_w, mod_b):
    n = cond.shape[0]
    return pl.pallas_call(
        _mod_kernel,
        grid=(DEPTH, 3),
        in_specs=[
            pl.BlockSpec((n, D_MODEL), lambda i, j: (0, 0)),
            pl.BlockSpec((1, D_MODEL, D_MODEL), lambda i, j: (i, 0, j)),
            pl.BlockSpec((1, 1, D_MODEL), lambda i, j: (i, 0, j)),
        ],
        out_specs=pl.BlockSpec((1, n, D_MODEL), lambda i, j: (i, 0, j)),
        out_shape=jax.ShapeDtypeStruct((DEPTH, n, 3 * D_MODEL), F32),
        compiler_params=_params("parallel", "parallel"),
    )(cond, mod_w, mod_b.reshape(DEPTH, 1, 3 * D_MODEL))


def _mod_row(latent):
    return (1 + pl.program_id(0)) if latent else 0


def _inproj_kernel(latent, splits, x_ref, g_ref, sh_ref, sc_ref, w_ref, *out_refs):
    r = _mod_row(latent)
    x = x_ref[0]
    y = x * lax.rsqrt(jnp.mean(x * x, axis=-1, keepdims=True) + EPS) * g_ref[...]
    shift = sh_ref[0, pl.ds(r, 1), :]
    scale = sc_ref[0, pl.ds(r, 1), :]
    h = (y * (1.0 + scale) + shift).astype(BF16)
    off = 0
    for o_ref, n in zip(out_refs, splits):
        o_ref[0] = _dot(h, w_ref[:, off:off + n]).astype(o_ref.dtype)
        off += n


def _inproj(x, norm_g, mod, layer, w_bf16, splits, latent, dtypes=None):
    B, T, _ = x.shape
    dtypes = dtypes or (F32,) * len(splits)
    cols = w_bf16.shape[1]
    nrow = mod.shape[1]
    return pl.pallas_call(
        functools.partial(_inproj_kernel, latent, splits),
        grid=(B, T // ROW_TILE),
        in_specs=[
            pl.BlockSpec((1, ROW_TILE, D_MODEL), lambda b, i: (b, i, 0)),
            pl.BlockSpec((1, D_MODEL), lambda b, i: (0, 0)),
            pl.BlockSpec((1, nrow, D_MODEL), lambda b, i: (layer, 0, 0)),
            pl.BlockSpec((1, nrow, D_MODEL), lambda b, i: (layer, 0, 1)),
            pl.BlockSpec((D_MODEL, cols), lambda b, i: (0, 0)),
        ],
        out_specs=[pl.BlockSpec((1, ROW_TILE, n), lambda b, i: (b, i, 0)) for n in splits],
        out_shape=[jax.ShapeDtypeStruct((B, T, n), dt) for n, dt in zip(splits, dtypes)],
        compiler_params=_params("parallel", "parallel"),
    )(x, norm_g.reshape(1, D_MODEL), mod, mod, w_bf16)


def _outproj_kernel(latent, final, n_in, *refs):
    o_refs = refs[:n_in]
    w_refs = refs[n_in:2 * n_in]
    x_ref, gate_ref = refs[2 * n_in], refs[2 * n_in + 1]
    rest = refs[2 * n_in + 2:]
    r = _mod_row(latent)
    acc = _dot(o_refs[0][0].astype(BF16), w_refs[0][...])
    for o_ref, w_ref in zip(o_refs[1:], w_refs[1:]):
        acc = acc + _dot(o_ref[0].astype(BF16), w_ref[...])
    y = x_ref[0] + gate_ref[0, pl.ds(r, 1), :] * acc
    if final:
        fg_ref, out_ref = rest
        y = y * lax.rsqrt(jnp.mean(y * y, axis=-1, keepdims=True) + EPS) * fg_ref[...]
    else:
        (out_ref,) = rest
    out_ref[0] = y


def _outproj(outs, ws_bf16, x, mod, layer, latent, final_g=None):
    B, T, _ = x.shape
    nrow = mod.shape[1]
    n_in = len(outs)
    final = final_g is not None
    in_specs = [pl.BlockSpec((1, ROW_TILE, o.shape[-1]), lambda b, i: (b, i, 0)) for o in outs]
    in_specs += [pl.BlockSpec(w.shape, lambda b, i: (0, 0)) for w in ws_bf16]
    in_specs += [
        pl.BlockSpec((1, ROW_TILE, D_MODEL), lambda b, i: (b, i, 0)),
        pl.BlockSpec((1, nrow, D_MODEL), lambda b, i: (layer, 0, 2)),
    ]
    args = list(outs) + list(ws_bf16) + [x, mod]
    if final:
        in_specs.append(pl.BlockSpec((1, D_MODEL), lambda b, i: (0, 0)))
        args.append(final_g.reshape(1, D_MODEL))
    return pl.pallas_call(
        functools.partial(_outproj_kernel, latent, final, n_in),
        grid=(B, T // ROW_TILE),
        in_specs=in_specs,
        out_specs=pl.BlockSpec((1, ROW_TILE, D_MODEL), lambda b, i: (b, i, 0)),
        out_shape=jax.ShapeDtypeStruct((B, T, D_MODEL), F32),
        compiler_params=_params("parallel", "parallel"),
    )(*args)


def _rope_tables(T):
    n_rows = T // GRID_W
    row = jnp.repeat(jnp.arange(n_rows), GRID_W).astype(F32)
    col = jnp.tile(jnp.arange(GRID_W), n_rows).astype(F32)
    n_freq = HD_A // 4
    inv = ROPE_BASE ** (-jnp.arange(n_freq, dtype=F32) / n_freq)
    ang_r = row[:, None] * inv
    ang_c = col[:, None] * inv
    zero = jnp.zeros_like(ang_r)
    cos = jnp.concatenate([jnp.cos(ang_r), jnp.cos(ang_r), jnp.cos(ang_c), jnp.cos(ang_c)], axis=1)
    s1 = jnp.concatenate([-jnp.sin(ang_r), zero, -jnp.sin(ang_c), zero], axis=1)
    s2 = jnp.concatenate([zero, jnp.sin(ang_r), zero, jnp.sin(ang_c)], axis=1)
    return cos, s1, s2


def _rope(x, cos, s1, s2):
    n = x.shape[-1]
    q = HD_A // 4
    return x * cos + pltpu.roll(x, n - q, 1) * s1 + pltpu.roll(x, q, 1) * s2


def _attn_kernel(latent, S, *refs):
    if latent:
        (q_ref, k_ref, v_ref, ga_ref, qg_ref, kg_ref, bd_ref, cos_ref, s1_ref, s2_ref,
         cosk_ref, s1k_ref, s2k_ref, ck_ref, cv_ref, o_ref, km_ref, vm_ref) = refs
    else:
        (q_ref, k_ref, v_ref, ga_ref, qg_ref, kg_ref, bd_ref, o_ref, kn_ref, km_ref, vm_ref) = refs
    bd = bd_ref[...]
    inv_d = 1.0 / HD_A
    kw = KV_A * HD_A

    @pl.when(pl.program_id(1) == 0)
    def _():
        k = k_ref[0]
        kn = k * lax.rsqrt(_seg_sum(k * k, bd[:kw, :kw]) * inv_d + EPS) * kg_ref[...]
        v = v_ref[0]
        if latent:
            kn = _rope(kn, cosk_ref[...], s1k_ref[...], s2k_ref[...])
            k_all = jnp.concatenate([ck_ref[0, 0], kn], axis=0)
            v_all = jnp.concatenate([cv_ref[0, 0], v], axis=0)
        else:
            kn_ref[0] = kn
            k_all, v_all = kn, v
        lane = lax.broadcasted_iota(jnp.int32, (S, kw), 1)
        k_sw = pltpu.roll(k_all, HD_A, 1)
        v_sw = pltpu.roll(v_all, HD_A, 1)
        for j in range(KV_A):
            for half in range(2):
                keep = (lane < HD_A) if half == 0 else (lane >= HD_A)
                src_k, src_v = (k_all, v_all) if j == half else (k_sw, v_sw)
                km_ref[2 * j + half] = jnp.where(keep, src_k, 0.0).astype(BF16)
                vm_ref[2 * j + half] = jnp.where(keep, src_v, 0.0).astype(BF16)

    q = q_ref[0]
    qn = q * lax.rsqrt(_seg_sum(q * q, bd) * inv_d + EPS) * qg_ref[...]
    if latent:
        qn = _rope(qn, cos_ref[...], s1_ref[...], s2_ref[...])
    qb = (qn * HD_A ** -0.5).astype(BF16)
    for m in range(H_A // 2):
        blk = slice(m * LANES, (m + 1) * LANES)
        qs = qb[:, blk]
        acc = None
        for half in range(2):
            j = (2 * m + half) // (H_A // KV_A)
            s = lax.dot_general(qs, km_ref[2 * j + half], (((1,), (1,)), ((), ())),
                                preferred_element_type=F32)
            e = jnp.exp(s - jnp.max(s, axis=-1, keepdims=True))
            l = jnp.sum(e, axis=-1, keepdims=True)
            pv = _dot(e.astype(BF16), vm_ref[2 * j + half]) * (1.0 / l)
            acc = pv if acc is None else acc + pv
        o_ref[0, :, blk] = (acc * _silu(ga_ref[0, :, blk])).astype(o_ref.dtype)


def _attention(qa, ka, va, ga, qn_g, kn_g, ctx_k=None, ctx_v=None, layer_j=0):
    B, T, _ = qa.shape
    latent = ctx_k is not None
    S = T + (ctx_k.shape[2] if latent else 0)
    kw = KV_A * HD_A
    qw = H_A * HD_A
    bd = _block_diag_ones(qw, HD_A)
    qblk = pl.BlockSpec((1, Q_TILE, qw), lambda b, i: (b, i, 0))
    kblk = pl.BlockSpec((1, T, kw), lambda b, i: (b, 0, 0))
    in_specs = [qblk, kblk, kblk, qblk,
                pl.BlockSpec((1, qw), lambda b, i: (0, 0)),
                pl.BlockSpec((1, kw), lambda b, i: (0, 0)),
                pl.BlockSpec(bd.shape, lambda b, i: (0, 0))]
    args = [qa, ka, va, ga, jnp.tile(qn_g, H_A).reshape(1, -1), jnp.tile(kn_g, KV_A).reshape(1, -1), bd]
    out_specs = [qblk]
    out_shape = [jax.ShapeDtypeStruct((B, T, qw), BF16)]
    if latent:
        tabs = _rope_tables(T)
        P = ctx_k.shape[2]
        in_specs += [pl.BlockSpec((Q_TILE, qw), lambda b, i: (i, 0))] * 3
        in_specs += [pl.BlockSpec((T, kw), lambda b, i: (0, 0))] * 3
        in_specs += [pl.BlockSpec((1, 1, P, kw), lambda b, i: (b, layer_j, 0, 0))] * 2
        args += [jnp.tile(t, (1, H_A)) for t in tabs] + [jnp.tile(t, (1, KV_A)) for t in tabs] + [ctx_k, ctx_v]
    else:
        out_specs.append(kblk)
        out_shape.append(jax.ShapeDtypeStruct((B, T, kw), F32))
    res = pl.pallas_call(
        functools.partial(_attn_kernel, latent, S),
        grid=(B, T // Q_TILE),
        in_specs=in_specs,
        out_specs=out_specs,
        out_shape=out_shape,
        scratch_shapes=[pltpu.VMEM((2 * KV_A, S, kw), BF16), pltpu.VMEM((2 * KV_A, S, kw), BF16)],
        compiler_params=_params("parallel", "arbitrary"),
    )(*args)
    return res if not latent else (res[0], None)


def _rwkv_prep_kernel(n_t, z_ref, zp_ref, zn_ref, mu_ref, w0_ref, a0_ref, w2_ref, a2_ref,
                      kkg_ref, ka_ref, rk_ref, bd_ref,
                      wf_ref, wb_ref, kdf_ref, kdb_ref, kaf_ref, kab_ref, nkk_ref, r_ref, v_ref, bonus_ref):
    i = pl.program_id(1)
    z = z_ref[0]
    n = z.shape[0]
    prev_row = jnp.where(i > 0, zp_ref[0, SUBLANES - 1:SUBLANES, :], 0.0)
    next_row = jnp.where(i < n_t - 1, zn_ref[0, 0:1, :], 0.0)
    rows = lax.broadcasted_iota(jnp.int32, (n, 1), 0)
    zp = jnp.where(rows == 0, prev_row, pltpu.roll(z, 1, 0))
    zn = jnp.where(rows == n - 1, next_row, pltpu.roll(z, n - 1, 0))
    zs = z + mu_ref[...] * (0.5 * (zp + zn) - z)

    rb = zs[:, 0:D_HALF]
    kb = zs[:, D_HALF:2 * D_HALF]
    vb = zs[:, 2 * D_HALF:3 * D_HALF]
    lw = zs[:, 3 * D_HALF:3 * D_HALF + 2 * W_LORA]
    la = zs[:, 3 * D_HALF + 2 * W_LORA:]
    w = jnp.exp(-RWKV_DECAY_SCALE * jax.nn.sigmoid(w0_ref[...] + _dot(jnp.tanh(lw).astype(BF16), w2_ref[...])))
    a = jax.nn.sigmoid(a0_ref[...] + _dot(la.astype(BF16), a2_ref[...]))
    bd = bd_ref[...]
    kk = kb * kkg_ref[...]
    kk = kk * lax.rsqrt(_seg_sum(kk * kk, bd) + 1e-12)
    ka = ka_ref[...]
    a_f, a_b = a[:, :D_HALF], a[:, D_HALF:]
    kd_f = kb * (1.0 + (a_f - 1.0) * ka)
    kd_b = kb * (1.0 + (a_b - 1.0) * ka)
    _put(wf_ref, w[:, :D_HALF])
    _put(wb_ref, w[:, D_HALF:])
    _put(kdf_ref, kd_f)
    _put(kdb_ref, kd_b)
    _put(kaf_ref, kk * a_f)
    _put(kab_ref, kk * a_b)
    _put(nkk_ref, -kk)
    _put(r_ref, rb)
    _put(v_ref, vb)
    _put(bonus_ref, _seg_sum(rb * rk_ref[...] * (kd_f + kd_b), bd) * vb)


def _block_diag2(m0, m1):
    z = jnp.zeros_like(m0)
    return jnp.concatenate([jnp.concatenate([m0, z], axis=1), jnp.concatenate([z, m1], axis=1)], axis=0)


def _rwkv_prep(zb, shift_mu, w0, w2, a0, a2, k_k, k_a, r_k):
    B, T, _ = zb.shape
    n_t = T // ROW_TILE
    per_tile = ROW_TILE // SUBLANES
    bd = _block_diag_ones(D_HALF, HS_B)
    row = lambda x: x.reshape(1, -1)
    vec = pl.BlockSpec((1, D_HALF), lambda b, i: (0, 0))
    vec2 = pl.BlockSpec((1, 2 * D_HALF), lambda b, i: (0, 0))
    out = pl.BlockSpec((1, ROW_TILE, D_HALF), lambda b, i: (b, i, 0))
    return pl.pallas_call(
        functools.partial(_rwkv_prep_kernel, n_t),
        grid=(B, n_t),
        in_specs=[
            pl.BlockSpec((1, ROW_TILE, B_SHIFT), lambda b, i: (b, i, 0)),
            pl.BlockSpec((1, SUBLANES, B_SHIFT), lambda b, i: (b, jnp.maximum(i * per_tile - 1, 0), 0)),
            pl.BlockSpec((1, SUBLANES, B_SHIFT),
                         lambda b, i: (b, jnp.minimum((i + 1) * per_tile, n_t * per_tile - 1), 0)),
            pl.BlockSpec((1, B_SHIFT), lambda b, i: (0, 0)),
            vec2, vec2,
            pl.BlockSpec((2 * W_LORA, 2 * D_HALF), lambda b, i: (0, 0)),
            pl.BlockSpec((2 * A_LORA, 2 * D_HALF), lambda b, i: (0, 0)),
            vec, vec, vec,
            pl.BlockSpec(bd.shape, lambda b, i: (0, 0)),
        ],
        out_specs=[out] * 10,
        out_shape=[jax.ShapeDtypeStruct((B, T, D_HALF), F32)] * 10,
        compiler_params=_params("parallel", "parallel"),
    )(zb, zb, zb, row(shift_mu), row(w0), row(a0),
      _block_diag2(w2[0], w2[1]).astype(BF16), _block_diag2(a2[0], a2[1]).astype(BF16),
      row(k_k), row(k_a), row(r_k), bd)


CHAIN_ROWS = 32
HEAD_PAIRS = H_B // 2


def _step_rows(ref, t, vs):
    if vs == 1:
        return ref[:, t, :]
    return jnp.concatenate([jnp.broadcast_to(ref[b, t:t + 1, :], (vs, ref.shape[2])) for b in range(ref.shape[0])],
                           axis=0)


def _to_chains(x):
    return jnp.concatenate([x[:, j * LANES:(j + 1) * LANES] for j in range(HEAD_PAIRS)], axis=0).T


def _rwkv_scan_kernel(reverse, vs, w_ref, nkk_ref, kka_ref, kd_ref, r_ref, v_ref, s0_ref, y_ref, s_ref,
                      vbuf, ybuf, *kbuf):
    vh = HS_B // vs

    @pl.when(pl.program_id(0) == 0)
    def _():
        s_ref[...] = s0_ref[...]

    split = lax.broadcasted_iota(jnp.int32, (vh, LANES), 1) % vs
    steps = range(SCAN_T - 1, -1, -1) if reverse else range(SCAN_T)

    def relayout(t):
        return tuple(_to_chains(_step_rows(ref, t, vs)) for ref in (w_ref, nkk_ref, kka_ref, kd_ref, r_ref, v_ref))

    def emit_y(t):
        halves = []
        for par in range(2):
            y = ybuf[t if kbuf else 0, par]
            if vs > 1:
                y = jnp.concatenate([jnp.where(split == q, y, 0.0) for q in range(vs)], axis=0)
            halves.append(y)
        yt = jnp.concatenate(halves, axis=0).T
        for j in range(HEAD_PAIRS):
            blk = yt[j * CHAIN_ROWS:(j + 1) * CHAIN_ROWS]
            if vs > 1:
                blk = jnp.sum(blk.reshape(CHAIN_ROWS // vs, vs, LANES), axis=1)
            y_ref[:, t, j * LANES:(j + 1) * LANES] = blk

    if kbuf:
        for t in steps:
            for a, x in enumerate(relayout(t)):
                kbuf[0][a, t] = x

    for t in steps:
        w, nkk, kka, kd, r, v_all = (kbuf[0][a, t] for a in range(6)) if kbuf else relayout(t)
        yslot = t if kbuf else 0
        for par in range(2):
            rows = slice(par * HS_B, (par + 1) * HS_B)
            lanes = slice(par * LANES, (par + 1) * LANES)
            v_par = v_all[rows]
            v_own = v_par[:vh]
            for q in range(1, vs):
                v_own = jnp.where(split == q, v_par[q * vh:(q + 1) * vh], v_own)
            vbuf[par] = v_own
            kvec = (w[rows], nkk[rows], kka[rows], kd[rows], r[rows])

            def v_step(g, c, par=par, lanes=lanes, kvec=kvec, yslot=yslot):
                w_, nkk_, kka_, kd_, r_ = kvec
                base = pl.multiple_of(g * SUBLANES, SUBLANES)
                vblk = vbuf[par, pl.ds(base, SUBLANES), :]
                ys = []
                for u in range(SUBLANES):
                    s = s_ref[base + u, :, lanes]
                    sa = jnp.sum(s * nkk_, axis=0, keepdims=True)
                    s = s * w_ + sa * kka_ + vblk[u:u + 1, :] * kd_
                    s_ref[base + u, :, lanes] = s
                    ys.append(jnp.sum(s * r_, axis=0, keepdims=True))
                ybuf[yslot, par, pl.ds(base, SUBLANES), :] = jnp.concatenate(ys, axis=0)
                return c

            lax.fori_loop(0, vh // SUBLANES, v_step, 0)
        if not kbuf:
            emit_y(t)

    if kbuf:
        for t in steps:
            emit_y(t)


def _rwkv_scan(w, nkk, kka, kd, r, v, s0, reverse):
    B, T, W = w.shape
    vs = CHAIN_ROWS // B
    vh = HS_B // vs
    n_t = T // SCAN_T
    hoist = vh > SUBLANES
    tblk = (lambda t: n_t - 1 - t) if reverse else (lambda t: t)
    vec = pl.BlockSpec((B, SCAN_T, W), lambda t: (0, tblk(t), 0))
    state = pl.BlockSpec((vh, HS_B, 2 * LANES), lambda t: (0, 0, 0))
    return pl.pallas_call(
        functools.partial(_rwkv_scan_kernel, reverse, vs),
        grid=(n_t,),
        in_specs=[vec] * 6 + [state],
        out_specs=[vec, state],
        out_shape=[jax.ShapeDtypeStruct((B, T, W), F32), jax.ShapeDtypeStruct((vh, HS_B, 2 * LANES), F32)],
        scratch_shapes=[pltpu.VMEM((2, vh, LANES), F32), pltpu.VMEM((SCAN_T if hoist else 1, 2, vh, LANES), F32)]
        + ([pltpu.VMEM((6, SCAN_T, LANES, LANES), F32)] if hoist else []),
        compiler_params=_params("arbitrary"),
    )(w, nkk, kka, kd, r, v, s0)


def _rwkv_post_kernel(yf_ref, yb_ref, bonus_ref, gb_ref, lng_ref, lnb_ref, bd_ref, o_ref):
    bd = bd_ref[...]
    y = _get(yf_ref) + _get(yb_ref)
    inv_n = 1.0 / HS_B
    d = y - _seg_sum(y, bd) * inv_n
    var = _seg_sum(d * d, bd) * inv_n
    yn = d * lax.rsqrt(var + GN_EPS) * lng_ref[...] + lnb_ref[...]
    o_ref[0] = ((yn + bonus_ref[0]) * _silu(gb_ref[0])).astype(o_ref.dtype)


def _rwkv_post(y_f, y_b, bonus, gb, ln_g, ln_b):
    B, T, _ = y_f.shape
    bd = _block_diag_ones(D_HALF, HS_B)
    blk = pl.BlockSpec((1, ROW_TILE, D_HALF), lambda b, i: (b, i, 0))
    vec = pl.BlockSpec((1, D_HALF), lambda b, i: (0, 0))
    return pl.pallas_call(
        _rwkv_post_kernel,
        grid=(B, T // ROW_TILE),
        in_specs=[blk, blk, blk, blk, vec, vec, pl.BlockSpec(bd.shape, lambda b, i: (0, 0))],
        out_specs=blk,
        out_shape=jax.ShapeDtypeStruct((B, T, D_HALF), BF16),
        compiler_params=_params("parallel", "parallel"),
    )(y_f, y_b, bonus, gb, ln_g.reshape(1, -1), ln_b.reshape(1, -1), bd)


def _rwkv_mixer(zb, gb, p, s_init):
    B, T, _ = zb.shape
    (w_f, w_b, kd_f, kd_b, ka_f, ka_b, nkk, r, v, bonus) = _rwkv_prep(
        zb, p["shift_mu"], p["w0"], p["w2"], p["a0"], p["a2"], p["k_k"], p["k_a"], p["r_k"])
    vs = CHAIN_ROWS // B
    vh = HS_B // vs
    chains = H_B * B * vs
    ys, fin = [], []
    for d, (w_d, ka_d, kd_d) in enumerate(((w_f, ka_f, kd_f), (w_b, ka_b, kd_b))):
        if s_init is None:
            s0 = jnp.zeros((vh, HS_B, chains), F32)
        else:
            s0 = s_init[d].reshape(B, HEAD_PAIRS, 2, vs, vh, HS_B).transpose(4, 5, 2, 1, 0, 3)
            s0 = s0.reshape(vh, HS_B, chains)
        y, s_fin = _rwkv_scan(w_d, nkk, ka_d, kd_d, r, v, s0, reverse=(d == 1))
        ys.append(y)
        fin.append(s_fin.reshape(vh, HS_B, 2, HEAD_PAIRS, B, vs).transpose(4, 3, 2, 5, 0, 1)
                   .reshape(B, H_B, HS_B, HS_B))
    return _rwkv_post(ys[0], ys[1], bonus, gb, p["ln_g"], p["ln_b"]), fin


def _chunk_cumsum(x, chunk, suffix):
    T = x.shape[0]
    pos = lax.broadcasted_iota(jnp.int32, (T, 1), 0) % chunk
    step = 1
    while step < chunk:
        if suffix:
            x = x + jnp.where(pos < chunk - step, pltpu.roll(x, T - step, 0), 0.0)
        else:
            x = x + jnp.where(pos >= step, pltpu.roll(x, step, 0), 0.0)
        step *= 2
    return x


def _loop(n, body, static):
    if static:
        for i in range(n):
            body(i)
    else:
        lax.fori_loop(0, n, lambda i, c: (body(i), c)[1], 0, unroll=4)


def _gla_kernel(T, hp, q_ref, k_ref, v_ref, g_ref, gl_ref, w2_ref, gb_ref, lng_ref, tri_ref, s0f_ref, s0b_ref,
                *rest):
    o_ref, sf_ref, sb_ref, b_ref, acc_ref, qb_ref, dec_ref, u_ref, sst_ref, st_ref = rest[-10:]
    C = GLA_CHUNK
    n_c = T // C
    static = n_c <= 4
    qscale = DK_C ** -0.5
    contract_last = (((1,), (1,)), ((), ()))
    contract_first = (((0,), (0,)), ((), ()))
    vis_f = tri_ref[0]
    vis_b = tri_ref[1]
    heads = range(hp)
    klanes = lambda h: slice(h * DK_C, (h + 1) * DK_C)
    vlanes = lambda h: slice(h * DV_C, (h + 1) * DV_C)

    for h in heads:
        pre = _dot(gl_ref[0].astype(BF16), w2_ref[h]) + gb_ref[h]
        la = jax.nn.log_sigmoid(pre) * (1.0 / GLA_TAU)
        b_ref[h, :, :DK_C] = _chunk_cumsum(la[:, :DK_C], C, suffix=False)
        b_ref[h, :, DK_C:] = _chunk_cumsum(la[:, DK_C:], C, suffix=True)
        st_ref[h, 0] = s0f_ref[0, 0, h].T
        st_ref[h, 1] = s0b_ref[0, 0, h].T

    def chunk_rows(c):
        return pl.ds(c * C, C) if static else pl.ds(pl.multiple_of(c * C, C), C)

    def dec_rows(c, n=SUBLANES):
        return pl.ds(c * SUBLANES, n) if static else pl.ds(pl.multiple_of(c * SUBLANES, SUBLANES), n)

    def intra(c):
        rows = chunk_rows(c)
        for h in heads:
            q = q_ref[0, rows, klanes(h)] * qscale
            k = k_ref[0, rows, klanes(h)]
            vc = v_ref[0, rows, vlanes(h)].astype(BF16)
            b = b_ref[h, rows, :]
            btot_f = b[C - 1:C, :DK_C]
            btot_b = b[0:1, DK_C:]
            qe, ke, qb, kl = [], [], [], []
            for bd, btot in ((b[:, :DK_C], btot_f), (b[:, DK_C:], btot_b)):
                mref = 0.5 * btot
                e_half = jnp.exp(mref)
                q_up = q * jnp.exp(bd - mref)
                k_dn = k * jnp.exp(mref - bd)
                qe.append(q_up.astype(BF16))
                ke.append(k_dn.astype(BF16))
                qb.append((q_up * e_half).astype(BF16))
                kl.append((k_dn * e_half).astype(BF16))
            sc = lax.dot_general(jnp.concatenate(qe, axis=0), jnp.concatenate(ke, axis=0), contract_last,
                                 preferred_element_type=F32)
            att = sc[:C] * vis_f + pltpu.roll(sc[C:], C, 1) * vis_b
            acc_ref[rows, vlanes(h)] = _dot(att[:, :C].astype(BF16), vc)
            u_ref[h, c] = lax.dot_general(vc, jnp.concatenate(kl, axis=1), contract_first,
                                          preferred_element_type=F32)
            qb_ref[h, rows, :] = jnp.concatenate(qb, axis=1)
            dec_ref[h, dec_rows(c), :] = jnp.broadcast_to(
                jnp.exp(jnp.concatenate([btot_f, btot_b], axis=1)), (SUBLANES, 2 * DK_C))

    _loop(n_c, intra, static)

    def states(i):
        for h in heads:
            for d, c in ((0, i), (1, n_c - 1 - i)):
                lanes = slice(d * DK_C, (d + 1) * DK_C)
                st = st_ref[h, d]
                sst_ref[h, c, :, lanes] = st.astype(BF16)
                st_ref[h, d] = st * dec_ref[h, dec_rows(c, 1), lanes] + u_ref[h, c, :, lanes]

    _loop(n_c, states, static)

    def inter(c):
        rows = chunk_rows(c)
        for h in heads:
            acc_ref[rows, vlanes(h)] = acc_ref[rows, vlanes(h)] + lax.dot_general(
                qb_ref[h, rows, :], sst_ref[h, c], contract_last, preferred_element_type=F32)

    _loop(n_c, inter, static)

    for h in heads:
        sf_ref[0, 0, h] = st_ref[h, 0].T
        sb_ref[0, 0, h] = st_ref[h, 1].T
        o = acc_ref[:, vlanes(h)]
        o = o * lax.rsqrt(jnp.mean(o * o, axis=-1, keepdims=True) + EPS) * lng_ref[...]
        o_ref[0, :, vlanes(h)] = (o * _silu(g_ref[0, :, vlanes(h)])).astype(o_ref.dtype)


def _gla_mixer(q, k, v, g, gl, gw2, gbias, ln_g, s_init, layer_j, prev_fin=None):
    B, T, _ = q.shape
    C = GLA_CHUNK
    n_c = T // C
    hp = H_C if n_c <= 4 else 1
    idx = np.arange(C)
    tri = np.zeros((2, C, LANES), np.float32)
    tri[0, :, :C] = idx[:, None] >= idx[None, :]
    tri[1, :, :C] = idx[:, None] <= idx[None, :]
    tri = jnp.asarray(tri)
    w2 = jnp.stack([_block_diag2(gw2[0][:, h * DK_C:(h + 1) * DK_C], gw2[1][:, h * DK_C:(h + 1) * DK_C])
                    for h in range(H_C)]).astype(BF16)
    gb = jnp.stack([jnp.concatenate([gbias[0][h * DK_C:(h + 1) * DK_C], gbias[1][h * DK_C:(h + 1) * DK_C]])
                    for h in range(H_C)]).reshape(H_C, 1, 2 * DK_C)
    if s_init is None:
        s0f = s0b = jnp.zeros((1, 1, hp, DK_C, DV_C), F32)
        s_spec = pl.BlockSpec((1, 1, hp, DK_C, DV_C), lambda b, h: (0, 0, 0, 0, 0))
    else:
        s0f, s0b = s_init
        s_spec = pl.BlockSpec((1, 1, hp, DK_C, DV_C), lambda b, h: (b, layer_j, h, 0, 0))
    n_odd = DEPTH // 2
    st_shape = (B, n_odd, H_C, DK_C, DV_C)
    st_out = pl.BlockSpec((1, 1, hp, DK_C, DV_C), lambda b, h: (b, layer_j, h, 0, 0))
    args = [q, k, v, g, gl, w2, gb, ln_g.reshape(1, -1), tri, s0f, s0b]
    extra_specs, aliases = [], {}
    if prev_fin is not None:
        extra_specs = [pl.BlockSpec(memory_space=pl.ANY)] * 2
        aliases = {len(args): 1, len(args) + 1: 2}
        args += list(prev_fin)
    o, sf, sb = pl.pallas_call(
        functools.partial(_gla_kernel, T, hp),
        grid=(B, H_C // hp),
        input_output_aliases=aliases,
        in_specs=[
            pl.BlockSpec((1, T, hp * DK_C), lambda b, h: (b, 0, h)),
            pl.BlockSpec((1, T, hp * DK_C), lambda b, h: (b, 0, h)),
            pl.BlockSpec((1, T, hp * DV_C), lambda b, h: (b, 0, h)),
            pl.BlockSpec((1, T, hp * DV_C), lambda b, h: (b, 0, h)),
            pl.BlockSpec((1, T, 2 * G_LORA), lambda b, h: (b, 0, 0)),
            pl.BlockSpec((hp, 2 * G_LORA, 2 * DK_C), lambda b, h: (h, 0, 0)),
            pl.BlockSpec((hp, 1, 2 * DK_C), lambda b, h: (h, 0, 0)),
            pl.BlockSpec((1, DV_C), lambda b, h: (0, 0)),
            pl.BlockSpec((2, C, LANES), lambda b, h: (0, 0, 0)),
            s_spec, s_spec,
        ] + extra_specs,
        out_specs=[pl.BlockSpec((1, T, hp * DV_C), lambda b, h: (b, 0, h)), st_out, st_out],
        out_shape=[
            jax.ShapeDtypeStruct((B, T, H_C * DV_C), BF16),
            jax.ShapeDtypeStruct(st_shape, F32),
            jax.ShapeDtypeStruct(st_shape, F32),
        ],
        scratch_shapes=[
            pltpu.VMEM((hp, T, 2 * DK_C), F32), pltpu.VMEM((T, hp * DV_C), F32),
            pltpu.VMEM((hp, T, 2 * DK_C), BF16), pltpu.VMEM((hp, n_c * SUBLANES, 2 * DK_C), F32),
            pltpu.VMEM((hp, n_c, DV_C, 2 * DK_C), F32), pltpu.VMEM((hp, n_c, DV_C, 2 * DK_C), BF16),
            pltpu.VMEM((hp, 2, DV_C, DK_C), F32),
        ],
        compiler_params=_params("parallel", "parallel"),
    )(*args)
    return o, (sf, sb)


def kernel(x_prompt, x_sample, c, cache_attn_k, cache_attn_v, state_rwkv_fwd, state_rwkv_bwd, state_gla_fwd, state_gla_bwd, c_ctx, norm_g, mod_w, mod_b, ev_w_in, ev_w_out, ev_qn_g, ev_kn_g, ev_shift_mu, rw_w0, rw_w2, rw_a0, rw_a2, rw_kk, rw_ka, rw_rk, rw_ln_g, rw_ln_b, od_w_in, od_w_out, gla_w2, gla_b, gla_ln_g, final_g):
    n_dec = c.shape[0]
    cond = jnp.concatenate([c_ctx[None], c, jnp.zeros((SUBLANES - 1 - n_dec, D_MODEL), F32)], axis=0)
    mod = _modulation(cond, mod_w, mod_b)

    ev_in = ev_w_in.astype(BF16)
    ev_out = ev_w_out.astype(BF16)
    od_in = od_w_in.astype(BF16)
    od_out = od_w_out.astype(BF16)
    kw = KV_A * HD_A
    ck = cache_attn_k.reshape(cache_attn_k.shape[:3] + (kw,))
    cv = cache_attn_v.reshape(cache_attn_v.shape[:3] + (kw,))

    def trunk(x, latent):
        new = {"k": [], "v": [], "rf": [], "rb": []}
        gla_fin = None
        for i in range(DEPTH):
            j = i // 2
            fg = final_g if i == DEPTH - 1 else None
            if i % 2 == 0:
                qa, ka, va, ga, zb, gb = _inproj(x, norm_g[i], mod, i, ev_in[j], EV_SPLITS, latent)
                if latent:
                    o_a, _ = _attention(qa, ka, va, ga, ev_qn_g[j], ev_kn_g[j], ck, cv, j)
                    s_init = (state_rwkv_fwd[:, j], state_rwkv_bwd[:, j])
                else:
                    o_a, kn = _attention(qa, ka, va, ga, ev_qn_g[j], ev_kn_g[j])
                    s_init = None
                    new["k"].append(kn)
                    new["v"].append(va)
                p = dict(shift_mu=ev_shift_mu[j], w0=rw_w0[j], w2=rw_w2[j], a0=rw_a0[j], a2=rw_a2[j],
                         k_k=rw_kk[j], k_a=rw_ka[j], r_k=rw_rk[j], ln_g=rw_ln_g[j], ln_b=rw_ln_b[j])
                o_b, fin = _rwkv_mixer(zb, gb, p, s_init)
                new["rf"].append(fin[0])
                new["rb"].append(fin[1])
                x = _outproj([o_a, o_b], [ev_out[j, :D_HALF], ev_out[j, D_HALF:]], x, mod, i, latent, fg)
            else:
                q, k, v, g, gl = _inproj(x, norm_g[i], mod, i, od_in[j], OD_SPLITS, latent,
                                         dtypes=(F32, F32, BF16, F32, F32))
                s_init = (state_gla_fwd, state_gla_bwd) if latent else None
                o, gla_fin = _gla_mixer(q, k, v, g, gl, gla_w2[j], gla_b[j], gla_ln_g[j], s_init, j, gla_fin)
                x = _outproj([o], [od_out[j]], x, mod, i, latent, fg)
        new["gf"], new["gb"] = gla_fin
        return x, new

    y_prompt, new = trunk(x_prompt, False)
    y_sample, _ = trunk(x_sample, True)
    B, T = x_prompt.shape[:2]
    heads = lambda t: t.reshape(B, T, KV_A, HD_A)
    return (y_prompt, y_sample,
            jnp.stack([heads(t) for t in new["k"]], axis=1), jnp.stack([heads(t) for t in new["v"]], axis=1),
            jnp.stack(new["rf"], axis=1), jnp.stack(new["rb"], axis=1),
            new["gf"], new["gb"])
```

```python
import functools

import numpy as np
import jax
import jax.numpy as jnp
from jax import lax
from jax.experimental import pallas as pl
from jax.experimental.pallas import tpu as pltpu

F32 = jnp.float32
BF16 = jnp.bfloat16

D_MODEL = 1024
DEPTH = 4
GRID_W = 64
D_HALF = D_MODEL // 2
HD_A = 64
H_A = D_HALF // HD_A
KV_A = H_A // 4
ROPE_BASE = 10000.0
HS_B = 64
H_B = D_HALF // HS_B
W_LORA = 64
A_LORA = 64
RWKV_DECAY_SCALE = 0.606531
GN_EPS = 64e-5
B_SHIFT = 3 * D_HALF + 2 * W_LORA + 2 * A_LORA
H_C = 4
DK_C = D_MODEL // 2 // H_C
DV_C = D_MODEL // H_C
G_LORA = 16
GLA_TAU = 16.0
EPS = 1e-6

EV_SPLITS = (H_A * HD_A, KV_A * HD_A, KV_A * HD_A, D_HALF, B_SHIFT, D_HALF)
OD_SPLITS = (H_C * DK_C, H_C * DK_C, D_MODEL, D_MODEL, 2 * G_LORA)

LANES = 128
SUBLANES = 8
VMEM_LIMIT_BYTES = 56 * 1024 * 1024

ROW_TILE = 256
Q_TILE = 256
GLA_CHUNK = 64
SCAN_T = 16


def _params(*sem):
    return pltpu.CompilerParams(dimension_semantics=sem, vmem_limit_bytes=VMEM_LIMIT_BYTES)


def _silu(x):
    return x * jax.nn.sigmoid(x)


def _dot(a, b):
    return jnp.dot(a, b, preferred_element_type=F32)


def _seg_sum(x, ones_bd):
    hi = x.astype(BF16)
    r1 = x - hi.astype(F32)
    mid = r1.astype(BF16)
    lo = (r1 - mid.astype(F32)).astype(BF16)
    return _dot(hi, ones_bd) + _dot(mid, ones_bd) + _dot(lo, ones_bd)


def _get(ref):
    return ref[...].reshape(ref.shape[-2:])


def _put(ref, val):
    ref[...] = val.reshape(ref.shape)


def _block_diag_ones(n, blk):
    i = np.arange(n) // blk
    return jnp.asarray((i[:, None] == i[None, :]).astype(np.float32), dtype=BF16)


def _mod_kernel(cond_ref, w_ref, b_ref, o_ref):
    s = _silu(cond_ref[...])
    o_ref[0] = _dot(s.astype(BF16), w_ref[0].astype(BF16)) + b_ref[0]


def _modulation(cond, mod_w, mod_b):
    n = cond.shape[0]
    return pl.pallas_call(
        _mod_kernel,
        grid=(DEPTH, 3),
        in_specs=[
            pl.BlockSpec((n, D_MODEL), lambda i, j: (0, 0)),
            pl.BlockSpec((1, D_MODEL, D_MODEL), lambda i, j: (i, 0, j)),
            pl.BlockSpec((1, 1, D_MODEL), lambda i, j: (i, 0, j)),
        ],
        out_specs=pl.BlockSpec((1, n, D_MODEL), lambda i, j: (i, 0, j)),
        out_shape=jax.ShapeDtypeStruct((DEPTH, n, 3 * D_MODEL), F32),
        compiler_params=_params("parallel", "parallel"),
    )(cond, mod_w, mod_b.reshape(DEPTH, 1, 3 * D_MODEL))


def _mod_row(latent):
    return (1 + pl.program_id(0)) if latent else 0


def _inproj_kernel(latent, splits, x_ref, g_ref, sh_ref, sc_ref, w_ref, *out_refs):
    r = _mod_row(latent)
    x = x_ref[0]
    y = x * lax.rsqrt(jnp.mean(x * x, axis=-1, keepdims=True) + EPS) * g_ref[...]
    shift = sh_ref[0, pl.ds(r, 1), :]
    scale = sc_ref[0, pl.ds(r, 1), :]
    h = (y * (1.0 + scale) + shift).astype(BF16)
    off = 0
    for o_ref, n in zip(out_refs, splits):
        o_ref[0] = _dot(h, w_ref[:, off:off + n]).astype(o_ref.dtype)
        off += n


def _inproj(x, norm_g, mod, layer, w_bf16, splits, latent, dtypes=None):
    B, T, _ = x.shape
    dtypes = dtypes or (F32,) * len(splits)
    cols = w_bf16.shape[1]
    nrow = mod.shape[1]
    return pl.pallas_call(
        functools.partial(_inproj_kernel, latent, splits),
        grid=(B, T // ROW_TILE),
        in_specs=[
            pl.BlockSpec((1, ROW_TILE, D_MODEL), lambda b, i: (b, i, 0)),
            pl.BlockSpec((1, D_MODEL), lambda b, i: (0, 0)),
            pl.BlockSpec((1, nrow, D_MODEL), lambda b, i: (layer, 0, 0)),
            pl.BlockSpec((1, nrow, D_MODEL), lambda b, i: (layer, 0, 1)),
            pl.BlockSpec((D_MODEL, cols), lambda b, i: (0, 0)),
        ],
        out_specs=[pl.BlockSpec((1, ROW_TILE, n), lambda b, i: (b, i, 0)) for n in splits],
        out_shape=[jax.ShapeDtypeStruct((B, T, n), dt) for n, dt in zip(splits, dtypes)],
        compiler_params=_params("parallel", "parallel"),
    )(x, norm_g.reshape(1, D_MODEL), mod, mod, w_bf16)


def _outproj_kernel(latent, final, n_in, *refs):
    o_refs = refs[:n_in]
    w_refs = refs[n_in:2 * n_in]
    x_ref, gate_ref = refs[2 * n_in], refs[2 * n_in + 1]
    rest = refs[2 * n_in + 2:]
    r = _mod_row(latent)
    acc = _dot(o_refs[0][0].astype(BF16), w_refs[0][...])
    for o_ref, w_ref in zip(o_refs[1:], w_refs[1:]):
        acc = acc + _dot(o_ref[0].astype(BF16), w_ref[...])
    y = x_ref[0] + gate_ref[0, pl.ds(r, 1), :] * acc
    if final:
        fg_ref, out_ref = rest
        y = y * lax.rsqrt(jnp.mean(y * y, axis=-1, keepdims=True) + EPS) * fg_ref[...]
    else:
        (out_ref,) = rest
    out_ref[0] = y


def _outproj(outs, ws_bf16, x, mod, layer, latent, final_g=None):
    B, T, _ = x.shape
    nrow = mod.shape[1]
    n_in = len(outs)
    final = final_g is not None
    in_specs = [pl.BlockSpec((1, ROW_TILE, o.shape[-1]), lambda b, i: (b, i, 0)) for o in outs]
    in_specs += [pl.BlockSpec(w.shape, lambda b, i: (0, 0)) for w in ws_bf16]
    in_specs += [
        pl.BlockSpec((1, ROW_TILE, D_MODEL), lambda b, i: (b, i, 0)),
        pl.BlockSpec((1, nrow, D_MODEL), lambda b, i: (layer, 0, 2)),
    ]
    args = list(outs) + list(ws_bf16) + [x, mod]
    if final:
        in_specs.append(pl.BlockSpec((1, D_MODEL), lambda b, i: (0, 0)))
        args.append(final_g.reshape(1, D_MODEL))
    return pl.pallas_call(
        functools.partial(_outproj_kernel, latent, final, n_in),
        grid=(B, T // ROW_TILE),
        in_specs=in_specs,
        out_specs=pl.BlockSpec((1, ROW_TILE, D_MODEL), lambda b, i: (b, i, 0)),
        out_shape=jax.ShapeDtypeStruct((B, T, D_MODEL), F32),
        compiler_params=_params("parallel", "parallel"),
    )(*args)


def _rope_tables(T):
    n_rows = T // GRID_W
    row = jnp.repeat(jnp.arange(n_rows), GRID_W).astype(F32)
    col = jnp.tile(jnp.arange(GRID_W), n_rows).astype(F32)
    n_freq = HD_A // 4
    inv = ROPE_BASE ** (-jnp.arange(n_freq, dtype=F32) / n_freq)
    ang_r = row[:, None] * inv
    ang_c = col[:, None] * inv
    zero = jnp.zeros_like(ang_r)
    cos = jnp.concatenate([jnp.cos(ang_r), jnp.cos(ang_r), jnp.cos(ang_c), jnp.cos(ang_c)], axis=1)
    s1 = jnp.concatenate([-jnp.sin(ang_r), zero, -jnp.sin(ang_c), zero], axis=1)
    s2 = jnp.concatenate([zero, jnp.sin(ang_r), zero, jnp.sin(ang_c)], axis=1)
    return cos, s1, s2


def _rope(x, cos, s1, s2):
    n = x.shape[-1]
    q = HD_A // 4
    return x * cos + pltpu.roll(x, n - q, 1) * s1 + pltpu.roll(x, q, 1) * s2


def _attn_kernel(latent, S, *refs):
    if latent:
        (q_ref, k_ref, v_ref, ga_ref, qg_ref, kg_ref, bd_ref, cos_ref, s1_ref, s2_ref,
         cosk_ref, s1k_ref, s2k_ref, ck_ref, cv_ref, o_ref, km_ref, vm_ref) = refs
    else:
        (q_ref, k_ref, v_ref, ga_ref, qg_ref, kg_ref, bd_ref, o_ref, kn_ref, km_ref, vm_ref) = refs
    bd = bd_ref[...]
    inv_d = 1.0 / HD_A
    kw = KV_A * HD_A

    @pl.when(pl.program_id(1) == 0)
    def _():
        k = k_ref[0]
        kn = k * lax.rsqrt(_seg_sum(k * k, bd[:kw, :kw]) * inv_d + EPS) * kg_ref[...]
        v = v_ref[0]
        if latent:
            kn = _rope(kn, cosk_ref[...], s1k_ref[...], s2k_ref[...])
            k_all = jnp.concatenate([ck_ref[0, 0], kn], axis=0)
            v_all = jnp.concatenate([cv_ref[0, 0], v], axis=0)
        else:
            kn_ref[0] = kn
            k_all, v_all = kn, v
        lane = lax.broadcasted_iota(jnp.int32, (S, kw), 1)
        k_sw = pltpu.roll(k_all, HD_A, 1)
        v_sw = pltpu.roll(v_all, HD_A, 1)
        for j in range(KV_A):
            for half in range(2):
                keep = (lane < HD_A) if half == 0 else (lane >= HD_A)
                src_k, src_v = (k_all, v_all) if j == half else (k_sw, v_sw)
                km_ref[2 * j + half] = jnp.where(keep, src_k, 0.0).astype(BF16)
                vm_ref[2 * j + half] = jnp.where(keep, src_v, 0.0).astype(BF16)

    q = q_ref[0]
    qn = q * lax.rsqrt(_seg_sum(q * q, bd) * inv_d + EPS) * qg_ref[...]
    if latent:
        qn = _rope(qn, cos_ref[...], s1_ref[...], s2_ref[...])
    qb = (qn * HD_A ** -0.5).astype(BF16)
    for m in range(H_A // 2):
        blk = slice(m * LANES, (m + 1) * LANES)
        qs = qb[:, blk]
        acc = None
        for half in range(2):
            j = (2 * m + half) // (H_A // KV_A)
            s = lax.dot_general(qs, km_ref[2 * j + half], (((1,), (1,)), ((), ())),
                                preferred_element_type=F32)
            e = jnp.exp(s - jnp.max(s, axis=-1, keepdims=True))
            l = jnp.sum(e, axis=-1, keepdims=True)
            pv = _dot(e.astype(BF16), vm_ref[2 * j + half]) * (1.0 / l)
            acc = pv if acc is None else acc + pv
        o_ref[0, :, blk] = (acc * _silu(ga_ref[0, :, blk])).astype(o_ref.dtype)


def _attention(qa, ka, va, ga, qn_g, kn_g, ctx_k=None, ctx_v=None, layer_j=0):
    B, T, _ = qa.shape
    latent = ctx_k is not None
    S = T + (ctx_k.shape[2] if latent else 0)
    kw = KV_A * HD_A
    qw = H_A * HD_A
    bd = _block_diag_ones(qw, HD_A)
    qblk = pl.BlockSpec((1, Q_TILE, qw), lambda b, i: (b, i, 0))
    kblk = pl.BlockSpec((1, T, kw), lambda b, i: (b, 0, 0))
    in_specs = [qblk, kblk, kblk, qblk,
                pl.BlockSpec((1, qw), lambda b, i: (0, 0)),
                pl.BlockSpec((1, kw), lambda b, i: (0, 0)),
                pl.BlockSpec(bd.shape, lambda b, i: (0, 0))]
    args = [qa, ka, va, ga, jnp.tile(qn_g, H_A).reshape(1, -1), jnp.tile(kn_g, KV_A).reshape(1, -1), bd]
    out_specs = [qblk]
    out_shape = [jax.ShapeDtypeStruct((B, T, qw), BF16)]
    if latent:
        tabs = _rope_tables(T)
        P = ctx_k.shape[2]
        in_specs += [pl.BlockSpec((Q_TILE, qw), lambda b, i: (i, 0))] * 3
        in_specs += [pl.BlockSpec((T, kw), lambda b, i: (0, 0))] * 3
        in_specs += [pl.BlockSpec((1, 1, P, kw), lambda b, i: (b, layer_j, 0, 0))] * 2
        args += [jnp.tile(t, (1, H_A)) for t in tabs] + [jnp.tile(t, (1, KV_A)) for t in tabs] + [ctx_k, ctx_v]
    else:
        out_specs.append(kblk)
        out_shape.append(jax.ShapeDtypeStruct((B, T, kw), F32))
    res = pl.pallas_call(
        functools.partial(_attn_kernel, latent, S),
        grid=(B, T // Q_TILE),
        in_specs=in_specs,
        out_specs=out_specs,
        out_shape=out_shape,
        scratch_shapes=[pltpu.VMEM((2 * KV_A, S, kw), BF16), pltpu.VMEM((2 * KV_A, S, kw), BF16)],
        compiler_params=_params("parallel", "arbitrary"),
    )(*args)
    return res if not latent else (res[0], None)


def _rwkv_prep_kernel(n_t, z_ref, zp_ref, zn_ref, mu_ref, w0_ref, a0_ref, w2_ref, a2_ref,
                      kkg_ref, ka_ref, rk_ref, bd_ref,
                      wf_ref, wb_ref, kdf_ref, kdb_ref, kaf_ref, kab_ref, nkk_ref, r_ref, v_ref, bonus_ref):
    i = pl.program_id(1)
    z = z_ref[0]
    n = z.shape[0]
    prev_row = jnp.where(i > 0, zp_ref[0, SUBLANES - 1:SUBLANES, :], 0.0)
    next_row = jnp.where(i < n_t - 1, zn_ref[0, 0:1, :], 0.0)
    rows = lax.broadcasted_iota(jnp.int32, (n, 1), 0)
    zp = jnp.where(rows == 0, prev_row, pltpu.roll(z, 1, 0))
    zn = jnp.where(rows == n - 1, next_row, pltpu.roll(z, n - 1, 0))
    zs = z + mu_ref[...] * (0.5 * (zp + zn) - z)

    rb = zs[:, 0:D_HALF]
    kb = zs[:, D_HALF:2 * D_HALF]
    vb = zs[:, 2 * D_HALF:3 * D_HALF]
    lw = zs[:, 3 * D_HALF:3 * D_HALF + 2 * W_LORA]
    la = zs[:, 3 * D_HALF + 2 * W_LORA:]
    w = jnp.exp(-RWKV_DECAY_SCALE * jax.nn.sigmoid(w0_ref[...] + _dot(jnp.tanh(lw).astype(BF16), w2_ref[...])))
    a = jax.nn.sigmoid(a0_ref[...] + _dot(la.astype(BF16), a2_ref[...]))
    bd = bd_ref[...]
    kk = kb * kkg_ref[...]
    kk = kk * lax.rsqrt(_seg_sum(kk * kk, bd) + 1e-12)
    ka = ka_ref[...]
    a_f, a_b = a[:, :D_HALF], a[:, D_HALF:]
    kd_f = kb * (1.0 + (a_f - 1.0) * ka)
    kd_b = kb * (1.0 + (a_b - 1.0) * ka)
    _put(wf_ref, w[:, :D_HALF])
    _put(wb_ref, w[:, D_HALF:])
    _put(kdf_ref, kd_f)
    _put(kdb_ref, kd_b)
    _put(kaf_ref, kk * a_f)
    _put(kab_ref, kk * a_b)
    _put(nkk_ref, -kk)
    _put(r_ref, rb)
    _put(v_ref, vb)
    _put(bonus_ref, _seg_sum(rb * rk_ref[...] * (kd_f + kd_b), bd) * vb)


def _block_diag2(m0, m1):
    z = jnp.zeros_like(m0)
    return jnp.concatenate([jnp.concatenate([m0, z], axis=1), jnp.concatenate([z, m1], axis=1)], axis=0)


def _rwkv_prep(zb, shift_mu, w0, w2, a0, a2, k_k, k_a, r_k):
    B, T, _ = zb.shape
    n_t = T // ROW_TILE
    per_tile = ROW_TILE // SUBLANES
    bd = _block_diag_ones(D_HALF, HS_B)
    row = lambda x: x.reshape(1, -1)
    vec = pl.BlockSpec((1, D_HALF), lambda b, i: (0, 0))
    vec2 = pl.BlockSpec((1, 2 * D_HALF), lambda b, i: (0, 0))
    out = pl.BlockSpec((1, ROW_TILE, D_HALF), lambda b, i: (b, i, 0))
    return pl.pallas_call(
        functools.partial(_rwkv_prep_kernel, n_t),
        grid=(B, n_t),
        in_specs=[
            pl.BlockSpec((1, ROW_TILE, B_SHIFT), lambda b, i: (b, i, 0)),
            pl.BlockSpec((1, SUBLANES, B_SHIFT), lambda b, i: (b, jnp.maximum(i * per_tile - 1, 0), 0)),
            pl.BlockSpec((1, SUBLANES, B_SHIFT),
                         lambda b, i: (b, jnp.minimum((i + 1) * per_tile, n_t * per_tile - 1), 0)),
            pl.BlockSpec((1, B_SHIFT), lambda b, i: (0, 0)),
            vec2, vec2,
            pl.BlockSpec((2 * W_LORA, 2 * D_HALF), lambda b, i: (0, 0)),
            pl.BlockSpec((2 * A_LORA, 2 * D_HALF), lambda b, i: (0, 0)),
            vec, vec, vec,
            pl.BlockSpec(bd.shape, lambda b, i: (0, 0)),
        ],
        out_specs=[out] * 10,
        out_shape=[jax.ShapeDtypeStruct((B, T, D_HALF), F32)] * 10,
        compiler_params=_params("parallel", "parallel"),
    )(zb, zb, zb, row(shift_mu), row(w0), row(a0),
      _block_diag2(w2[0], w2[1]).astype(BF16), _block_diag2(a2[0], a2[1]).astype(BF16),
      row(k_k), row(k_a), row(r_k), bd)


CHAIN_ROWS = 32
HEAD_PAIRS = H_B // 2


def _step_rows(ref, t, vs):
    if vs == 1:
        return ref[:, t, :]
    return jnp.concatenate([jnp.broadcast_to(ref[b, t:t + 1, :], (vs, ref.shape[2])) for b in range(ref.shape[0])],
                           axis=0)


def _to_chains(x):
    return jnp.concatenate([x[:, j * LANES:(j + 1) * LANES] for j in range(HEAD_PAIRS)], axis=0).T


def _rwkv_scan_kernel(reverse, vs, w_ref, nkk_ref, kka_ref, kd_ref, r_ref, v_ref, s0_ref, y_ref, s_ref,
                      vbuf, ybuf, *kbuf):
    vh = HS_B // vs

    @pl.when(pl.program_id(0) == 0)
    def _():
        s_ref[...] = s0_ref[...]

    split = lax.broadcasted_iota(jnp.int32, (vh, LANES), 1) % vs
    steps = range(SCAN_T - 1, -1, -1) if reverse else range(SCAN_T)

    def relayout(t):
        return tuple(_to_chains(_step_rows(ref, t, vs)) for ref in (w_ref, nkk_ref, kka_ref, kd_ref, r_ref, v_ref))

    def emit_y(t):
        halves = []
        for par in range(2):
            y = ybuf[t if kbuf else 0, par]
            if vs > 1:
                y = jnp.concatenate([jnp.where(split == q, y, 0.0) for q in range(vs)], axis=0)
            halves.append(y)
        yt = jnp.concatenate(halves, axis=0).T
        for j in range(HEAD_PAIRS):
            blk = yt[j * CHAIN_ROWS:(j + 1) * CHAIN_ROWS]
            if vs > 1:
                blk = jnp.sum(blk.reshape(CHAIN_ROWS // vs, vs, LANES), axis=1)
            y_ref[:, t, j * LANES:(j + 1) * LANES] = blk

    if kbuf:
        for t in steps:
            for a, x in enumerate(relayout(t)):
                kbuf[0][a, t] = x

    for t in steps:
        w, nkk, kka, kd, r, v_all = (kbuf[0][a, t] for a in range(6)) if kbuf else relayout(t)
        yslot = t if kbuf else 0
        for par in range(2):
            rows = slice(par * HS_B, (par + 1) * HS_B)
            lanes = slice(par * LANES, (par + 1) * LANES)
            v_par = v_all[rows]
            v_own = v_par[:vh]
            for q in range(1, vs):
                v_own = jnp.where(split == q, v_par[q * vh:(q + 1) * vh], v_own)
            vbuf[par] = v_own
            kvec = (w[rows], nkk[rows], kka[rows], kd[rows], r[rows])

            def v_step(g, c, par=par, lanes=lanes, kvec=kvec, yslot=yslot):
                w_, nkk_, kka_, kd_, r_ = kvec
                base = pl.multiple_of(g * SUBLANES, SUBLANES)
                vblk = vbuf[par, pl.ds(base, SUBLANES), :]
                ys = []
                for u in range(SUBLANES):
                    s = s_ref[base + u, :, lanes]
                    sa = jnp.sum(s * nkk_, axis=0, keepdims=True)
                    s = s * w_ + sa * kka_ + vblk[u:u + 1, :] * kd_
                    s_ref[base + u, :, lanes] = s
                    ys.append(jnp.sum(s * r_, axis=0, keepdims=True))
                ybuf[yslot, par, pl.ds(base, SUBLANES), :] = jnp.concatenate(ys, axis=0)
                return c

            lax.fori_loop(0, vh // SUBLANES, v_step, 0)
        if not kbuf:
            emit_y(t)

    if kbuf:
        for t in steps:
            emit_y(t)


def _rwkv_scan(w, nkk, kka, kd, r, v, s0, reverse):
    B, T, W = w.shape
    vs = CHAIN_ROWS // B
    vh = HS_B // vs
    n_t = T // SCAN_T
    hoist = vh > SUBLANES
    tblk = (lambda t: n_t - 1 - t) if reverse else (lambda t: t)
    vec = pl.BlockSpec((B, SCAN_T, W), lambda t: (0, tblk(t), 0))
    state = pl.BlockSpec((vh, HS_B, 2 * LANES), lambda t: (0, 0, 0))
    return pl.pallas_call(
        functools.partial(_rwkv_scan_kernel, reverse, vs),
        grid=(n_t,),
        in_specs=[vec] * 6 + [state],
        out_specs=[vec, state],
        out_shape=[jax.ShapeDtypeStruct((B, T, W), F32), jax.ShapeDtypeStruct((vh, HS_B, 2 * LANES), F32)],
        scratch_shapes=[pltpu.VMEM((2, vh, LANES), F32), pltpu.VMEM((SCAN_T if hoist else 1, 2, vh, LANES), F32)]
        + ([pltpu.VMEM((6, SCAN_T, LANES, LANES), F32)] if hoist else []),
        compiler_params=_params("arbitrary"),
    )(w, nkk, kka, kd, r, v, s0)


def _rwkv_post_kernel(yf_ref, yb_ref, bonus_ref, gb_ref, lng_ref, lnb_ref, bd_ref, o_ref):
    bd = bd_ref[...]
    y = _get(yf_ref) + _get(yb_ref)
    inv_n = 1.0 / HS_B
    d = y - _seg_sum(y, bd) * inv_n
    var = _seg_sum(d * d, bd) * inv_n
    yn = d * lax.rsqrt(var + GN_EPS) * lng_ref[...] + lnb_ref[...]
    o_ref[0] = ((yn + bonus_ref[0]) * _silu(gb_ref[0])).astype(o_ref.dtype)


def _rwkv_post(y_f, y_b, bonus, gb, ln_g, ln_b):
    B, T, _ = y_f.shape
    bd = _block_diag_ones(D_HALF, HS_B)
    blk = pl.BlockSpec((1, ROW_TILE, D_HALF), lambda b, i: (b, i, 0))
    vec = pl.BlockSpec((1, D_HALF), lambda b, i: (0, 0))
    return pl.pallas_call(
        _rwkv_post_kernel,
        grid=(B, T // ROW_TILE),
        in_specs=[blk, blk, blk, blk, vec, vec, pl.BlockSpec(bd.shape, lambda b, i: (0, 0))],
        out_specs=blk,
        out_shape=jax.ShapeDtypeStruct((B, T, D_HALF), BF16),
        compiler_params=_params("parallel", "parallel"),
    )(y_f, y_b, bonus, gb, ln_g.reshape(1, -1), ln_b.reshape(1, -1), bd)


def _rwkv_mixer(zb, gb, p, s_init):
    B, T, _ = zb.shape
    (w_f, w_b, kd_f, kd_b, ka_f, ka_b, nkk, r, v, bonus) = _rwkv_prep(
        zb, p["shift_mu"], p["w0"], p["w2"], p["a0"], p["a2"], p["k_k"], p["k_a"], p["r_k"])
    vs = CHAIN_ROWS // B
    vh = HS_B // vs
    chains = H_B * B * vs
    ys, fin = [], []
    for d, (w_d, ka_d, kd_d) in enumerate(((w_f, ka_f, kd_f), (w_b, ka_b, kd_b))):
        if s_init is None:
            s0 = jnp.zeros((vh, HS_B, chains), F32)
        else:
            s0 = s_init[d].reshape(B, HEAD_PAIRS, 2, vs, vh, HS_B).transpose(4, 5, 2, 1, 0, 3)
            s0 = s0.reshape(vh, HS_B, chains)
        y, s_fin = _rwkv_scan(w_d, nkk, ka_d, kd_d, r, v, s0, reverse=(d == 1))
        ys.append(y)
        fin.append(s_fin.reshape(vh, HS_B, 2, HEAD_PAIRS, B, vs).transpose(4, 3, 2, 5, 0, 1)
                   .reshape(B, H_B, HS_B, HS_B))
    return _rwkv_post(ys[0], ys[1], bonus, gb, p["ln_g"], p["ln_b"]), fin


def _chunk_cumsum(x, chunk, suffix):
    T = x.shape[0]
    pos = lax.broadcasted_iota(jnp.int32, (T, 1), 0) % chunk
    step = 1
    while step < chunk:
        if suffix:
            x = x + jnp.where(pos < chunk - step, pltpu.roll(x, T - step, 0), 0.0)
        else:
            x = x + jnp.where(pos >= step, pltpu.roll(x, step, 0), 0.0)
        step *= 2
    return x


def _loop(n, body, static):
    if static:
        for i in range(n):
            body(i)
    else:
        lax.fori_loop(0, n, lambda i, c: (body(i), c)[1], 0, unroll=4)


def _gla_kernel(T, hp, q_ref, k_ref, v_ref, g_ref, gl_ref, w2_ref, gb_ref, lng_ref, tri_ref, s0f_ref, s0b_ref,
                *rest):
    o_ref, sf_ref, sb_ref, b_ref, acc_ref, qb_ref, dec_ref, u_ref, sst_ref, st_ref = rest[-10:]
    C = GLA_CHUNK
    n_c = T // C
    static = n_c <= 4
    qscale = DK_C ** -0.5
    contract_last = (((1,), (1,)), ((), ()))
    contract_first = (((0,), (0,)), ((), ()))
    vis_f = tri_ref[0]
    vis_b = tri_ref[1]
    heads = range(hp)
    klanes = lambda h: slice(h * DK_C, (h + 1) * DK_C)
    vlanes = lambda h: slice(h * DV_C, (h + 1) * DV_C)

    for h in heads:
        pre = _dot(gl_ref[0].astype(BF16), w2_ref[h]) + gb_ref[h]
        la = jax.nn.log_sigmoid(pre) * (1.0 / GLA_TAU)
        b_ref[h, :, :DK_C] = _chunk_cumsum(la[:, :DK_C], C, suffix=False)
        b_ref[h, :, DK_C:] = _chunk_cumsum(la[:, DK_C:], C, suffix=True)
        st_ref[h, 0] = s0f_ref[0, 0, h].T
        st_ref[h, 1] = s0b_ref[0, 0, h].T

    def chunk_rows(c):
        return pl.ds(c * C, C) if static else pl.ds(pl.multiple_of(c * C, C), C)

    def dec_rows(c, n=SUBLANES):
        return pl.ds(c * SUBLANES, n) if static else pl.ds(pl.multiple_of(c * SUBLANES, SUBLANES), n)

    def intra(c):
        rows = chunk_rows(c)
        for h in heads:
            q = q_ref[0, rows, klanes(h)] * qscale
            k = k_ref[0, rows, klanes(h)]
            vc = v_ref[0, rows, vlanes(h)].astype(BF16)
            b = b_ref[h, rows, :]
            btot_f = b[C - 1:C, :DK_C]
            btot_b = b[0:1, DK_C:]
            qe, ke, qb, kl = [], [], [], []
            for bd, btot in ((b[:, :DK_C], btot_f), (b[:, DK_C:], btot_b)):
                mref = 0.5 * btot
                e_half = jnp.exp(mref)
                q_up = q * jnp.exp(bd - mref)
                k_dn = k * jnp.exp(mref - bd)
                qe.append(q_up.astype(BF16))
                ke.append(k_dn.astype(BF16))
                qb.append((q_up * e_half).astype(BF16))
                kl.append((k_dn * e_half).astype(BF16))
            sc = lax.dot_general(jnp.concatenate(qe, axis=0), jnp.concatenate(ke, axis=0), contract_last,
                                 preferred_element_type=F32)
            att = sc[:C] * vis_f + pltpu.roll(sc[C:], C, 1) * vis_b
            acc_ref[rows, vlanes(h)] = _dot(att[:, :C].astype(BF16), vc)
            u_ref[h, c] = lax.dot_general(vc, jnp.concatenate(kl, axis=1), contract_first,
                                          preferred_element_type=F32)
            qb_ref[h, rows, :] = jnp.concatenate(qb, axis=1)
            dec_ref[h, dec_rows(c), :] = jnp.broadcast_to(
                jnp.exp(jnp.concatenate([btot_f, btot_b], axis=1)), (SUBLANES, 2 * DK_C))

    _loop(n_c, intra, static)

    def states(i):
        for h in heads:
            for d, c in ((0, i), (1, n_c - 1 - i)):
                lanes = slice(d * DK_C, (d + 1) * DK_C)
                st = st_ref[h, d]
                sst_ref[h, c, :, lanes] = st.astype(BF16)
                st_ref[h, d] = st * dec_ref[h, dec_rows(c, 1), lanes] + u_ref[h, c, :, lanes]

    _loop(n_c, states, static)

    def inter(c):
        rows = chunk_rows(c)
        for h in heads:
            acc_ref[rows, vlanes(h)] = acc_ref[rows, vlanes(h)] + lax.dot_general(
                qb_ref[h, rows, :], sst_ref[h, c], contract_last, preferred_element_type=F32)

    _loop(n_c, inter, static)

    for h in heads:
        sf_ref[0, 0, h] = st_ref[h, 0].T
        sb_ref[0, 0, h] = st_ref[h, 1].T
        o = acc_ref[:, vlanes(h)]
        o = o * lax.rsqrt(jnp.mean(o * o, axis=-1, keepdims=True) + EPS) * lng_ref[...]
        o_ref[0, :, vlanes(h)] = (o * _silu(g_ref[0, :, vlanes(h)])).astype(o_ref.dtype)


def _gla_mixer(q, k, v, g, gl, gw2, gbias, ln_g, s_init, layer_j, prev_fin=None):
    B, T, _ = q.shape
    C = GLA_CHUNK
    n_c = T // C
    hp = H_C if n_c <= 4 else 2
    idx = np.arange(C)
    tri = np.zeros((2, C, LANES), np.float32)
    tri[0, :, :C] = idx[:, None] >= idx[None, :]
    tri[1, :, :C] = idx[:, None] <= idx[None, :]
    tri = jnp.asarray(tri)
    w2 = jnp.stack([_block_diag2(gw2[0][:, h * DK_C:(h + 1) * DK_C], gw2[1][:, h * DK_C:(h + 1) * DK_C])
                    for h in range(H_C)]).astype(BF16)
    gb = jnp.stack([jnp.concatenate([gbias[0][h * DK_C:(h + 1) * DK_C], gbias[1][h * DK_C:(h + 1) * DK_C]])
                    for h in range(H_C)]).reshape(H_C, 1, 2 * DK_C)
    if s_init is None:
        s0f = s0b = jnp.zeros((1, 1, hp, DK_C, DV_C), F32)
        s_spec = pl.BlockSpec((1, 1, hp, DK_C, DV_C), lambda b, h: (0, 0, 0, 0, 0))
    else:
        s0f, s0b = s_init
        s_spec = pl.BlockSpec((1, 1, hp, DK_C, DV_C), lambda b, h: (b, layer_j, h, 0, 0))
    n_odd = DEPTH // 2
    st_shape = (B, n_odd, H_C, DK_C, DV_C)
    st_out = pl.BlockSpec((1, 1, hp, DK_C, DV_C), lambda b, h: (b, layer_j, h, 0, 0))
    args = [q, k, v, g, gl, w2, gb, ln_g.reshape(1, -1), tri, s0f, s0b]
    extra_specs, aliases = [], {}
    if prev_fin is not None:
        extra_specs = [pl.BlockSpec(memory_space=pl.ANY)] * 2
        aliases = {len(args): 1, len(args) + 1: 2}
        args += list(prev_fin)
    o, sf, sb = pl.pallas_call(
        functools.partial(_gla_kernel, T, hp),
        grid=(B, H_C // hp),
        input_output_aliases=aliases,
        in_specs=[
            pl.BlockSpec((1, T, hp * DK_C), lambda b, h: (b, 0, h)),
            pl.BlockSpec((1, T, hp * DK_C), lambda b, h: (b, 0, h)),
            pl.BlockSpec((1, T, hp * DV_C), lambda b, h: (b, 0, h)),
            pl.BlockSpec((1, T, hp * DV_C), lambda b, h: (b, 0, h)),
            pl.BlockSpec((1, T, 2 * G_LORA), lambda b, h: (b, 0, 0)),
            pl.BlockSpec((hp, 2 * G_LORA, 2 * DK_C), lambda b, h: (h, 0, 0)),
            pl.BlockSpec((hp, 1, 2 * DK_C), lambda b, h: (h, 0, 0)),
            pl.BlockSpec((1, DV_C), lambda b, h: (0, 0)),
            pl.BlockSpec((2, C, LANES), lambda b, h: (0, 0, 0)),
            s_spec, s_spec,
        ] + extra_specs,
        out_specs=[pl.BlockSpec((1, T, hp * DV_C), lambda b, h: (b, 0, h)), st_out, st_out],
        out_shape=[
            jax.ShapeDtypeStruct((B, T, H_C * DV_C), BF16),
            jax.ShapeDtypeStruct(st_shape, F32),
            jax.ShapeDtypeStruct(st_shape, F32),
        ],
        scratch_shapes=[
            pltpu.VMEM((hp, T, 2 * DK_C), F32), pltpu.VMEM((T, hp * DV_C), F32),
            pltpu.VMEM((hp, T, 2 * DK_C), BF16), pltpu.VMEM((hp, n_c * SUBLANES, 2 * DK_C), F32),
            pltpu.VMEM((hp, n_c, DV_C, 2 * DK_C), F32), pltpu.VMEM((hp, n_c, DV_C, 2 * DK_C), BF16),
            pltpu.VMEM((hp, 2, DV_C, DK_C), F32),
        ],
        compiler_params=_params("parallel", "parallel"),
    )(*args)
    return o, (sf, sb)


def kernel(x_prompt, x_sample, c, cache_attn_k, cache_attn_v, state_rwkv_fwd, state_rwkv_bwd, state_gla_fwd, state_gla_bwd, c_ctx, norm_g, mod_w, mod_b, ev_w_in, ev_w_out, ev_qn_g, ev_kn_g, ev_shift_mu, rw_w0, rw_w2, rw_a0, rw_a2, rw_kk, rw_ka, rw_rk, rw_ln_g, rw_ln_b, od_w_in, od_w_out, gla_w2, gla_b, gla_ln_g, final_g):
    n_dec = c.shape[0]
    cond = jnp.concatenate([c_ctx[None], c, jnp.zeros((SUBLANES - 1 - n_dec, D_MODEL), F32)], axis=0)
    mod = _modulation(cond, mod_w, mod_b)

    ev_in = ev_w_in.astype(BF16)
    ev_out = ev_w_out.astype(BF16)
    od_in = od_w_in.astype(BF16)
    od_out = od_w_out.astype(BF16)
    kw = KV_A * HD_A
    ck = cache_attn_k.reshape(cache_attn_k.shape[:3] + (kw,))
    cv = cache_attn_v.reshape(cache_attn_v.shape[:3] + (kw,))

    def trunk(x, latent):
        new = {"k": [], "v": [], "rf": [], "rb": []}
        gla_fin = None
        for i in range(DEPTH):
            j = i // 2
            fg = final_g if i == DEPTH - 1 else None
            if i % 2 == 0:
                qa, ka, va, ga, zb, gb = _inproj(x, norm_g[i], mod, i, ev_in[j], EV_SPLITS, latent)
                if latent:
                    o_a, _ = _attention(qa, ka, va, ga, ev_qn_g[j], ev_kn_g[j], ck, cv, j)
                    s_init = (state_rwkv_fwd[:, j], state_rwkv_bwd[:, j])
                else:
                    o_a, kn = _attention(qa, ka, va, ga, ev_qn_g[j], ev_kn_g[j])
                    s_init = None
                    new["k"].append(kn)
                    new["v"].append(va)
                p = dict(shift_mu=ev_shift_mu[j], w0=rw_w0[j], w2=rw_w2[j], a0=rw_a0[j], a2=rw_a2[j],
                         k_k=rw_kk[j], k_a=rw_ka[j], r_k=rw_rk[j], ln_g=rw_ln_g[j], ln_b=rw_ln_b[j])
                o_b, fin = _rwkv_mixer(zb, gb, p, s_init)
                new["rf"].append(fin[0])
                new["rb"].append(fin[1])
                x = _outproj([o_a, o_b], [ev_out[j, :D_HALF], ev_out[j, D_HALF:]], x, mod, i, latent, fg)
            else:
                q, k, v, g, gl = _inproj(x, norm_g[i], mod, i, od_in[j], OD_SPLITS, latent,
                                         dtypes=(F32, F32, BF16, F32, F32))
                s_init = (state_gla_fwd, state_gla_bwd) if latent else None
                o, gla_fin = _gla_mixer(q, k, v, g, gl, gla_w2[j], gla_b[j], gla_ln_g[j], s_init, j, gla_fin)
                x = _outproj([o], [od_out[j]], x, mod, i, latent, fg)
        new["gf"], new["gb"] = gla_fin
        return x, new

    y_prompt, new = trunk(x_prompt, False)
    y_sample, _ = trunk(x_sample, True)
    B, T = x_prompt.shape[:2]
    heads = lambda t: t.reshape(B, T, KV_A, HD_A)
    return (y_prompt, y_sample,
            jnp.stack([heads(t) for t in new["k"]], axis=1), jnp.stack([heads(t) for t in new["v"]], axis=1),
            jnp.stack(new["rf"], axis=1), jnp.stack(new["rb"], axis=1),
            new["gf"], new["gb"])
```

```python
import functools

import numpy as np
import jax
import jax.numpy as jnp
from jax import lax
from jax.experimental import pallas as pl
from jax.experimental.pallas import tpu as pltpu

F32 = jnp.float32
BF16 = jnp.bfloat16

D_MODEL = 1024
DEPTH = 4
GRID_W = 64
D_HALF = D_MODEL // 2
HD_A = 64
H_A = D_HALF // HD_A
KV_A = H_A // 4
ROPE_BASE = 10000.0
HS_B = 64
H_B = D_HALF // HS_B
W_LORA = 64
A_LORA = 64
RWKV_DECAY_SCALE = 0.606531
GN_EPS = 64e-5
B_SHIFT = 3 * D_HALF + 2 * W_LORA + 2 * A_LORA
H_C = 4
DK_C = D_MODEL // 2 // H_C
DV_C = D_MODEL // H_C
G_LORA = 16
GLA_TAU = 16.0
EPS = 1e-6

EV_SPLITS = (H_A * HD_A, KV_A * HD_A, KV_A * HD_A, D_HALF, B_SHIFT, D_HALF)
OD_SPLITS = (H_C * DK_C, H_C * DK_C, D_MODEL, D_MODEL, 2 * G_LORA)

LANES = 128
SUBLANES = 8
VMEM_LIMIT_BYTES = 56 * 1024 * 1024

ROW_TILE = 256
Q_TILE = 256
GLA_CHUNK = 64
SCAN_T = 16


def _params(*sem):
    return pltpu.CompilerParams(dimension_semantics=sem, vmem_limit_bytes=VMEM_LIMIT_BYTES)


def _silu(x):
    return x * jax.nn.sigmoid(x)


def _dot(a, b):
    return jnp.dot(a, b, preferred_element_type=F32)


def _seg_sum(x, ones_bd):
    hi = x.astype(BF16)
    r1 = x - hi.astype(F32)
    mid = r1.astype(BF16)
    lo = (r1 - mid.astype(F32)).astype(BF16)
    return _dot(hi, ones_bd) + _dot(mid, ones_bd) + _dot(lo, ones_bd)


def _get(ref):
    return ref[...].reshape(ref.shape[-2:])


def _put(ref, val):
    ref[...] = val.reshape(ref.shape)


def _block_diag_ones(n, blk):
    i = np.arange(n) // blk
    return jnp.asarray((i[:, None] == i[None, :]).astype(np.float32), dtype=BF16)


def _mod_kernel(cond_ref, w_ref, b_ref, o_ref):
    s = _silu(cond_ref[...])
    o_ref[0] = _dot(s.astype(BF16), w_ref[0].astype(BF16)) + b_ref[0]


def _modulation(cond, mod_w, mod_b):
    n = cond.shape[0]
    return pl.pallas_call(
        _mod_kernel,
        grid=(DEPTH, 3),
        in_specs=[
            pl.BlockSpec((n, D_MODEL), lambda i, j: (0, 0)),
            pl.BlockSpec((1, D_MODEL, D_MODEL), lambda i, j: (i, 0, j)),
            pl.BlockSpec((1, 1, D_MODEL), lambda i, j: (i, 0, j)),
        ],
        out_specs=pl.BlockSpec((1, n, D_MODEL), lambda i, j: (i, 0, j)),
        out_shape=jax.ShapeDtypeStruct((DEPTH, n, 3 * D_MODEL), F32),
        compiler_params=_params("parallel", "parallel"),
    )(cond, mod_w, mod_b.reshape(DEPTH, 1, 3 * D_MODEL))


def _mod_row(latent):
    return (1 + pl.program_id(0)) if latent else 0


def _inproj_kernel(latent, splits, x_ref, g_ref, sh_ref, sc_ref, w_ref, *out_refs):
    r = _mod_row(latent)
    x = x_ref[0]
    y = x * lax.rsqrt(jnp.mean(x * x, axis=-1, keepdims=True) + EPS) * g_ref[...]
    shift = sh_ref[0, pl.ds(r, 1), :]
    scale = sc_ref[0, pl.ds(r, 1), :]
    h = (y * (1.0 + scale) + shift).astype(BF16)
    off = 0
    for o_ref, n in zip(out_refs, splits):
        o_ref[0] = _dot(h, w_ref[:, off:off + n]).astype(o_ref.dtype)
        off += n


def _inproj(x, norm_g, mod, layer, w_bf16, splits, latent, dtypes=None):
    B, T, _ = x.shape
    dtypes = dtypes or (F32,) * len(splits)
    cols = w_bf16.shape[1]
    nrow = mod.shape[1]
    return pl.pallas_call(
        functools.partial(_inproj_kernel, latent, splits),
        grid=(B, T // ROW_TILE),
        in_specs=[
            pl.BlockSpec((1, ROW_TILE, D_MODEL), lambda b, i: (b, i, 0)),
            pl.BlockSpec((1, D_MODEL), lambda b, i: (0, 0)),
            pl.BlockSpec((1, nrow, D_MODEL), lambda b, i: (layer, 0, 0)),
            pl.BlockSpec((1, nrow, D_MODEL), lambda b, i: (layer, 0, 1)),
            pl.BlockSpec((D_MODEL, cols), lambda b, i: (0, 0)),
        ],
        out_specs=[pl.BlockSpec((1, ROW_TILE, n), lambda b, i: (b, i, 0)) for n in splits],
        out_shape=[jax.ShapeDtypeStruct((B, T, n), dt) for n, dt in zip(splits, dtypes)],
        compiler_params=_params("parallel", "parallel"),
    )(x, norm_g.reshape(1, D_MODEL), mod, mod, w_bf16)


def _outproj_kernel(latent, final, n_in, *refs):
    o_refs = refs[:n_in]
    w_refs = refs[n_in:2 * n_in]
    x_ref, gate_ref = refs[2 * n_in], refs[2 * n_in + 1]
    rest = refs[2 * n_in + 2:]
    r = _mod_row(latent)
    acc = _dot(o_refs[0][0].astype(BF16), w_refs[0][...])
    for o_ref, w_ref in zip(o_refs[1:], w_refs[1:]):
        acc = acc + _dot(o_ref[0].astype(BF16), w_ref[...])
    y = x_ref[0] + gate_ref[0, pl.ds(r, 1), :] * acc
    if final:
        fg_ref, out_ref = rest
        y = y * lax.rsqrt(jnp.mean(y * y, axis=-1, keepdims=True) + EPS) * fg_ref[...]
    else:
        (out_ref,) = rest
    out_ref[0] = y


def _outproj(outs, ws_bf16, x, mod, layer, latent, final_g=None):
    B, T, _ = x.shape
    nrow = mod.shape[1]
    n_in = len(outs)
    final = final_g is not None
    in_specs = [pl.BlockSpec((1, ROW_TILE, o.shape[-1]), lambda b, i: (b, i, 0)) for o in outs]
    in_specs += [pl.BlockSpec(w.shape, lambda b, i: (0, 0)) for w in ws_bf16]
    in_specs += [
        pl.BlockSpec((1, ROW_TILE, D_MODEL), lambda b, i: (b, i, 0)),
        pl.BlockSpec((1, nrow, D_MODEL), lambda b, i: (layer, 0, 2)),
    ]
    args = list(outs) + list(ws_bf16) + [x, mod]
    if final:
        in_specs.append(pl.BlockSpec((1, D_MODEL), lambda b, i: (0, 0)))
        args.append(final_g.reshape(1, D_MODEL))
    return pl.pallas_call(
        functools.partial(_outproj_kernel, latent, final, n_in),
        grid=(B, T // ROW_TILE),
        in_specs=in_specs,
        out_specs=pl.BlockSpec((1, ROW_TILE, D_MODEL), lambda b, i: (b, i, 0)),
        out_shape=jax.ShapeDtypeStruct((B, T, D_MODEL), F32),
        compiler_params=_params("parallel", "parallel"),
    )(*args)


def _rope_tables(T):
    n_rows = T // GRID_W
    row = jnp.repeat(jnp.arange(n_rows), GRID_W).astype(F32)
    col = jnp.tile(jnp.arange(GRID_W), n_rows).astype(F32)
    n_freq = HD_A // 4
    inv = ROPE_BASE ** (-jnp.arange(n_freq, dtype=F32) / n_freq)
    ang_r = row[:, None] * inv
    ang_c = col[:, None] * inv
    zero = jnp.zeros_like(ang_r)
    cos = jnp.concatenate([jnp.cos(ang_r), jnp.cos(ang_r), jnp.cos(ang_c), jnp.cos(ang_c)], axis=1)
    s1 = jnp.concatenate([-jnp.sin(ang_r), zero, -jnp.sin(ang_c), zero], axis=1)
    s2 = jnp.concatenate([zero, jnp.sin(ang_r), zero, jnp.sin(ang_c)], axis=1)
    return cos, s1, s2


def _rope(x, cos, s1, s2):
    n = x.shape[-1]
    q = HD_A // 4
    return x * cos + pltpu.roll(x, n - q, 1) * s1 + pltpu.roll(x, q, 1) * s2


def _attn_kernel(latent, S, *refs):
    if latent:
        (q_ref, k_ref, v_ref, ga_ref, qg_ref, kg_ref, bd_ref, cos_ref, s1_ref, s2_ref,
         cosk_ref, s1k_ref, s2k_ref, ck_ref, cv_ref, o_ref, km_ref, vm_ref) = refs
    else:
        (q_ref, k_ref, v_ref, ga_ref, qg_ref, kg_ref, bd_ref, o_ref, kn_ref, km_ref, vm_ref) = refs
    bd = bd_ref[...]
    inv_d = 1.0 / HD_A
    kw = KV_A * HD_A

    @pl.when(pl.program_id(1) == 0)
    def _():
        k = k_ref[0]
        kn = k * lax.rsqrt(_seg_sum(k * k, bd[:kw, :kw]) * inv_d + EPS) * kg_ref[...]
        v = v_ref[0]
        if latent:
            kn = _rope(kn, cosk_ref[...], s1k_ref[...], s2k_ref[...])
            k_all = jnp.concatenate([ck_ref[0, 0], kn], axis=0)
            v_all = jnp.concatenate([cv_ref[0, 0], v], axis=0)
        else:
            kn_ref[0] = kn
            k_all, v_all = kn, v
        lane = lax.broadcasted_iota(jnp.int32, (S, kw), 1)
        k_sw = pltpu.roll(k_all, HD_A, 1)
        v_sw = pltpu.roll(v_all, HD_A, 1)
        for j in range(KV_A):
            for half in range(2):
                keep = (lane < HD_A) if half == 0 else (lane >= HD_A)
                src_k, src_v = (k_all, v_all) if j == half else (k_sw, v_sw)
                km_ref[2 * j + half] = jnp.where(keep, src_k, 0.0).astype(BF16)
                vm_ref[2 * j + half] = jnp.where(keep, src_v, 0.0).astype(BF16)

    q = q_ref[0]
    qn = q * lax.rsqrt(_seg_sum(q * q, bd) * inv_d + EPS) * qg_ref[...]
    if latent:
        qn = _rope(qn, cos_ref[...], s1_ref[...], s2_ref[...])
    qb = (qn * HD_A ** -0.5).astype(BF16)
    for m in range(H_A // 2):
        blk = slice(m * LANES, (m + 1) * LANES)
        qs = qb[:, blk]
        acc = None
        for half in range(2):
            j = (2 * m + half) // (H_A // KV_A)
            s = lax.dot_general(qs, km_ref[2 * j + half], (((1,), (1,)), ((), ())),
                                preferred_element_type=F32)
            e = jnp.exp(s - jnp.max(s, axis=-1, keepdims=True))
            l = jnp.sum(e, axis=-1, keepdims=True)
            pv = _dot(e.astype(BF16), vm_ref[2 * j + half]) * (1.0 / l)
            acc = pv if acc is None else acc + pv
        o_ref[0, :, blk] = (acc * _silu(ga_ref[0, :, blk])).astype(o_ref.dtype)


def _attention(qa, ka, va, ga, qn_g, kn_g, ctx_k=None, ctx_v=None, layer_j=0):
    B, T, _ = qa.shape
    latent = ctx_k is not None
    S = T + (ctx_k.shape[2] if latent else 0)
    kw = KV_A * HD_A
    qw = H_A * HD_A
    bd = _block_diag_ones(qw, HD_A)
    qblk = pl.BlockSpec((1, Q_TILE, qw), lambda b, i: (b, i, 0))
    kblk = pl.BlockSpec((1, T, kw), lambda b, i: (b, 0, 0))
    in_specs = [qblk, kblk, kblk, qblk,
                pl.BlockSpec((1, qw), lambda b, i: (0, 0)),
                pl.BlockSpec((1, kw), lambda b, i: (0, 0)),
                pl.BlockSpec(bd.shape, lambda b, i: (0, 0))]
    args = [qa, ka, va, ga, jnp.tile(qn_g, H_A).reshape(1, -1), jnp.tile(kn_g, KV_A).reshape(1, -1), bd]
    out_specs = [qblk]
    out_shape = [jax.ShapeDtypeStruct((B, T, qw), BF16)]
    if latent:
        tabs = _rope_tables(T)
        P = ctx_k.shape[2]
        in_specs += [pl.BlockSpec((Q_TILE, qw), lambda b, i: (i, 0))] * 3
        in_specs += [pl.BlockSpec((T, kw), lambda b, i: (0, 0))] * 3
        in_specs += [pl.BlockSpec((1, 1, P, kw), lambda b, i: (b, layer_j, 0, 0))] * 2
        args += [jnp.tile(t, (1, H_A)) for t in tabs] + [jnp.tile(t, (1, KV_A)) for t in tabs] + [ctx_k, ctx_v]
    else:
        out_specs.append(kblk)
        out_shape.append(jax.ShapeDtypeStruct((B, T, kw), F32))
    res = pl.pallas_call(
        functools.partial(_attn_kernel, latent, S),
        grid=(B, T // Q_TILE),
        in_specs=in_specs,
        out_specs=out_specs,
        out_shape=out_shape,
        scratch_shapes=[pltpu.VMEM((2 * KV_A, S, kw), BF16), pltpu.VMEM((2 * KV_A, S, kw), BF16)],
        compiler_params=_params("parallel", "arbitrary"),
    )(*args)
    return res if not latent else (res[0], None)


def _rwkv_prep_kernel(n_t, z_ref, zp_ref, zn_ref, mu_ref, w0_ref, a0_ref, w2_ref, a2_ref,
                      kkg_ref, ka_ref, rk_ref, bd_ref,
                      wf_ref, wb_ref, kdf_ref, kdb_ref, kaf_ref, kab_ref, nkk_ref, r_ref, v_ref, bonus_ref):
    i = pl.program_id(1)
    z = z_ref[0]
    n = z.shape[0]
    prev_row = jnp.where(i > 0, zp_ref[0, SUBLANES - 1:SUBLANES, :], 0.0)
    next_row = jnp.where(i < n_t - 1, zn_ref[0, 0:1, :], 0.0)
    rows = lax.broadcasted_iota(jnp.int32, (n, 1), 0)
    zp = jnp.where(rows == 0, prev_row, pltpu.roll(z, 1, 0))
    zn = jnp.where(rows == n - 1, next_row, pltpu.roll(z, n - 1, 0))
    zs = z + mu_ref[...] * (0.5 * (zp + zn) - z)

    rb = zs[:, 0:D_HALF]
    kb = zs[:, D_HALF:2 * D_HALF]
    vb = zs[:, 2 * D_HALF:3 * D_HALF]
    lw = zs[:, 3 * D_HALF:3 * D_HALF + 2 * W_LORA]
    la = zs[:, 3 * D_HALF + 2 * W_LORA:]
    w = jnp.exp(-RWKV_DECAY_SCALE * jax.nn.sigmoid(w0_ref[...] + _dot(jnp.tanh(lw).astype(BF16), w2_ref[...])))
    a = jax.nn.sigmoid(a0_ref[...] + _dot(la.astype(BF16), a2_ref[...]))
    bd = bd_ref[...]
    kk = kb * kkg_ref[...]
    kk = kk * lax.rsqrt(_seg_sum(kk * kk, bd) + 1e-12)
    ka = ka_ref[...]
    a_f, a_b = a[:, :D_HALF], a[:, D_HALF:]
    kd_f = kb * (1.0 + (a_f - 1.0) * ka)
    kd_b = kb * (1.0 + (a_b - 1.0) * ka)
    _put(wf_ref, w[:, :D_HALF])
    _put(wb_ref, w[:, D_HALF:])
    _put(kdf_ref, kd_f)
    _put(kdb_ref, kd_b)
    _put(kaf_ref, kk * a_f)
    _put(kab_ref, kk * a_b)
    _put(nkk_ref, -kk)
    _put(r_ref, rb)
    _put(v_ref, vb)
    _put(bonus_ref, _seg_sum(rb * rk_ref[...] * (kd_f + kd_b), bd) * vb)


def _block_diag2(m0, m1):
    z = jnp.zeros_like(m0)
    return jnp.concatenate([jnp.concatenate([m0, z], axis=1), jnp.concatenate([z, m1], axis=1)], axis=0)


def _rwkv_prep(zb, shift_mu, w0, w2, a0, a2, k_k, k_a, r_k):
    B, T, _ = zb.shape
    n_t = T // ROW_TILE
    per_tile = ROW_TILE // SUBLANES
    bd = _block_diag_ones(D_HALF, HS_B)
    row = lambda x: x.reshape(1, -1)
    vec = pl.BlockSpec((1, D_HALF), lambda b, i: (0, 0))
    vec2 = pl.BlockSpec((1, 2 * D_HALF), lambda b, i: (0, 0))
    out = pl.BlockSpec((1, ROW_TILE, D_HALF), lambda b, i: (b, i, 0))
    return pl.pallas_call(
        functools.partial(_rwkv_prep_kernel, n_t),
        grid=(B, n_t),
        in_specs=[
            pl.BlockSpec((1, ROW_TILE, B_SHIFT), lambda b, i: (b, i, 0)),
            pl.BlockSpec((1, SUBLANES, B_SHIFT), lambda b, i: (b, jnp.maximum(i * per_tile - 1, 0), 0)),
            pl.BlockSpec((1, SUBLANES, B_SHIFT),
                         lambda b, i: (b, jnp.minimum((i + 1) * per_tile, n_t * per_tile - 1), 0)),
            pl.BlockSpec((1, B_SHIFT), lambda b, i: (0, 0)),
            vec2, vec2,
            pl.BlockSpec((2 * W_LORA, 2 * D_HALF), lambda b, i: (0, 0)),
            pl.BlockSpec((2 * A_LORA, 2 * D_HALF), lambda b, i: (0, 0)),
            vec, vec, vec,
            pl.BlockSpec(bd.shape, lambda b, i: (0, 0)),
        ],
        out_specs=[out] * 10,
        out_shape=[jax.ShapeDtypeStruct((B, T, D_HALF), F32)] * 10,
        compiler_params=_params("parallel", "parallel"),
    )(zb, zb, zb, row(shift_mu), row(w0), row(a0),
      _block_diag2(w2[0], w2[1]).astype(BF16), _block_diag2(a2[0], a2[1]).astype(BF16),
      row(k_k), row(k_a), row(r_k), bd)


CHAIN_ROWS = 32
HEAD_PAIRS = H_B // 2


def _step_rows(ref, t, vs):
    if vs == 1:
        return ref[:, t, :]
    return jnp.concatenate([jnp.broadcast_to(ref[b, t:t + 1, :], (vs, ref.shape[2])) for b in range(ref.shape[0])],
                           axis=0)


def _to_chains(x):
    return jnp.concatenate([x[:, j * LANES:(j + 1) * LANES] for j in range(HEAD_PAIRS)], axis=0).T


def _rwkv_scan_kernel(reverse, vs, w_ref, nkk_ref, kka_ref, kd_ref, r_ref, v_ref, s0_ref, y_ref, s_ref,
                      vbuf, ybuf, *kbuf):
    vh = HS_B // vs

    @pl.when(pl.program_id(0) == 0)
    def _():
        s_ref[...] = s0_ref[...]

    split = lax.broadcasted_iota(jnp.int32, (vh, LANES), 1) % vs
    steps = range(SCAN_T - 1, -1, -1) if reverse else range(SCAN_T)

    def relayout(t):
        return tuple(_to_chains(_step_rows(ref, t, vs)) for ref in (w_ref, nkk_ref, kka_ref, kd_ref, r_ref, v_ref))

    def emit_y(t):
        halves = []
        for par in range(2):
            y = ybuf[t if kbuf else 0, par]
            if vs > 1:
                y = jnp.concatenate([jnp.where(split == q, y, 0.0) for q in range(vs)], axis=0)
            halves.append(y)
        yt = jnp.concatenate(halves, axis=0).T
        for j in range(HEAD_PAIRS):
            blk = yt[j * CHAIN_ROWS:(j + 1) * CHAIN_ROWS]
            if vs > 1:
                blk = jnp.sum(blk.reshape(CHAIN_ROWS // vs, vs, LANES), axis=1)
            y_ref[:, t, j * LANES:(j + 1) * LANES] = blk

    if kbuf:
        for t in steps:
            for a, x in enumerate(relayout(t)):
                kbuf[0][a, t] = x

    for t in steps:
        w, nkk, kka, kd, r, v_all = (kbuf[0][a, t] for a in range(6)) if kbuf else relayout(t)
        yslot = t if kbuf else 0
        for par in range(2):
            rows = slice(par * HS_B, (par + 1) * HS_B)
            lanes = slice(par * LANES, (par + 1) * LANES)
            v_par = v_all[rows]
            v_own = v_par[:vh]
            for q in range(1, vs):
                v_own = jnp.where(split == q, v_par[q * vh:(q + 1) * vh], v_own)
            vbuf[par] = v_own
            kvec = (w[rows], nkk[rows], kka[rows], kd[rows], r[rows])

            def v_step(g, c, par=par, lanes=lanes, kvec=kvec, yslot=yslot):
                w_, nkk_, kka_, kd_, r_ = kvec
                base = pl.multiple_of(g * SUBLANES, SUBLANES)
                vblk = vbuf[par, pl.ds(base, SUBLANES), :]
                ys = []
                for u in range(SUBLANES):
                    s = s_ref[base + u, :, lanes]
                    sa = jnp.sum(s * nkk_, axis=0, keepdims=True)
                    s = s * w_ + sa * kka_ + vblk[u:u + 1, :] * kd_
                    s_ref[base + u, :, lanes] = s
                    ys.append(jnp.sum(s * r_, axis=0, keepdims=True))
                ybuf[yslot, par, pl.ds(base, SUBLANES), :] = jnp.concatenate(ys, axis=0)
                return c

            lax.fori_loop(0, vh // SUBLANES, v_step, 0)
        if not kbuf:
            emit_y(t)

    if kbuf:
        for t in steps:
            emit_y(t)


def _rwkv_scan(w, nkk, kka, kd, r, v, s0, reverse):
    B, T, W = w.shape
    vs = CHAIN_ROWS // B
    vh = HS_B // vs
    n_t = T // SCAN_T
    hoist = vh > SUBLANES
    tblk = (lambda t: n_t - 1 - t) if reverse else (lambda t: t)
    vec = pl.BlockSpec((B, SCAN_T, W), lambda t: (0, tblk(t), 0))
    state = pl.BlockSpec((vh, HS_B, 2 * LANES), lambda t: (0, 0, 0))
    return pl.pallas_call(
        functools.partial(_rwkv_scan_kernel, reverse, vs),
        grid=(n_t,),
        in_specs=[vec] * 6 + [state],
        out_specs=[vec, state],
        out_shape=[jax.ShapeDtypeStruct((B, T, W), F32), jax.ShapeDtypeStruct((vh, HS_B, 2 * LANES), F32)],
        scratch_shapes=[pltpu.VMEM((2, vh, LANES), F32), pltpu.VMEM((SCAN_T if hoist else 1, 2, vh, LANES), F32)]
        + ([pltpu.VMEM((6, SCAN_T, LANES, LANES), F32)] if hoist else []),
        compiler_params=_params("arbitrary"),
    )(w, nkk, kka, kd, r, v, s0)


def _outproj_even_kernel(latent, oa_ref, yf_ref, yb_ref, bonus_ref, gb_ref, lng_ref, lnb_ref, bd_ref,
                         wa_ref, wb_ref, x_ref, gate_ref, out_ref):
    r = _mod_row(latent)
    bd = bd_ref[...]
    y = yf_ref[0] + yb_ref[0]
    inv_n = 1.0 / HS_B
    d = y - _seg_sum(y, bd) * inv_n
    var = _seg_sum(d * d, bd) * inv_n
    yn = d * lax.rsqrt(var + GN_EPS) * lng_ref[...] + lnb_ref[...]
    o_b = ((yn + bonus_ref[0]) * _silu(gb_ref[0])).astype(BF16)
    acc = _dot(oa_ref[0], wa_ref[...]) + _dot(o_b, wb_ref[...])
    out_ref[0] = x_ref[0] + gate_ref[0, pl.ds(r, 1), :] * acc


def _outproj_even(o_a, y_f, y_b, bonus, gb, ln_g, ln_b, w_a, w_b, x, mod, layer, latent):
    B, T, _ = x.shape
    nrow = mod.shape[1]
    bd = _block_diag_ones(D_HALF, HS_B)
    half = pl.BlockSpec((1, ROW_TILE, D_HALF), lambda b, i: (b, i, 0))
    full = pl.BlockSpec((1, ROW_TILE, D_MODEL), lambda b, i: (b, i, 0))
    vec = pl.BlockSpec((1, D_HALF), lambda b, i: (0, 0))
    const = lambda a: pl.BlockSpec(a.shape, lambda b, i: (0, 0))
    return pl.pallas_call(
        functools.partial(_outproj_even_kernel, latent),
        grid=(B, T // ROW_TILE),
        in_specs=[half, half, half, half, half, vec, vec, const(bd), const(w_a), const(w_b), full,
                  pl.BlockSpec((1, nrow, D_MODEL), lambda b, i: (layer, 0, 2))],
        out_specs=full,
        out_shape=jax.ShapeDtypeStruct((B, T, D_MODEL), F32),
        compiler_params=_params("parallel", "parallel"),
    )(o_a, y_f, y_b, bonus, gb, ln_g.reshape(1, -1), ln_b.reshape(1, -1), bd, w_a, w_b, x, mod)


def _rwkv_mixer(zb, p, s_init):
    B, T, _ = zb.shape
    (w_f, w_b, kd_f, kd_b, ka_f, ka_b, nkk, r, v, bonus) = _rwkv_prep(
        zb, p["shift_mu"], p["w0"], p["w2"], p["a0"], p["a2"], p["k_k"], p["k_a"], p["r_k"])
    vs = CHAIN_ROWS // B
    vh = HS_B // vs
    chains = H_B * B * vs
    ys, fin = [], []
    for d, (w_d, ka_d, kd_d) in enumerate(((w_f, ka_f, kd_f), (w_b, ka_b, kd_b))):
        if s_init is None:
            s0 = jnp.zeros((vh, HS_B, chains), F32)
        else:
            s0 = s_init[d].reshape(B, HEAD_PAIRS, 2, vs, vh, HS_B).transpose(4, 5, 2, 1, 0, 3)
            s0 = s0.reshape(vh, HS_B, chains)
        y, s_fin = _rwkv_scan(w_d, nkk, ka_d, kd_d, r, v, s0, reverse=(d == 1))
        ys.append(y)
        fin.append(s_fin.reshape(vh, HS_B, 2, HEAD_PAIRS, B, vs).transpose(4, 3, 2, 5, 0, 1)
                   .reshape(B, H_B, HS_B, HS_B))
    return (ys[0], ys[1], bonus), fin


def _chunk_cumsum(x, chunk, suffix):
    T = x.shape[0]
    pos = lax.broadcasted_iota(jnp.int32, (T, 1), 0) % chunk
    step = 1
    while step < chunk:
        if suffix:
            x = x + jnp.where(pos < chunk - step, pltpu.roll(x, T - step, 0), 0.0)
        else:
            x = x + jnp.where(pos >= step, pltpu.roll(x, step, 0), 0.0)
        step *= 2
    return x


def _loop(n, body, static):
    if static:
        for i in range(n):
            body(i)
    else:
        lax.fori_loop(0, n, lambda i, c: (body(i), c)[1], 0, unroll=4)


def _gla_kernel(T, hp, q_ref, k_ref, v_ref, g_ref, gl_ref, w2_ref, gb_ref, lng_ref, tri_ref, s0f_ref, s0b_ref,
                *rest):
    o_ref, sf_ref, sb_ref, b_ref, acc_ref, qb_ref, dec_ref, u_ref, sst_ref, st_ref = rest[-10:]
    C = GLA_CHUNK
    n_c = T // C
    static = n_c <= 4
    qscale = DK_C ** -0.5
    contract_last = (((1,), (1,)), ((), ()))
    contract_first = (((0,), (0,)), ((), ()))
    vis_f = tri_ref[0]
    vis_b = tri_ref[1]
    heads = range(hp)
    klanes = lambda h: slice(h * DK_C, (h + 1) * DK_C)
    vlanes = lambda h: slice(h * DV_C, (h + 1) * DV_C)

    for h in heads:
        pre = _dot(gl_ref[0].astype(BF16), w2_ref[h]) + gb_ref[h]
        la = jax.nn.log_sigmoid(pre) * (1.0 / GLA_TAU)
        b_ref[h, :, :DK_C] = _chunk_cumsum(la[:, :DK_C], C, suffix=False)
        b_ref[h, :, DK_C:] = _chunk_cumsum(la[:, DK_C:], C, suffix=True)
        st_ref[h, 0] = s0f_ref[0, 0, h].T
        st_ref[h, 1] = s0b_ref[0, 0, h].T

    def chunk_rows(c):
        return pl.ds(c * C, C) if static else pl.ds(pl.multiple_of(c * C, C), C)

    def dec_rows(c, n=SUBLANES):
        return pl.ds(c * SUBLANES, n) if static else pl.ds(pl.multiple_of(c * SUBLANES, SUBLANES), n)

    def intra(c):
        rows = chunk_rows(c)
        for h in heads:
            q = q_ref[0, rows, klanes(h)] * qscale
            k = k_ref[0, rows, klanes(h)]
            vc = v_ref[0, rows, vlanes(h)].astype(BF16)
            b = b_ref[h, rows, :]
            btot_f = b[C - 1:C, :DK_C]
            btot_b = b[0:1, DK_C:]
            qe, ke, qb, kl = [], [], [], []
            for bd, btot in ((b[:, :DK_C], btot_f), (b[:, DK_C:], btot_b)):
                mref = 0.5 * btot
                e_half = jnp.exp(mref)
                q_up = q * jnp.exp(bd - mref)
                k_dn = k * jnp.exp(mref - bd)
                qe.append(q_up.astype(BF16))
                ke.append(k_dn.astype(BF16))
                qb.append((q_up * e_half).astype(BF16))
                kl.append((k_dn * e_half).astype(BF16))
            sc = lax.dot_general(jnp.concatenate(qe, axis=0), jnp.concatenate(ke, axis=0), contract_last,
                                 preferred_element_type=F32)
            att = sc[:C] * vis_f + pltpu.roll(sc[C:], C, 1) * vis_b
            acc_ref[rows, vlanes(h)] = _dot(att[:, :C].astype(BF16), vc)
            u_ref[h, c] = lax.dot_general(vc, jnp.concatenate(kl, axis=1), contract_first,
                                          preferred_element_type=F32)
            qb_ref[h, rows, :] = jnp.concatenate(qb, axis=1)
            dec_ref[h, dec_rows(c), :] = jnp.broadcast_to(
                jnp.exp(jnp.concatenate([btot_f, btot_b], axis=1)), (SUBLANES, 2 * DK_C))

    _loop(n_c, intra, static)

    def states(i):
        for h in heads:
            for d, c in ((0, i), (1, n_c - 1 - i)):
                lanes = slice(d * DK_C, (d + 1) * DK_C)
                st = st_ref[h, d]
                sst_ref[h, c, :, lanes] = st.astype(BF16)
                st_ref[h, d] = st * dec_ref[h, dec_rows(c, 1), lanes] + u_ref[h, c, :, lanes]

    _loop(n_c, states, static)

    def inter(c):
        rows = chunk_rows(c)
        for h in heads:
            acc_ref[rows, vlanes(h)] = acc_ref[rows, vlanes(h)] + lax.dot_general(
                qb_ref[h, rows, :], sst_ref[h, c], contract_last, preferred_element_type=F32)

    _loop(n_c, inter, static)

    for h in heads:
        sf_ref[0, 0, h] = st_ref[h, 0].T
        sb_ref[0, 0, h] = st_ref[h, 1].T
        o = acc_ref[:, vlanes(h)]
        o = o * lax.rsqrt(jnp.mean(o * o, axis=-1, keepdims=True) + EPS) * lng_ref[...]
        o_ref[0, :, vlanes(h)] = (o * _silu(g_ref[0, :, vlanes(h)])).astype(o_ref.dtype)


def _gla_mixer(q, k, v, g, gl, gw2, gbias, ln_g, s_init, layer_j, prev_fin=None):
    B, T, _ = q.shape
    C = GLA_CHUNK
    n_c = T // C
    hp = H_C if n_c <= 4 else 2
    idx = np.arange(C)
    tri = np.zeros((2, C, LANES), np.float32)
    tri[0, :, :C] = idx[:, None] >= idx[None, :]
    tri[1, :, :C] = idx[:, None] <= idx[None, :]
    tri = jnp.asarray(tri)
    w2 = jnp.stack([_block_diag2(gw2[0][:, h * DK_C:(h + 1) * DK_C], gw2[1][:, h * DK_C:(h + 1) * DK_C])
                    for h in range(H_C)]).astype(BF16)
    gb = jnp.stack([jnp.concatenate([gbias[0][h * DK_C:(h + 1) * DK_C], gbias[1][h * DK_C:(h + 1) * DK_C]])
                    for h in range(H_C)]).reshape(H_C, 1, 2 * DK_C)
    if s_init is None:
        s0f = s0b = jnp.zeros((1, 1, hp, DK_C, DV_C), F32)
        s_spec = pl.BlockSpec((1, 1, hp, DK_C, DV_C), lambda b, h: (0, 0, 0, 0, 0))
    else:
        s0f, s0b = s_init
        s_spec = pl.BlockSpec((1, 1, hp, DK_C, DV_C), lambda b, h: (b, layer_j, h, 0, 0))
    n_odd = DEPTH // 2
    st_shape = (B, n_odd, H_C, DK_C, DV_C)
    st_out = pl.BlockSpec((1, 1, hp, DK_C, DV_C), lambda b, h: (b, layer_j, h, 0, 0))
    args = [q, k, v, g, gl, w2, gb, ln_g.reshape(1, -1), tri, s0f, s0b]
    if prev_fin is None:
        prev_fin = (jnp.zeros(st_shape, F32), jnp.zeros(st_shape, F32))
    extra_specs = [pl.BlockSpec(memory_space=pl.ANY)] * 2
    aliases = {len(args): 1, len(args) + 1: 2}
    args += list(prev_fin)
    o, sf, sb = pl.pallas_call(
        functools.partial(_gla_kernel, T, hp),
        grid=(B, H_C // hp),
        input_output_aliases=aliases,
        in_specs=[
            pl.BlockSpec((1, T, hp * DK_C), lambda b, h: (b, 0, h)),
            pl.BlockSpec((1, T, hp * DK_C), lambda b, h: (b, 0, h)),
            pl.BlockSpec((1, T, hp * DV_C), lambda b, h: (b, 0, h)),
            pl.BlockSpec((1, T, hp * DV_C), lambda b, h: (b, 0, h)),
            pl.BlockSpec((1, T, 2 * G_LORA), lambda b, h: (b, 0, 0)),
            pl.BlockSpec((hp, 2 * G_LORA, 2 * DK_C), lambda b, h: (h, 0, 0)),
            pl.BlockSpec((hp, 1, 2 * DK_C), lambda b, h: (h, 0, 0)),
            pl.BlockSpec((1, DV_C), lambda b, h: (0, 0)),
            pl.BlockSpec((2, C, LANES), lambda b, h: (0, 0, 0)),
            s_spec, s_spec,
        ] + extra_specs,
        out_specs=[pl.BlockSpec((1, T, hp * DV_C), lambda b, h: (b, 0, h)), st_out, st_out],
        out_shape=[
            jax.ShapeDtypeStruct((B, T, H_C * DV_C), BF16),
            jax.ShapeDtypeStruct(st_shape, F32),
            jax.ShapeDtypeStruct(st_shape, F32),
        ],
        scratch_shapes=[
            pltpu.VMEM((hp, T, 2 * DK_C), F32), pltpu.VMEM((T, hp * DV_C), F32),
            pltpu.VMEM((hp, T, 2 * DK_C), BF16), pltpu.VMEM((hp, n_c * SUBLANES, 2 * DK_C), F32),
            pltpu.VMEM((hp, n_c, DV_C, 2 * DK_C), F32), pltpu.VMEM((hp, n_c, DV_C, 2 * DK_C), BF16),
            pltpu.VMEM((hp, 2, DV_C, DK_C), F32),
        ],
        compiler_params=_params("parallel", "parallel"),
    )(*args)
    return o, (sf, sb)


def kernel(x_prompt, x_sample, c, cache_attn_k, cache_attn_v, state_rwkv_fwd, state_rwkv_bwd, state_gla_fwd, state_gla_bwd, c_ctx, norm_g, mod_w, mod_b, ev_w_in, ev_w_out, ev_qn_g, ev_kn_g, ev_shift_mu, rw_w0, rw_w2, rw_a0, rw_a2, rw_kk, rw_ka, rw_rk, rw_ln_g, rw_ln_b, od_w_in, od_w_out, gla_w2, gla_b, gla_ln_g, final_g):
    n_dec = c.shape[0]
    cond = jnp.concatenate([c_ctx[None], c, jnp.zeros((SUBLANES - 1 - n_dec, D_MODEL), F32)], axis=0)
    mod = _modulation(cond, mod_w, mod_b)

    ev_in = ev_w_in.astype(BF16)
    ev_out = ev_w_out.astype(BF16)
    od_in = od_w_in.astype(BF16)
    od_out = od_w_out.astype(BF16)
    kw = KV_A * HD_A
    ck = cache_attn_k.reshape(cache_attn_k.shape[:3] + (kw,))
    cv = cache_attn_v.reshape(cache_attn_v.shape[:3] + (kw,))

    def trunk(x, latent):
        new = {"k": [], "v": [], "rf": [], "rb": []}
        gla_fin = None
        for i in range(DEPTH):
            j = i // 2
            fg = final_g if i == DEPTH - 1 else None
            if i % 2 == 0:
                qa, ka, va, ga, zb, gb = _inproj(x, norm_g[i], mod, i, ev_in[j], EV_SPLITS, latent)
                if latent:
                    o_a, _ = _attention(qa, ka, va, ga, ev_qn_g[j], ev_kn_g[j], ck, cv, j)
                    s_init = (state_rwkv_fwd[:, j], state_rwkv_bwd[:, j])
                else:
                    o_a, kn = _attention(qa, ka, va, ga, ev_qn_g[j], ev_kn_g[j])
                    s_init = None
                    new["k"].append(kn)
                    new["v"].append(va)
                p = dict(shift_mu=ev_shift_mu[j], w0=rw_w0[j], w2=rw_w2[j], a0=rw_a0[j], a2=rw_a2[j],
                         k_k=rw_kk[j], k_a=rw_ka[j], r_k=rw_rk[j])
                (y_f, y_b, bonus), fin = _rwkv_mixer(zb, p, s_init)
                new["rf"].append(fin[0])
                new["rb"].append(fin[1])
                x = _outproj_even(o_a, y_f, y_b, bonus, gb, rw_ln_g[j], rw_ln_b[j],
                                  ev_out[j, :D_HALF], ev_out[j, D_HALF:], x, mod, i, latent)
            else:
                q, k, v, g, gl = _inproj(x, norm_g[i], mod, i, od_in[j], OD_SPLITS, latent,
                                         dtypes=(F32, F32, BF16, F32, F32))
                s_init = (state_gla_fwd, state_gla_bwd) if latent else None
                o, gla_fin = _gla_mixer(q, k, v, g, gl, gla_w2[j], gla_b[j], gla_ln_g[j], s_init, j, gla_fin)
                x = _outproj([o], [od_out[j]], x, mod, i, latent, fg)
        new["gf"], new["gb"] = gla_fin
        return x, new

    y_prompt, new = trunk(x_prompt, False)
    y_sample, _ = trunk(x_sample, True)
    B, T = x_prompt.shape[:2]
    heads = lambda t: t.reshape(B, T, KV_A, HD_A)
    return (y_prompt, y_sample,
            jnp.stack([heads(t) for t in new["k"]], axis=1), jnp.stack([heads(t) for t in new["v"]], axis=1),
            jnp.stack(new["rf"], axis=1), jnp.stack(new["rb"], axis=1),
            new["gf"], new["gb"])
```

```python
import functools

import numpy as np
import jax
import jax.numpy as jnp
from jax import lax
from jax.experimental import pallas as pl
from jax.experimental.pallas import tpu as pltpu

F32 = jnp.float32
BF16 = jnp.bfloat16

D_MODEL = 1024
DEPTH = 4
GRID_W = 64
D_HALF = D_MODEL // 2
HD_A = 64
H_A = D_HALF // HD_A
KV_A = H_A // 4
ROPE_BASE = 10000.0
HS_B = 64
H_B = D_HALF // HS_B
W_LORA = 64
A_LORA = 64
RWKV_DECAY_SCALE = 0.606531
GN_EPS = 64e-5
B_SHIFT = 3 * D_HALF + 2 * W_LORA + 2 * A_LORA
H_C = 4
DK_C = D_MODEL // 2 // H_C
DV_C = D_MODEL // H_C
G_LORA = 16
GLA_TAU = 16.0
EPS = 1e-6

EV_SPLITS = (H_A * HD_A, KV_A * HD_A, KV_A * HD_A, D_HALF, B_SHIFT, D_HALF)
OD_SPLITS = (H_C * DK_C, H_C * DK_C, D_MODEL, D_MODEL, 2 * G_LORA)

LANES = 128
SUBLANES = 8
VMEM_LIMIT_BYTES = 56 * 1024 * 1024

ROW_TILE = 256
Q_TILE = 256
GLA_CHUNK = 64
SCAN_T = 16


def _params(*sem):
    return pltpu.CompilerParams(dimension_semantics=sem, vmem_limit_bytes=VMEM_LIMIT_BYTES)


def _silu(x):
    return x * jax.nn.sigmoid(x)


def _dot(a, b):
    return jnp.dot(a, b, preferred_element_type=F32)


def _seg_sum(x, ones_bd):
    hi = x.astype(BF16)
    r1 = x - hi.astype(F32)
    mid = r1.astype(BF16)
    lo = (r1 - mid.astype(F32)).astype(BF16)
    return _dot(hi, ones_bd) + _dot(mid, ones_bd) + _dot(lo, ones_bd)


def _get(ref):
    return ref[...].reshape(ref.shape[-2:])


def _put(ref, val):
    ref[...] = val.reshape(ref.shape)


def _block_diag_ones(n, blk):
    i = np.arange(n) // blk
    return jnp.asarray((i[:, None] == i[None, :]).astype(np.float32), dtype=BF16)


def _mod_kernel(cond_ref, w_ref, b_ref, o_ref):
    s = _silu(cond_ref[...])
    o_ref[0] = _dot(s.astype(BF16), w_ref[0].astype(BF16)) + b_ref[0]


def _modulation(cond, mod_w, mod_b):
    n = cond.shape[0]
    return pl.pallas_call(
        _mod_kernel,
        grid=(DEPTH, 3),
        in_specs=[
            pl.BlockSpec((n, D_MODEL), lambda i, j: (0, 0)),
            pl.BlockSpec((1, D_MODEL, D_MODEL), lambda i, j: (i, 0, j)),
            pl.BlockSpec((1, 1, D_MODEL), lambda i, j: (i, 0, j)),
        ],
        out_specs=pl.BlockSpec((1, n, D_MODEL), lambda i, j: (i, 0, j)),
        out_shape=jax.ShapeDtypeStruct((DEPTH, n, 3 * D_MODEL), F32),
        compiler_params=_params("parallel", "parallel"),
    )(cond, mod_w, mod_b.reshape(DEPTH, 1, 3 * D_MODEL))


def _mod_row(latent):
    return (1 + pl.program_id(0)) if latent else 0


def _inproj_kernel(latent, splits, x_ref, g_ref, sh_ref, sc_ref, w_ref, *out_refs):
    r = _mod_row(latent)
    x = x_ref[0]
    y = x * lax.rsqrt(jnp.mean(x * x, axis=-1, keepdims=True) + EPS) * g_ref[...]
    shift = sh_ref[0, pl.ds(r, 1), :]
    scale = sc_ref[0, pl.ds(r, 1), :]
    h = (y * (1.0 + scale) + shift).astype(BF16)
    off = 0
    for o_ref, n in zip(out_refs, splits):
        o_ref[0] = _dot(h, w_ref[:, off:off + n]).astype(o_ref.dtype)
        off += n


def _inproj(x, norm_g, mod, layer, w_bf16, splits, latent, dtypes=None):
    B, T, _ = x.shape
    dtypes = dtypes or (F32,) * len(splits)
    cols = w_bf16.shape[1]
    nrow = mod.shape[1]
    return pl.pallas_call(
        functools.partial(_inproj_kernel, latent, splits),
        grid=(B, T // ROW_TILE),
        in_specs=[
            pl.BlockSpec((1, ROW_TILE, D_MODEL), lambda b, i: (b, i, 0)),
            pl.BlockSpec((1, D_MODEL), lambda b, i: (0, 0)),
            pl.BlockSpec((1, nrow, D_MODEL), lambda b, i: (layer, 0, 0)),
            pl.BlockSpec((1, nrow, D_MODEL), lambda b, i: (layer, 0, 1)),
            pl.BlockSpec((D_MODEL, cols), lambda b, i: (0, 0)),
        ],
        out_specs=[pl.BlockSpec((1, ROW_TILE, n), lambda b, i: (b, i, 0)) for n in splits],
        out_shape=[jax.ShapeDtypeStruct((B, T, n), dt) for n, dt in zip(splits, dtypes)],
        compiler_params=_params("parallel", "parallel"),
    )(x, norm_g.reshape(1, D_MODEL), mod, mod, w_bf16)


def _outproj_kernel(latent, final, n_in, *refs):
    o_refs = refs[:n_in]
    w_refs = refs[n_in:2 * n_in]
    x_ref, gate_ref = refs[2 * n_in], refs[2 * n_in + 1]
    rest = refs[2 * n_in + 2:]
    r = _mod_row(latent)
    acc = _dot(o_refs[0][0].astype(BF16), w_refs[0][...])
    for o_ref, w_ref in zip(o_refs[1:], w_refs[1:]):
        acc = acc + _dot(o_ref[0].astype(BF16), w_ref[...])
    y = x_ref[0] + gate_ref[0, pl.ds(r, 1), :] * acc
    if final:
        fg_ref, out_ref = rest
        y = y * lax.rsqrt(jnp.mean(y * y, axis=-1, keepdims=True) + EPS) * fg_ref[...]
    else:
        (out_ref,) = rest
    out_ref[0] = y


def _outproj(outs, ws_bf16, x, mod, layer, latent, final_g=None):
    B, T, _ = x.shape
    nrow = mod.shape[1]
    n_in = len(outs)
    final = final_g is not None
    in_specs = [pl.BlockSpec((1, ROW_TILE, o.shape[-1]), lambda b, i: (b, i, 0)) for o in outs]
    in_specs += [pl.BlockSpec(w.shape, lambda b, i: (0, 0)) for w in ws_bf16]
    in_specs += [
        pl.BlockSpec((1, ROW_TILE, D_MODEL), lambda b, i: (b, i, 0)),
        pl.BlockSpec((1, nrow, D_MODEL), lambda b, i: (layer, 0, 2)),
    ]
    args = list(outs) + list(ws_bf16) + [x, mod]
    if final:
        in_specs.append(pl.BlockSpec((1, D_MODEL), lambda b, i: (0, 0)))
        args.append(final_g.reshape(1, D_MODEL))
    return pl.pallas_call(
        functools.partial(_outproj_kernel, latent, final, n_in),
        grid=(B, T // ROW_TILE),
        in_specs=in_specs,
        out_specs=pl.BlockSpec((1, ROW_TILE, D_MODEL), lambda b, i: (b, i, 0)),
        out_shape=jax.ShapeDtypeStruct((B, T, D_MODEL), F32),
        compiler_params=_params("parallel", "parallel"),
    )(*args)


def _rope_tables(T):
    n_rows = T // GRID_W
    row = jnp.repeat(jnp.arange(n_rows), GRID_W).astype(F32)
    col = jnp.tile(jnp.arange(GRID_W), n_rows).astype(F32)
    n_freq = HD_A // 4
    inv = ROPE_BASE ** (-jnp.arange(n_freq, dtype=F32) / n_freq)
    ang_r = row[:, None] * inv
    ang_c = col[:, None] * inv
    zero = jnp.zeros_like(ang_r)
    cos = jnp.concatenate([jnp.cos(ang_r), jnp.cos(ang_r), jnp.cos(ang_c), jnp.cos(ang_c)], axis=1)
    s1 = jnp.concatenate([-jnp.sin(ang_r), zero, -jnp.sin(ang_c), zero], axis=1)
    s2 = jnp.concatenate([zero, jnp.sin(ang_r), zero, jnp.sin(ang_c)], axis=1)
    return cos, s1, s2


def _rope(x, cos, s1, s2):
    n = x.shape[-1]
    q = HD_A // 4
    return x * cos + pltpu.roll(x, n - q, 1) * s1 + pltpu.roll(x, q, 1) * s2


def _attn_kernel(latent, S, *refs):
    if latent:
        (q_ref, k_ref, v_ref, ga_ref, qg_ref, kg_ref, bd_ref, cos_ref, s1_ref, s2_ref,
         cosk_ref, s1k_ref, s2k_ref, ck_ref, cv_ref, o_ref, km_ref, vm_ref) = refs
    else:
        (q_ref, k_ref, v_ref, ga_ref, qg_ref, kg_ref, bd_ref, o_ref, kn_ref, km_ref, vm_ref) = refs
    bd = bd_ref[...]
    inv_d = 1.0 / HD_A
    kw = KV_A * HD_A

    @pl.when(pl.program_id(1) == 0)
    def _():
        k = k_ref[0]
        kn = k * lax.rsqrt(_seg_sum(k * k, bd[:kw, :kw]) * inv_d + EPS) * kg_ref[...]
        v = v_ref[0]
        if latent:
            kn = _rope(kn, cosk_ref[...], s1k_ref[...], s2k_ref[...])
            k_all = jnp.concatenate([ck_ref[0, 0], kn], axis=0)
            v_all = jnp.concatenate([cv_ref[0, 0], v], axis=0)
        else:
            kn_ref[0] = kn
            k_all, v_all = kn, v
        lane = lax.broadcasted_iota(jnp.int32, (S, kw), 1)
        k_sw = pltpu.roll(k_all, HD_A, 1)
        v_sw = pltpu.roll(v_all, HD_A, 1)
        for j in range(KV_A):
            for half in range(2):
                keep = (lane < HD_A) if half == 0 else (lane >= HD_A)
                src_k, src_v = (k_all, v_all) if j == half else (k_sw, v_sw)
                km_ref[2 * j + half] = jnp.where(keep, src_k, 0.0).astype(BF16)
                vm_ref[2 * j + half] = jnp.where(keep, src_v, 0.0).astype(BF16)

    q = q_ref[0]
    qn = q * lax.rsqrt(_seg_sum(q * q, bd) * inv_d + EPS) * qg_ref[...]
    if latent:
        qn = _rope(qn, cos_ref[...], s1_ref[...], s2_ref[...])
    qb = (qn * HD_A ** -0.5).astype(BF16)
    for m in range(H_A // 2):
        blk = slice(m * LANES, (m + 1) * LANES)
        qs = qb[:, blk]
        acc = None
        for half in range(2):
            j = (2 * m + half) // (H_A // KV_A)
            s = lax.dot_general(qs, km_ref[2 * j + half], (((1,), (1,)), ((), ())),
                                preferred_element_type=F32)
            e = jnp.exp(s - jnp.max(s, axis=-1, keepdims=True))
            l = jnp.sum(e, axis=-1, keepdims=True)
            pv = _dot(e.astype(BF16), vm_ref[2 * j + half]) * (1.0 / l)
            acc = pv if acc is None else acc + pv
        o_ref[0, :, blk] = (acc * _silu(ga_ref[0, :, blk])).astype(o_ref.dtype)


def _attention(qa, ka, va, ga, qn_g, kn_g, ctx_k=None, ctx_v=None, layer_j=0):
    B, T, _ = qa.shape
    latent = ctx_k is not None
    S = T + (ctx_k.shape[2] if latent else 0)
    kw = KV_A * HD_A
    qw = H_A * HD_A
    bd = _block_diag_ones(qw, HD_A)
    qblk = pl.BlockSpec((1, Q_TILE, qw), lambda b, i: (b, i, 0))
    kblk = pl.BlockSpec((1, T, kw), lambda b, i: (b, 0, 0))
    in_specs = [qblk, kblk, kblk, qblk,
                pl.BlockSpec((1, qw), lambda b, i: (0, 0)),
                pl.BlockSpec((1, kw), lambda b, i: (0, 0)),
                pl.BlockSpec(bd.shape, lambda b, i: (0, 0))]
    args = [qa, ka, va, ga, jnp.tile(qn_g, H_A).reshape(1, -1), jnp.tile(kn_g, KV_A).reshape(1, -1), bd]
    out_specs = [qblk]
    out_shape = [jax.ShapeDtypeStruct((B, T, qw), BF16)]
    if latent:
        tabs = _rope_tables(T)
        P = ctx_k.shape[2]
        in_specs += [pl.BlockSpec((Q_TILE, qw), lambda b, i: (i, 0))] * 3
        in_specs += [pl.BlockSpec((T, kw), lambda b, i: (0, 0))] * 3
        in_specs += [pl.BlockSpec((1, 1, P, kw), lambda b, i: (b, layer_j, 0, 0))] * 2
        args += [jnp.tile(t, (1, H_A)) for t in tabs] + [jnp.tile(t, (1, KV_A)) for t in tabs] + [ctx_k, ctx_v]
    else:
        out_specs.append(kblk)
        out_shape.append(jax.ShapeDtypeStruct((B, T, kw), F32))
    res = pl.pallas_call(
        functools.partial(_attn_kernel, latent, S),
        grid=(B, T // Q_TILE),
        in_specs=in_specs,
        out_specs=out_specs,
        out_shape=out_shape,
        scratch_shapes=[pltpu.VMEM((2 * KV_A, S, kw), BF16), pltpu.VMEM((2 * KV_A, S, kw), BF16)],
        compiler_params=_params("parallel", "arbitrary"),
    )(*args)
    return res if not latent else (res[0], None)


def _rwkv_prep_kernel(n_t, z_ref, zp_ref, zn_ref, mu_ref, w0_ref, a0_ref, w2_ref, a2_ref,
                      kkg_ref, ka_ref, rk_ref, bd_ref,
                      wf_ref, wb_ref, kdf_ref, kdb_ref, kaf_ref, kab_ref, nkk_ref, r_ref, v_ref, bonus_ref):
    i = pl.program_id(1)
    z = z_ref[0]
    n = z.shape[0]
    prev_row = jnp.where(i > 0, zp_ref[0, SUBLANES - 1:SUBLANES, :], 0.0)
    next_row = jnp.where(i < n_t - 1, zn_ref[0, 0:1, :], 0.0)
    rows = lax.broadcasted_iota(jnp.int32, (n, 1), 0)
    zp = jnp.where(rows == 0, prev_row, pltpu.roll(z, 1, 0))
    zn = jnp.where(rows == n - 1, next_row, pltpu.roll(z, n - 1, 0))
    zs = z + mu_ref[...] * (0.5 * (zp + zn) - z)

    rb = zs[:, 0:D_HALF]
    kb = zs[:, D_HALF:2 * D_HALF]
    vb = zs[:, 2 * D_HALF:3 * D_HALF]
    lw = zs[:, 3 * D_HALF:3 * D_HALF + 2 * W_LORA]
    la = zs[:, 3 * D_HALF + 2 * W_LORA:]
    w = jnp.exp(-RWKV_DECAY_SCALE * jax.nn.sigmoid(w0_ref[...] + _dot(jnp.tanh(lw).astype(BF16), w2_ref[...])))
    a = jax.nn.sigmoid(a0_ref[...] + _dot(la.astype(BF16), a2_ref[...]))
    bd = bd_ref[...]
    kk = kb * kkg_ref[...]
    kk = kk * lax.rsqrt(_seg_sum(kk * kk, bd) + 1e-12)
    ka = ka_ref[...]
    a_f, a_b = a[:, :D_HALF], a[:, D_HALF:]
    kd_f = kb * (1.0 + (a_f - 1.0) * ka)
    kd_b = kb * (1.0 + (a_b - 1.0) * ka)
    _put(wf_ref, w[:, :D_HALF])
    _put(wb_ref, w[:, D_HALF:])
    _put(kdf_ref, kd_f)
    _put(kdb_ref, kd_b)
    _put(kaf_ref, kk * a_f)
    _put(kab_ref, kk * a_b)
    _put(nkk_ref, -kk)
    _put(r_ref, rb)
    _put(v_ref, vb)
    _put(bonus_ref, _seg_sum(rb * rk_ref[...] * (kd_f + kd_b), bd) * vb)


def _block_diag2(m0, m1):
    z = jnp.zeros_like(m0)
    return jnp.concatenate([jnp.concatenate([m0, z], axis=1), jnp.concatenate([z, m1], axis=1)], axis=0)


def _rwkv_prep(zb, shift_mu, w0, w2, a0, a2, k_k, k_a, r_k):
    B, T, _ = zb.shape
    n_t = T // ROW_TILE
    per_tile = ROW_TILE // SUBLANES
    bd = _block_diag_ones(D_HALF, HS_B)
    row = lambda x: x.reshape(1, -1)
    vec = pl.BlockSpec((1, D_HALF), lambda b, i: (0, 0))
    vec2 = pl.BlockSpec((1, 2 * D_HALF), lambda b, i: (0, 0))
    out = pl.BlockSpec((1, ROW_TILE, D_HALF), lambda b, i: (b, i, 0))
    return pl.pallas_call(
        functools.partial(_rwkv_prep_kernel, n_t),
        grid=(B, n_t),
        in_specs=[
            pl.BlockSpec((1, ROW_TILE, B_SHIFT), lambda b, i: (b, i, 0)),
            pl.BlockSpec((1, SUBLANES, B_SHIFT), lambda b, i: (b, jnp.maximum(i * per_tile - 1, 0), 0)),
            pl.BlockSpec((1, SUBLANES, B_SHIFT),
                         lambda b, i: (b, jnp.minimum((i + 1) * per_tile, n_t * per_tile - 1), 0)),
            pl.BlockSpec((1, B_SHIFT), lambda b, i: (0, 0)),
            vec2, vec2,
            pl.BlockSpec((2 * W_LORA, 2 * D_HALF), lambda b, i: (0, 0)),
            pl.BlockSpec((2 * A_LORA, 2 * D_HALF), lambda b, i: (0, 0)),
            vec, vec, vec,
            pl.BlockSpec(bd.shape, lambda b, i: (0, 0)),
        ],
        out_specs=[out] * 10,
        out_shape=[jax.ShapeDtypeStruct((B, T, D_HALF), F32)] * 10,
        compiler_params=_params("parallel", "parallel"),
    )(zb, zb, zb, row(shift_mu), row(w0), row(a0),
      _block_diag2(w2[0], w2[1]).astype(BF16), _block_diag2(a2[0], a2[1]).astype(BF16),
      row(k_k), row(k_a), row(r_k), bd)


CHAIN_ROWS = 32
HEAD_PAIRS = H_B // 2


def _step_rows(ref, t, vs):
    if vs == 1:
        return ref[:, t, :]
    return jnp.concatenate([jnp.broadcast_to(ref[b, t:t + 1, :], (vs, ref.shape[2])) for b in range(ref.shape[0])],
                           axis=0)


def _to_chains(x):
    return jnp.concatenate([x[:, j * LANES:(j + 1) * LANES] for j in range(HEAD_PAIRS)], axis=0).T


def _rwkv_scan_kernel(reverse, vs, w_ref, nkk_ref, kka_ref, kd_ref, r_ref, v_ref, s0_ref, y_ref, s_ref,
                      kbuf, vbuf, ybuf):
    vh = HS_B // vs
    hoist = vh > SUBLANES
    groups = 2 if vh >= 2 * SUBLANES else 1

    @pl.when(pl.program_id(0) == 0)
    def _():
        s_ref[...] = s0_ref[...]

    split = lax.broadcasted_iota(jnp.int32, (vh, LANES), 1) % vs
    steps = range(SCAN_T - 1, -1, -1) if reverse else range(SCAN_T)
    W_, NKK_, KKA_, KD_, R_, V_ = range(6)

    def relayout(t):
        for a, ref in enumerate((w_ref, nkk_ref, kka_ref, kd_ref, r_ref, v_ref)):
            kbuf[t, a] = _to_chains(_step_rows(ref, t, vs))

    def key_row(t, a, par, k):
        return kbuf[t, a, pl.ds(par * HS_B + k, 1), :]

    def tree_sum(parts):
        return (parts[0] + parts[1]) + (parts[2] + parts[3])

    def update(t, par, bases, yslot):
        lanes = slice(par * LANES, (par + 1) * LANES)
        rows = [pl.ds(base, SUBLANES) for base in bases]
        acc = [[None] * 4 for _ in bases]
        for k in range(HS_B):
            nkk = key_row(t, NKK_, par, k)
            for i, rw in enumerate(rows):
                p = s_ref[k, rw, lanes] * nkk
                acc[i][k % 4] = p if acc[i][k % 4] is None else acc[i][k % 4] + p
        sa = [tree_sum(a) for a in acc]
        vb = [vbuf[par, rw, :] for rw in rows]
        yacc = [[None] * 4 for _ in bases]
        for k in range(HS_B):
            w, kka, kd, r = (key_row(t, a, par, k) for a in (W_, KKA_, KD_, R_))
            for i, rw in enumerate(rows):
                s = s_ref[k, rw, lanes] * w + sa[i] * kka + vb[i] * kd
                s_ref[k, rw, lanes] = s
                p = s * r
                yacc[i][k % 4] = p if yacc[i][k % 4] is None else yacc[i][k % 4] + p
        for i, rw in enumerate(rows):
            ybuf[yslot, par, rw, :] = tree_sum(yacc[i])

    def step(t):
        yslot = t if hoist else 0
        for par in range(2):
            v_par = kbuf[t, V_, par * HS_B:(par + 1) * HS_B, :]
            v_own = v_par[:vh]
            for q in range(1, vs):
                v_own = jnp.where(split == q, v_par[q * vh:(q + 1) * vh], v_own)
            vbuf[par] = v_own
            if vh == SUBLANES:
                update(t, par, [0], yslot)
            else:
                span = groups * SUBLANES

                def body(g, c, par=par):
                    base = pl.multiple_of(g * span, span)
                    update(t, par, [base + i * SUBLANES for i in range(groups)], yslot)
                    return c

                lax.fori_loop(0, vh // span, body, 0)

    def emit_y(t):
        halves = []
        for par in range(2):
            y = ybuf[t if hoist else 0, par]
            if vs > 1:
                y = jnp.concatenate([jnp.where(split == q, y, 0.0) for q in range(vs)], axis=0)
            halves.append(y)
        yt = jnp.concatenate(halves, axis=0).T
        for j in range(HEAD_PAIRS):
            blk = yt[j * CHAIN_ROWS:(j + 1) * CHAIN_ROWS]
            if vs > 1:
                blk = jnp.sum(blk.reshape(CHAIN_ROWS // vs, vs, LANES), axis=1)
            y_ref[:, t, j * LANES:(j + 1) * LANES] = blk

    if hoist:
        for t in steps:
            relayout(t)
        for t in steps:
            step(t)
        for t in steps:
            emit_y(t)
    else:
        for t in steps:
            relayout(t)
            step(t)
            emit_y(t)


def _rwkv_scan(w, nkk, kka, kd, r, v, s0, reverse):
    B, T, W = w.shape
    vs = CHAIN_ROWS // B
    vh = HS_B // vs
    n_t = T // SCAN_T
    tblk = (lambda t: n_t - 1 - t) if reverse else (lambda t: t)
    vec = pl.BlockSpec((B, SCAN_T, W), lambda t: (0, tblk(t), 0))
    state = pl.BlockSpec((HS_B, vh, 2 * LANES), lambda t: (0, 0, 0))
    return pl.pallas_call(
        functools.partial(_rwkv_scan_kernel, reverse, vs),
        grid=(n_t,),
        in_specs=[vec] * 6 + [state],
        out_specs=[vec, state],
        out_shape=[jax.ShapeDtypeStruct((B, T, W), F32), jax.ShapeDtypeStruct((HS_B, vh, 2 * LANES), F32)],
        scratch_shapes=[pltpu.VMEM((SCAN_T, 6, LANES, LANES), F32), pltpu.VMEM((2, vh, LANES), F32),
                        pltpu.VMEM((SCAN_T if vh > SUBLANES else 1, 2, vh, LANES), F32)],
        compiler_params=_params("arbitrary"),
    )(w, nkk, kka, kd, r, v, s0)


def _outproj_even_kernel(latent, oa_ref, yf_ref, yb_ref, bonus_ref, gb_ref, lng_ref, lnb_ref, bd_ref,
                         wa_ref, wb_ref, x_ref, gate_ref, out_ref):
    r = _mod_row(latent)
    bd = bd_ref[...]
    y = yf_ref[0] + yb_ref[0]
    inv_n = 1.0 / HS_B
    d = y - _seg_sum(y, bd) * inv_n
    var = _seg_sum(d * d, bd) * inv_n
    yn = d * lax.rsqrt(var + GN_EPS) * lng_ref[...] + lnb_ref[...]
    o_b = ((yn + bonus_ref[0]) * _silu(gb_ref[0])).astype(BF16)
    acc = _dot(oa_ref[0], wa_ref[...]) + _dot(o_b, wb_ref[...])
    out_ref[0] = x_ref[0] + gate_ref[0, pl.ds(r, 1), :] * acc


def _outproj_even(o_a, y_f, y_b, bonus, gb, ln_g, ln_b, w_a, w_b, x, mod, layer, latent):
    B, T, _ = x.shape
    nrow = mod.shape[1]
    bd = _block_diag_ones(D_HALF, HS_B)
    half = pl.BlockSpec((1, ROW_TILE, D_HALF), lambda b, i: (b, i, 0))
    full = pl.BlockSpec((1, ROW_TILE, D_MODEL), lambda b, i: (b, i, 0))
    vec = pl.BlockSpec((1, D_HALF), lambda b, i: (0, 0))
    const = lambda a: pl.BlockSpec(a.shape, lambda b, i: (0, 0))
    return pl.pallas_call(
        functools.partial(_outproj_even_kernel, latent),
        grid=(B, T // ROW_TILE),
        in_specs=[half, half, half, half, half, vec, vec, const(bd), const(w_a), const(w_b), full,
                  pl.BlockSpec((1, nrow, D_MODEL), lambda b, i: (layer, 0, 2))],
        out_specs=full,
        out_shape=jax.ShapeDtypeStruct((B, T, D_MODEL), F32),
        compiler_params=_params("parallel", "parallel"),
    )(o_a, y_f, y_b, bonus, gb, ln_g.reshape(1, -1), ln_b.reshape(1, -1), bd, w_a, w_b, x, mod)


def _rwkv_mixer(zb, p, s_init):
    B, T, _ = zb.shape
    (w_f, w_b, kd_f, kd_b, ka_f, ka_b, nkk, r, v, bonus) = _rwkv_prep(
        zb, p["shift_mu"], p["w0"], p["w2"], p["a0"], p["a2"], p["k_k"], p["k_a"], p["r_k"])
    vs = CHAIN_ROWS // B
    vh = HS_B // vs
    chains = H_B * B * vs
    ys, fin = [], []
    for d, (w_d, ka_d, kd_d) in enumerate(((w_f, ka_f, kd_f), (w_b, ka_b, kd_b))):
        if s_init is None:
            s0 = jnp.zeros((HS_B, vh, chains), F32)
        else:
            s0 = s_init[d].reshape(B, HEAD_PAIRS, 2, vs, vh, HS_B).transpose(5, 4, 2, 1, 0, 3)
            s0 = s0.reshape(HS_B, vh, chains)
        y, s_fin = _rwkv_scan(w_d, nkk, ka_d, kd_d, r, v, s0, reverse=(d == 1))
        ys.append(y)
        fin.append(s_fin.reshape(HS_B, vh, 2, HEAD_PAIRS, B, vs).transpose(4, 3, 2, 5, 1, 0)
                   .reshape(B, H_B, HS_B, HS_B))
    return (ys[0], ys[1], bonus), fin


def _chunk_cumsum(x, chunk, suffix):
    T = x.shape[0]
    pos = lax.broadcasted_iota(jnp.int32, (T, 1), 0) % chunk
    step = 1
    while step < chunk:
        if suffix:
            x = x + jnp.where(pos < chunk - step, pltpu.roll(x, T - step, 0), 0.0)
        else:
            x = x + jnp.where(pos >= step, pltpu.roll(x, step, 0), 0.0)
        step *= 2
    return x


def _loop(n, body, static):
    if static:
        for i in range(n):
            body(i)
    else:
        lax.fori_loop(0, n, lambda i, c: (body(i), c)[1], 0, unroll=4)


def _gla_kernel(T, hp, q_ref, k_ref, v_ref, g_ref, gl_ref, w2_ref, gb_ref, lng_ref, tri_ref, s0f_ref, s0b_ref,
                *rest):
    o_ref, sf_ref, sb_ref, b_ref, acc_ref, qb_ref, dec_ref, u_ref, sst_ref, st_ref = rest[-10:]
    C = GLA_CHUNK
    n_c = T // C
    static = n_c <= 4
    qscale = DK_C ** -0.5
    contract_last = (((1,), (1,)), ((), ()))
    contract_first = (((0,), (0,)), ((), ()))
    vis_f = tri_ref[0]
    vis_b = tri_ref[1]
    heads = range(hp)
    klanes = lambda h: slice(h * DK_C, (h + 1) * DK_C)
    vlanes = lambda h: slice(h * DV_C, (h + 1) * DV_C)

    for h in heads:
        pre = _dot(gl_ref[0].astype(BF16), w2_ref[h]) + gb_ref[h]
        la = jax.nn.log_sigmoid(pre) * (1.0 / GLA_TAU)
        b_ref[h, :, :DK_C] = _chunk_cumsum(la[:, :DK_C], C, suffix=False)
        b_ref[h, :, DK_C:] = _chunk_cumsum(la[:, DK_C:], C, suffix=True)
        st_ref[h, 0] = s0f_ref[0, 0, h].T
        st_ref[h, 1] = s0b_ref[0, 0, h].T

    def chunk_rows(c):
        return pl.ds(c * C, C) if static else pl.ds(pl.multiple_of(c * C, C), C)

    def dec_rows(c, n=SUBLANES):
        return pl.ds(c * SUBLANES, n) if static else pl.ds(pl.multiple_of(c * SUBLANES, SUBLANES), n)

    def intra(c):
        rows = chunk_rows(c)
        for h in heads:
            q = q_ref[0, rows, klanes(h)] * qscale
            k = k_ref[0, rows, klanes(h)]
            vc = v_ref[0, rows, vlanes(h)].astype(BF16)
            b = b_ref[h, rows, :]
            btot_f = b[C - 1:C, :DK_C]
            btot_b = b[0:1, DK_C:]
            qe, ke, qb, kl = [], [], [], []
            for bd, btot in ((b[:, :DK_C], btot_f), (b[:, DK_C:], btot_b)):
                mref = 0.5 * btot
                e_half = jnp.exp(mref)
                q_up = q * jnp.exp(bd - mref)
                k_dn = k * jnp.exp(mref - bd)
                qe.append(q_up.astype(BF16))
                ke.append(k_dn.astype(BF16))
                qb.append((q_up * e_half).astype(BF16))
                kl.append((k_dn * e_half).astype(BF16))
            sc = lax.dot_general(jnp.concatenate(qe, axis=0), jnp.concatenate(ke, axis=0), contract_last,
                                 preferred_element_type=F32)
            att = sc[:C] * vis_f + pltpu.roll(sc[C:], C, 1) * vis_b
            acc_ref[rows, vlanes(h)] = _dot(att[:, :C].astype(BF16), vc)
            u_ref[h, c] = lax.dot_general(vc, jnp.concatenate(kl, axis=1), contract_first,
                                          preferred_element_type=F32)
            qb_ref[h, rows, :] = jnp.concatenate(qb, axis=1)
            dec_ref[h, dec_rows(c), :] = jnp.broadcast_to(
                jnp.exp(jnp.concatenate([btot_f, btot_b], axis=1)), (SUBLANES, 2 * DK_C))

    _loop(n_c, intra, static)

    def states(i):
        for h in heads:
            for d, c in ((0, i), (1, n_c - 1 - i)):
                lanes = slice(d * DK_C, (d + 1) * DK_C)
                st = st_ref[h, d]
                sst_ref[h, c, :, lanes] = st.astype(BF16)
                st_ref[h, d] = st * dec_ref[h, dec_rows(c, 1), lanes] + u_ref[h, c, :, lanes]

    _loop(n_c, states, static)

    def inter(c):
        rows = chunk_rows(c)
        for h in heads:
            acc_ref[rows, vlanes(h)] = acc_ref[rows, vlanes(h)] + lax.dot_general(
                qb_ref[h, rows, :], sst_ref[h, c], contract_last, preferred_element_type=F32)

    _loop(n_c, inter, static)

    for h in heads:
        sf_ref[0, 0, h] = st_ref[h, 0].T
        sb_ref[0, 0, h] = st_ref[h, 1].T
        o = acc_ref[:, vlanes(h)]
        o = o * lax.rsqrt(jnp.mean(o * o, axis=-1, keepdims=True) + EPS) * lng_ref[...]
        o_ref[0, :, vlanes(h)] = (o * _silu(g_ref[0, :, vlanes(h)])).astype(o_ref.dtype)


def _gla_mixer(q, k, v, g, gl, gw2, gbias, ln_g, s_init, layer_j, prev_fin=None):
    B, T, _ = q.shape
    C = GLA_CHUNK
    n_c = T // C
    hp = H_C if n_c <= 4 else 2
    idx = np.arange(C)
    tri = np.zeros((2, C, LANES), np.float32)
    tri[0, :, :C] = idx[:, None] >= idx[None, :]
    tri[1, :, :C] = idx[:, None] <= idx[None, :]
    tri = jnp.asarray(tri)
    w2 = jnp.stack([_block_diag2(gw2[0][:, h * DK_C:(h + 1) * DK_C], gw2[1][:, h * DK_C:(h + 1) * DK_C])
                    for h in range(H_C)]).astype(BF16)
    gb = jnp.stack([jnp.concatenate([gbias[0][h * DK_C:(h + 1) * DK_C], gbias[1][h * DK_C:(h + 1) * DK_C]])
                    for h in range(H_C)]).reshape(H_C, 1, 2 * DK_C)
    if s_init is None:
        s0f = s0b = jnp.zeros((1, 1, hp, DK_C, DV_C), F32)
        s_spec = pl.BlockSpec((1, 1, hp, DK_C, DV_C), lambda b, h: (0, 0, 0, 0, 0))
    else:
        s0f, s0b = s_init
        s_spec = pl.BlockSpec((1, 1, hp, DK_C, DV_C), lambda b, h: (b, layer_j, h, 0, 0))
    n_odd = DEPTH // 2
    st_shape = (B, n_odd, H_C, DK_C, DV_C)
    st_out = pl.BlockSpec((1, 1, hp, DK_C, DV_C), lambda b, h: (b, layer_j, h, 0, 0))
    args = [q, k, v, g, gl, w2, gb, ln_g.reshape(1, -1), tri, s0f, s0b]
    if prev_fin is None:
        prev_fin = (jnp.zeros(st_shape, F32), jnp.zeros(st_shape, F32))
    extra_specs = [pl.BlockSpec(memory_space=pl.ANY)] * 2
    aliases = {len(args): 1, len(args) + 1: 2}
    args += list(prev_fin)
    o, sf, sb = pl.pallas_call(
        functools.partial(_gla_kernel, T, hp),
        grid=(B, H_C // hp),
        input_output_aliases=aliases,
        in_specs=[
            pl.BlockSpec((1, T, hp * DK_C), lambda b, h: (b, 0, h)),
            pl.BlockSpec((1, T, hp * DK_C), lambda b, h: (b, 0, h)),
            pl.BlockSpec((1, T, hp * DV_C), lambda b, h: (b, 0, h)),
            pl.BlockSpec((1, T, hp * DV_C), lambda b, h: (b, 0, h)),
            pl.BlockSpec((1, T, 2 * G_LORA), lambda b, h: (b, 0, 0)),
            pl.BlockSpec((hp, 2 * G_LORA, 2 * DK_C), lambda b, h: (h, 0, 0)),
            pl.BlockSpec((hp, 1, 2 * DK_C), lambda b, h: (h, 0, 0)),
            pl.BlockSpec((1, DV_C), lambda b, h: (0, 0)),
            pl.BlockSpec((2, C, LANES), lambda b, h: (0, 0, 0)),
            s_spec, s_spec,
        ] + extra_specs,
        out_specs=[pl.BlockSpec((1, T, hp * DV_C), lambda b, h: (b, 0, h)), st_out, st_out],
        out_shape=[
            jax.ShapeDtypeStruct((B, T, H_C * DV_C), BF16),
            jax.ShapeDtypeStruct(st_shape, F32),
            jax.ShapeDtypeStruct(st_shape, F32),
        ],
        scratch_shapes=[
            pltpu.VMEM((hp, T, 2 * DK_C), F32), pltpu.VMEM((T, hp * DV_C), F32),
            pltpu.VMEM((hp, T, 2 * DK_C), BF16), pltpu.VMEM((hp, n_c * SUBLANES, 2 * DK_C), F32),
            pltpu.VMEM((hp, n_c, DV_C, 2 * DK_C), F32), pltpu.VMEM((hp, n_c, DV_C, 2 * DK_C), BF16),
            pltpu.VMEM((hp, 2, DV_C, DK_C), F32),
        ],
        compiler_params=_params("parallel", "parallel"),
    )(*args)
    return o, (sf, sb)


def kernel(x_prompt, x_sample, c, cache_attn_k, cache_attn_v, state_rwkv_fwd, state_rwkv_bwd, state_gla_fwd, state_gla_bwd, c_ctx, norm_g, mod_w, mod_b, ev_w_in, ev_w_out, ev_qn_g, ev_kn_g, ev_shift_mu, rw_w0, rw_w2, rw_a0, rw_a2, rw_kk, rw_ka, rw_rk, rw_ln_g, rw_ln_b, od_w_in, od_w_out, gla_w2, gla_b, gla_ln_g, final_g):
    n_dec = c.shape[0]
    cond = jnp.concatenate([c_ctx[None], c, jnp.zeros((SUBLANES - 1 - n_dec, D_MODEL), F32)], axis=0)
    mod = _modulation(cond, mod_w, mod_b)

    ev_in = ev_w_in.astype(BF16)
    ev_out = ev_w_out.astype(BF16)
    od_in = od_w_in.astype(BF16)
    od_out = od_w_out.astype(BF16)
    kw = KV_A * HD_A
    ck = cache_attn_k.reshape(cache_attn_k.shape[:3] + (kw,))
    cv = cache_attn_v.reshape(cache_attn_v.shape[:3] + (kw,))

    def trunk(x, latent):
        new = {"k": [], "v": [], "rf": [], "rb": []}
        gla_fin = None
        for i in range(DEPTH):
            j = i // 2
            fg = final_g if i == DEPTH - 1 else None
            if i % 2 == 0:
                qa, ka, va, ga, zb, gb = _inproj(x, norm_g[i], mod, i, ev_in[j], EV_SPLITS, latent)
                if latent:
                    o_a, _ = _attention(qa, ka, va, ga, ev_qn_g[j], ev_kn_g[j], ck, cv, j)
                    s_init = (state_rwkv_fwd[:, j], state_rwkv_bwd[:, j])
                else:
                    o_a, kn = _attention(qa, ka, va, ga, ev_qn_g[j], ev_kn_g[j])
                    s_init = None
                    new["k"].append(kn)
                    new["v"].append(va)
                p = dict(shift_mu=ev_shift_mu[j], w0=rw_w0[j], w2=rw_w2[j], a0=rw_a0[j], a2=rw_a2[j],
                         k_k=rw_kk[j], k_a=rw_ka[j], r_k=rw_rk[j])
                (y_f, y_b, bonus), fin = _rwkv_mixer(zb, p, s_init)
                new["rf"].append(fin[0])
                new["rb"].append(fin[1])
                x = _outproj_even(o_a, y_f, y_b, bonus, gb, rw_ln_g[j], rw_ln_b[j],
                                  ev_out[j, :D_HALF], ev_out[j, D_HALF:], x, mod, i, latent)
            else:
                q, k, v, g, gl = _inproj(x, norm_g[i], mod, i, od_in[j], OD_SPLITS, latent,
                                         dtypes=(F32, F32, BF16, F32, F32))
                s_init = (state_gla_fwd, state_gla_bwd) if latent else None
                o, gla_fin = _gla_mixer(q, k, v, g, gl, gla_w2[j], gla_b[j], gla_ln_g[j], s_init, j, gla_fin)
                x = _outproj([o], [od_out[j]], x, mod, i, latent, fg)
        new["gf"], new["gb"] = gla_fin
        return x, new

    y_prompt, new = trunk(x_prompt, False)
    y_sample, _ = trunk(x_sample, True)
    B, T = x_prompt.shape[:2]
    heads = lambda t: t.reshape(B, T, KV_A, HD_A)
    return (y_prompt, y_sample,
            jnp.stack([heads(t) for t in new["k"]], axis=1), jnp.stack([heads(t) for t in new["v"]], axis=1),
            jnp.stack(new["rf"], axis=1), jnp.stack(new["rb"], axis=1),
            new["gf"], new["gb"])
```

```python
import functools

import numpy as np
import jax
import jax.numpy as jnp
from jax import lax
from jax.experimental import pallas as pl
from jax.experimental.pallas import tpu as pltpu

F32 = jnp.float32
BF16 = jnp.bfloat16

D_MODEL = 1024
DEPTH = 4
GRID_W = 64
D_HALF = D_MODEL // 2
HD_A = 64
H_A = D_HALF // HD_A
KV_A = H_A // 4
ROPE_BASE = 10000.0
HS_B = 64
H_B = D_HALF // HS_B
W_LORA = 64
A_LORA = 64
RWKV_DECAY_SCALE = 0.606531
GN_EPS = 64e-5
B_SHIFT = 3 * D_HALF + 2 * W_LORA + 2 * A_LORA
H_C = 4
DK_C = D_MODEL // 2 // H_C
DV_C = D_MODEL // H_C
G_LORA = 16
GLA_TAU = 16.0
EPS = 1e-6

EV_SPLITS = (H_A * HD_A, KV_A * HD_A, KV_A * HD_A, D_HALF, B_SHIFT, D_HALF)
OD_SPLITS = (H_C * DK_C, H_C * DK_C, D_MODEL, D_MODEL, 2 * G_LORA)

LANES = 128
SUBLANES = 8
VMEM_LIMIT_BYTES = 56 * 1024 * 1024

ROW_TILE = 256
Q_TILE = 256
GLA_CHUNK = 64
SCAN_T = 16


def _params(*sem):
    return pltpu.CompilerParams(dimension_semantics=sem, vmem_limit_bytes=VMEM_LIMIT_BYTES)


def _silu(x):
    return x * jax.nn.sigmoid(x)


def _dot(a, b):
    return jnp.dot(a, b, preferred_element_type=F32)


def _seg_sum(x, ones_bd):
    hi = x.astype(BF16)
    r1 = x - hi.astype(F32)
    mid = r1.astype(BF16)
    lo = (r1 - mid.astype(F32)).astype(BF16)
    return _dot(hi, ones_bd) + _dot(mid, ones_bd) + _dot(lo, ones_bd)


def _get(ref):
    return ref[...].reshape(ref.shape[-2:])


def _put(ref, val):
    ref[...] = val.reshape(ref.shape)


def _block_diag_ones(n, blk):
    i = np.arange(n) // blk
    return jnp.asarray((i[:, None] == i[None, :]).astype(np.float32), dtype=BF16)


def _mod_kernel(cond_ref, w_ref, b_ref, o_ref):
    s = _silu(cond_ref[...])
    o_ref[0] = _dot(s.astype(BF16), w_ref[0].astype(BF16)) + b_ref[0]


def _modulation(cond, mod_w, mod_b):
    n = cond.shape[0]
    return pl.pallas_call(
        _mod_kernel,
        grid=(DEPTH, 3),
        in_specs=[
            pl.BlockSpec((n, D_MODEL), lambda i, j: (0, 0)),
            pl.BlockSpec((1, D_MODEL, D_MODEL), lambda i, j: (i, 0, j)),
            pl.BlockSpec((1, 1, D_MODEL), lambda i, j: (i, 0, j)),
        ],
        out_specs=pl.BlockSpec((1, n, D_MODEL), lambda i, j: (i, 0, j)),
        out_shape=jax.ShapeDtypeStruct((DEPTH, n, 3 * D_MODEL), F32),
        compiler_params=_params("parallel", "parallel"),
    )(cond, mod_w, mod_b.reshape(DEPTH, 1, 3 * D_MODEL))


def _mod_row(latent):
    return (1 + pl.program_id(0)) if latent else 0


def _inproj_kernel(latent, splits, x_ref, g_ref, sh_ref, sc_ref, w_ref, *out_refs):
    r = _mod_row(latent)
    x = x_ref[0]
    y = x * lax.rsqrt(jnp.mean(x * x, axis=-1, keepdims=True) + EPS) * g_ref[...]
    shift = sh_ref[0, pl.ds(r, 1), :]
    scale = sc_ref[0, pl.ds(r, 1), :]
    h = (y * (1.0 + scale) + shift).astype(BF16)
    off = 0
    for o_ref, n in zip(out_refs, splits):
        o_ref[0] = _dot(h, w_ref[:, off:off + n]).astype(o_ref.dtype)
        off += n


def _inproj(x, norm_g, mod, layer, w_bf16, splits, latent, dtypes=None):
    B, T, _ = x.shape
    dtypes = dtypes or (F32,) * len(splits)
    cols = w_bf16.shape[1]
    nrow = mod.shape[1]
    return pl.pallas_call(
        functools.partial(_inproj_kernel, latent, splits),
        grid=(B, T // ROW_TILE),
        in_specs=[
            pl.BlockSpec((1, ROW_TILE, D_MODEL), lambda b, i: (b, i, 0)),
            pl.BlockSpec((1, D_MODEL), lambda b, i: (0, 0)),
            pl.BlockSpec((1, nrow, D_MODEL), lambda b, i: (layer, 0, 0)),
            pl.BlockSpec((1, nrow, D_MODEL), lambda b, i: (layer, 0, 1)),
            pl.BlockSpec((D_MODEL, cols), lambda b, i: (0, 0)),
        ],
        out_specs=[pl.BlockSpec((1, ROW_TILE, n), lambda b, i: (b, i, 0)) for n in splits],
        out_shape=[jax.ShapeDtypeStruct((B, T, n), dt) for n, dt in zip(splits, dtypes)],
        compiler_params=_params("parallel", "parallel"),
    )(x, norm_g.reshape(1, D_MODEL), mod, mod, w_bf16)


def _outproj_kernel(latent, final, n_in, *refs):
    o_refs = refs[:n_in]
    w_refs = refs[n_in:2 * n_in]
    x_ref, gate_ref = refs[2 * n_in], refs[2 * n_in + 1]
    rest = refs[2 * n_in + 2:]
    r = _mod_row(latent)
    acc = _dot(o_refs[0][0].astype(BF16), w_refs[0][...])
    for o_ref, w_ref in zip(o_refs[1:], w_refs[1:]):
        acc = acc + _dot(o_ref[0].astype(BF16), w_ref[...])
    y = x_ref[0] + gate_ref[0, pl.ds(r, 1), :] * acc
    if final:
        fg_ref, out_ref = rest
        y = y * lax.rsqrt(jnp.mean(y * y, axis=-1, keepdims=True) + EPS) * fg_ref[...]
    else:
        (out_ref,) = rest
    out_ref[0] = y


def _outproj(outs, ws_bf16, x, mod, layer, latent, final_g=None):
    B, T, _ = x.shape
    nrow = mod.shape[1]
    n_in = len(outs)
    final = final_g is not None
    in_specs = [pl.BlockSpec((1, ROW_TILE, o.shape[-1]), lambda b, i: (b, i, 0)) for o in outs]
    in_specs += [pl.BlockSpec(w.shape, lambda b, i: (0, 0)) for w in ws_bf16]
    in_specs += [
        pl.BlockSpec((1, ROW_TILE, D_MODEL), lambda b, i: (b, i, 0)),
        pl.BlockSpec((1, nrow, D_MODEL), lambda b, i: (layer, 0, 2)),
    ]
    args = list(outs) + list(ws_bf16) + [x, mod]
    if final:
        in_specs.append(pl.BlockSpec((1, D_MODEL), lambda b, i: (0, 0)))
        args.append(final_g.reshape(1, D_MODEL))
    return pl.pallas_call(
        functools.partial(_outproj_kernel, latent, final, n_in),
        grid=(B, T // ROW_TILE),
        in_specs=in_specs,
        out_specs=pl.BlockSpec((1, ROW_TILE, D_MODEL), lambda b, i: (b, i, 0)),
        out_shape=jax.ShapeDtypeStruct((B, T, D_MODEL), F32),
        compiler_params=_params("parallel", "parallel"),
    )(*args)


def _rope_tables(T):
    n_rows = T // GRID_W
    row = jnp.repeat(jnp.arange(n_rows), GRID_W).astype(F32)
    col = jnp.tile(jnp.arange(GRID_W), n_rows).astype(F32)
    n_freq = HD_A // 4
    inv = ROPE_BASE ** (-jnp.arange(n_freq, dtype=F32) / n_freq)
    ang_r = row[:, None] * inv
    ang_c = col[:, None] * inv
    zero = jnp.zeros_like(ang_r)
    cos = jnp.concatenate([jnp.cos(ang_r), jnp.cos(ang_r), jnp.cos(ang_c), jnp.cos(ang_c)], axis=1)
    s1 = jnp.concatenate([-jnp.sin(ang_r), zero, -jnp.sin(ang_c), zero], axis=1)
    s2 = jnp.concatenate([zero, jnp.sin(ang_r), zero, jnp.sin(ang_c)], axis=1)
    return cos, s1, s2


def _rope(x, cos, s1, s2):
    n = x.shape[-1]
    q = HD_A // 4
    return x * cos + pltpu.roll(x, n - q, 1) * s1 + pltpu.roll(x, q, 1) * s2


def _attn_kernel(latent, S, *refs):
    if latent:
        (q_ref, k_ref, v_ref, ga_ref, qg_ref, kg_ref, bd_ref, cos_ref, s1_ref, s2_ref,
         cosk_ref, s1k_ref, s2k_ref, ck_ref, cv_ref, o_ref, km_ref, vm_ref) = refs
    else:
        (q_ref, k_ref, v_ref, ga_ref, qg_ref, kg_ref, bd_ref, o_ref, kn_ref, km_ref, vm_ref) = refs
    bd = bd_ref[...]
    inv_d = 1.0 / HD_A
    kw = KV_A * HD_A

    @pl.when(pl.program_id(1) == 0)
    def _():
        k = k_ref[0]
        kn = k * lax.rsqrt(_seg_sum(k * k, bd[:kw, :kw]) * inv_d + EPS) * kg_ref[...]
        v = v_ref[0]
        if latent:
            kn = _rope(kn, cosk_ref[...], s1k_ref[...], s2k_ref[...])
            k_all = jnp.concatenate([ck_ref[0, 0], kn], axis=0)
            v_all = jnp.concatenate([cv_ref[0, 0], v], axis=0)
        else:
            kn_ref[0] = kn
            k_all, v_all = kn, v
        lane = lax.broadcasted_iota(jnp.int32, (S, kw), 1)
        k_sw = pltpu.roll(k_all, HD_A, 1)
        v_sw = pltpu.roll(v_all, HD_A, 1)
        for j in range(KV_A):
            for half in range(2):
                keep = (lane < HD_A) if half == 0 else (lane >= HD_A)
                src_k, src_v = (k_all, v_all) if j == half else (k_sw, v_sw)
                km_ref[2 * j + half] = jnp.where(keep, src_k, 0.0).astype(BF16)
                vm_ref[2 * j + half] = jnp.where(keep, src_v, 0.0).astype(BF16)

    q = q_ref[0]
    qn = q * lax.rsqrt(_seg_sum(q * q, bd) * inv_d + EPS) * qg_ref[...]
    if latent:
        qn = _rope(qn, cos_ref[...], s1_ref[...], s2_ref[...])
    qb = (qn * HD_A ** -0.5).astype(BF16)
    for m in range(H_A // 2):
        blk = slice(m * LANES, (m + 1) * LANES)
        qs = qb[:, blk]
        acc = None
        for half in range(2):
            j = (2 * m + half) // (H_A // KV_A)
            s = lax.dot_general(qs, km_ref[2 * j + half], (((1,), (1,)), ((), ())),
                                preferred_element_type=F32)
            e = jnp.exp(s - jnp.max(s, axis=-1, keepdims=True))
            l = jnp.sum(e, axis=-1, keepdims=True)
            pv = _dot(e.astype(BF16), vm_ref[2 * j + half]) * (1.0 / l)
            acc = pv if acc is None else acc + pv
        o_ref[0, :, blk] = (acc * _silu(ga_ref[0, :, blk])).astype(o_ref.dtype)


def _attention(qa, ka, va, ga, qn_g, kn_g, ctx_k=None, ctx_v=None, layer_j=0):
    B, T, _ = qa.shape
    latent = ctx_k is not None
    S = T + (ctx_k.shape[2] if latent else 0)
    kw = KV_A * HD_A
    qw = H_A * HD_A
    bd = _block_diag_ones(qw, HD_A)
    qblk = pl.BlockSpec((1, Q_TILE, qw), lambda b, i: (b, i, 0))
    kblk = pl.BlockSpec((1, T, kw), lambda b, i: (b, 0, 0))
    in_specs = [qblk, kblk, kblk, qblk,
                pl.BlockSpec((1, qw), lambda b, i: (0, 0)),
                pl.BlockSpec((1, kw), lambda b, i: (0, 0)),
                pl.BlockSpec(bd.shape, lambda b, i: (0, 0))]
    args = [qa, ka, va, ga, jnp.tile(qn_g, H_A).reshape(1, -1), jnp.tile(kn_g, KV_A).reshape(1, -1), bd]
    out_specs = [qblk]
    out_shape = [jax.ShapeDtypeStruct((B, T, qw), BF16)]
    if latent:
        tabs = _rope_tables(T)
        P = ctx_k.shape[2]
        in_specs += [pl.BlockSpec((Q_TILE, qw), lambda b, i: (i, 0))] * 3
        in_specs += [pl.BlockSpec((T, kw), lambda b, i: (0, 0))] * 3
        in_specs += [pl.BlockSpec((1, 1, P, kw), lambda b, i: (b, layer_j, 0, 0))] * 2
        args += [jnp.tile(t, (1, H_A)) for t in tabs] + [jnp.tile(t, (1, KV_A)) for t in tabs] + [ctx_k, ctx_v]
    else:
        out_specs.append(kblk)
        out_shape.append(jax.ShapeDtypeStruct((B, T, kw), F32))
    res = pl.pallas_call(
        functools.partial(_attn_kernel, latent, S),
        grid=(B, T // Q_TILE),
        in_specs=in_specs,
        out_specs=out_specs,
        out_shape=out_shape,
        scratch_shapes=[pltpu.VMEM((2 * KV_A, S, kw), BF16), pltpu.VMEM((2 * KV_A, S, kw), BF16)],
        compiler_params=_params("parallel", "arbitrary"),
    )(*args)
    return res if not latent else (res[0], None)


def _rwkv_prep_kernel(n_t, z_ref, zp_ref, zn_ref, mu_ref, w0_ref, a0_ref, w2_ref, a2_ref,
                      kkg_ref, ka_ref, rk_ref, bd_ref,
                      wf_ref, wb_ref, kdf_ref, kdb_ref, kaf_ref, kab_ref, nkk_ref, r_ref, v_ref, bonus_ref):
    i = pl.program_id(1)
    z = z_ref[0]
    n = z.shape[0]
    prev_row = jnp.where(i > 0, zp_ref[0, SUBLANES - 1:SUBLANES, :], 0.0)
    next_row = jnp.where(i < n_t - 1, zn_ref[0, 0:1, :], 0.0)
    rows = lax.broadcasted_iota(jnp.int32, (n, 1), 0)
    zp = jnp.where(rows == 0, prev_row, pltpu.roll(z, 1, 0))
    zn = jnp.where(rows == n - 1, next_row, pltpu.roll(z, n - 1, 0))
    zs = z + mu_ref[...] * (0.5 * (zp + zn) - z)

    rb = zs[:, 0:D_HALF]
    kb = zs[:, D_HALF:2 * D_HALF]
    vb = zs[:, 2 * D_HALF:3 * D_HALF]
    lw = zs[:, 3 * D_HALF:3 * D_HALF + 2 * W_LORA]
    la = zs[:, 3 * D_HALF + 2 * W_LORA:]
    w = jnp.exp(-RWKV_DECAY_SCALE * jax.nn.sigmoid(w0_ref[...] + _dot(jnp.tanh(lw).astype(BF16), w2_ref[...])))
    a = jax.nn.sigmoid(a0_ref[...] + _dot(la.astype(BF16), a2_ref[...]))
    bd = bd_ref[...]
    kk = kb * kkg_ref[...]
    kk = kk * lax.rsqrt(_seg_sum(kk * kk, bd) + 1e-12)
    ka = ka_ref[...]
    a_f, a_b = a[:, :D_HALF], a[:, D_HALF:]
    kd_f = kb * (1.0 + (a_f - 1.0) * ka)
    kd_b = kb * (1.0 + (a_b - 1.0) * ka)
    _put(wf_ref, w[:, :D_HALF])
    _put(wb_ref, w[:, D_HALF:])
    _put(kdf_ref, kd_f)
    _put(kdb_ref, kd_b)
    _put(kaf_ref, kk * a_f)
    _put(kab_ref, kk * a_b)
    _put(nkk_ref, -kk)
    _put(r_ref, rb)
    _put(v_ref, vb)
    _put(bonus_ref, _seg_sum(rb * rk_ref[...] * (kd_f + kd_b), bd) * vb)


def _block_diag2(m0, m1):
    z = jnp.zeros_like(m0)
    return jnp.concatenate([jnp.concatenate([m0, z], axis=1), jnp.concatenate([z, m1], axis=1)], axis=0)


def _rwkv_prep(zb, shift_mu, w0, w2, a0, a2, k_k, k_a, r_k):
    B, T, _ = zb.shape
    n_t = T // ROW_TILE
    per_tile = ROW_TILE // SUBLANES
    bd = _block_diag_ones(D_HALF, HS_B)
    row = lambda x: x.reshape(1, -1)
    vec = pl.BlockSpec((1, D_HALF), lambda b, i: (0, 0))
    vec2 = pl.BlockSpec((1, 2 * D_HALF), lambda b, i: (0, 0))
    out = pl.BlockSpec((1, ROW_TILE, D_HALF), lambda b, i: (b, i, 0))
    return pl.pallas_call(
        functools.partial(_rwkv_prep_kernel, n_t),
        grid=(B, n_t),
        in_specs=[
            pl.BlockSpec((1, ROW_TILE, B_SHIFT), lambda b, i: (b, i, 0)),
            pl.BlockSpec((1, SUBLANES, B_SHIFT), lambda b, i: (b, jnp.maximum(i * per_tile - 1, 0), 0)),
            pl.BlockSpec((1, SUBLANES, B_SHIFT),
                         lambda b, i: (b, jnp.minimum((i + 1) * per_tile, n_t * per_tile - 1), 0)),
            pl.BlockSpec((1, B_SHIFT), lambda b, i: (0, 0)),
            vec2, vec2,
            pl.BlockSpec((2 * W_LORA, 2 * D_HALF), lambda b, i: (0, 0)),
            pl.BlockSpec((2 * A_LORA, 2 * D_HALF), lambda b, i: (0, 0)),
            vec, vec, vec,
            pl.BlockSpec(bd.shape, lambda b, i: (0, 0)),
        ],
        out_specs=[out] * 10,
        out_shape=[jax.ShapeDtypeStruct((B, T, D_HALF), F32)] * 10,
        compiler_params=_params("parallel", "parallel"),
    )(zb, zb, zb, row(shift_mu), row(w0), row(a0),
      _block_diag2(w2[0], w2[1]).astype(BF16), _block_diag2(a2[0], a2[1]).astype(BF16),
      row(k_k), row(k_a), row(r_k), bd)


CHAIN_ROWS = 32
HEAD_PAIRS = H_B // 2


def _step_rows(ref, t, vs):
    if vs == 1:
        return ref[:, t, :]
    return jnp.concatenate([jnp.broadcast_to(ref[b, t:t + 1, :], (vs, ref.shape[2])) for b in range(ref.shape[0])],
                           axis=0)


def _to_chains(x):
    return jnp.concatenate([x[:, j * LANES:(j + 1) * LANES] for j in range(HEAD_PAIRS)], axis=0).T


def _rwkv_scan_kernel(reverse, vs, w_ref, nkk_ref, kka_ref, kd_ref, r_ref, v_ref, s0_ref, y_ref, s_ref,
                      kbuf, vbuf, ybuf):
    vh = HS_B // vs
    hoist = vh > SUBLANES
    groups = 2 if vh >= 2 * SUBLANES else 1

    @pl.when(pl.program_id(0) == 0)
    def _():
        s_ref[...] = s0_ref[...]

    split = lax.broadcasted_iota(jnp.int32, (vh, LANES), 1) % vs
    steps = range(SCAN_T - 1, -1, -1) if reverse else range(SCAN_T)
    W_, NKK_, KKA_, KD_, R_, V_ = range(6)

    def relayout(t):
        for a, ref in enumerate((w_ref, nkk_ref, kka_ref, kd_ref, r_ref, v_ref)):
            kbuf[t, a] = _to_chains(_step_rows(ref, t, vs))

    def key_row(t, a, par, k):
        return kbuf[t, a, pl.ds(par * HS_B + k, 1), :]

    def tree_sum(parts):
        return (parts[0] + parts[1]) + (parts[2] + parts[3])

    def update_groups(t, par, gs, yslot):
        lanes = slice(par * LANES, (par + 1) * LANES)
        rows = [pl.ds(pl.multiple_of(g * SUBLANES, SUBLANES), SUBLANES) for g in gs]
        acc = [[None] * 4 for _ in gs]
        for k in range(HS_B):
            nkk = key_row(t, NKK_, par, k)
            for i, g in enumerate(gs):
                p = s_ref[g, k, :, lanes] * nkk
                acc[i][k % 4] = p if acc[i][k % 4] is None else acc[i][k % 4] + p
        sa = [tree_sum(a) for a in acc]
        vb = [vbuf[par, rw, :] for rw in rows]
        yacc = [[None] * 4 for _ in gs]
        for k in range(HS_B):
            w, kka, kd, r = (key_row(t, a, par, k) for a in (W_, KKA_, KD_, R_))
            for i, g in enumerate(gs):
                s = s_ref[g, k, :, lanes] * w + sa[i] * kka + vb[i] * kd
                s_ref[g, k, :, lanes] = s
                p = s * r
                yacc[i][k % 4] = p if yacc[i][k % 4] is None else yacc[i][k % 4] + p
        for i, rw in enumerate(rows):
            ybuf[yslot, par, rw, :] = tree_sum(yacc[i])

    def update_rows(t, par, yslot):
        lanes = slice(par * LANES, (par + 1) * LANES)
        w, nkk, kka, kd, r = (kbuf[t, a, par * HS_B:(par + 1) * HS_B, :] for a in (W_, NKK_, KKA_, KD_, R_))
        vblk = vbuf[par]
        ys = []
        for u in range(SUBLANES):
            s = s_ref[u, :, lanes]
            sa = jnp.sum(s * nkk, axis=0, keepdims=True)
            s = s * w + sa * kka + vblk[u:u + 1, :] * kd
            s_ref[u, :, lanes] = s
            ys.append(jnp.sum(s * r, axis=0, keepdims=True))
        ybuf[yslot, par] = jnp.concatenate(ys, axis=0)

    def step(t):
        yslot = t if hoist else 0
        for par in range(2):
            v_par = kbuf[t, V_, par * HS_B:(par + 1) * HS_B, :]
            v_own = v_par[:vh]
            for q in range(1, vs):
                v_own = jnp.where(split == q, v_par[q * vh:(q + 1) * vh], v_own)
            vbuf[par] = v_own
            if not hoist:
                update_rows(t, par, yslot)
            else:
                def body(g, c, par=par):
                    update_groups(t, par, [g * groups + i for i in range(groups)], yslot)
                    return c

                lax.fori_loop(0, vh // (groups * SUBLANES), body, 0)

    def emit_y(t):
        halves = []
        for par in range(2):
            y = ybuf[t if hoist else 0, par]
            if vs > 1:
                y = jnp.concatenate([jnp.where(split == q, y, 0.0) for q in range(vs)], axis=0)
            halves.append(y)
        yt = jnp.concatenate(halves, axis=0).T
        for j in range(HEAD_PAIRS):
            blk = yt[j * CHAIN_ROWS:(j + 1) * CHAIN_ROWS]
            if vs > 1:
                blk = jnp.sum(blk.reshape(CHAIN_ROWS // vs, vs, LANES), axis=1)
            y_ref[:, t, j * LANES:(j + 1) * LANES] = blk

    if hoist:
        for t in steps:
            relayout(t)
        for t in steps:
            step(t)
        for t in steps:
            emit_y(t)
    else:
        for t in steps:
            relayout(t)
            step(t)
            emit_y(t)


def _rwkv_scan(w, nkk, kka, kd, r, v, s0, reverse):
    B, T, W = w.shape
    vs = CHAIN_ROWS // B
    vh = HS_B // vs
    n_t = T // SCAN_T
    tblk = (lambda t: n_t - 1 - t) if reverse else (lambda t: t)
    vec = pl.BlockSpec((B, SCAN_T, W), lambda t: (0, tblk(t), 0))
    state = pl.BlockSpec(s0.shape, lambda t: (0,) * s0.ndim)
    return pl.pallas_call(
        functools.partial(_rwkv_scan_kernel, reverse, vs),
        grid=(n_t,),
        in_specs=[vec] * 6 + [state],
        out_specs=[vec, state],
        out_shape=[jax.ShapeDtypeStruct((B, T, W), F32), jax.ShapeDtypeStruct(s0.shape, F32)],
        scratch_shapes=[pltpu.VMEM((SCAN_T, 6, LANES, LANES), F32), pltpu.VMEM((2, vh, LANES), F32),
                        pltpu.VMEM((SCAN_T if vh > SUBLANES else 1, 2, vh, LANES), F32)],
        compiler_params=_params("arbitrary"),
    )(w, nkk, kka, kd, r, v, s0)


def _outproj_even_kernel(latent, oa_ref, yf_ref, yb_ref, bonus_ref, gb_ref, lng_ref, lnb_ref, bd_ref,
                         wa_ref, wb_ref, x_ref, gate_ref, out_ref):
    r = _mod_row(latent)
    bd = bd_ref[...]
    y = yf_ref[0] + yb_ref[0]
    inv_n = 1.0 / HS_B
    d = y - _seg_sum(y, bd) * inv_n
    var = _seg_sum(d * d, bd) * inv_n
    yn = d * lax.rsqrt(var + GN_EPS) * lng_ref[...] + lnb_ref[...]
    o_b = ((yn + bonus_ref[0]) * _silu(gb_ref[0])).astype(BF16)
    acc = _dot(oa_ref[0], wa_ref[...]) + _dot(o_b, wb_ref[...])
    out_ref[0] = x_ref[0] + gate_ref[0, pl.ds(r, 1), :] * acc


def _outproj_even(o_a, y_f, y_b, bonus, gb, ln_g, ln_b, w_a, w_b, x, mod, layer, latent):
    B, T, _ = x.shape
    nrow = mod.shape[1]
    bd = _block_diag_ones(D_HALF, HS_B)
    half = pl.BlockSpec((1, ROW_TILE, D_HALF), lambda b, i: (b, i, 0))
    full = pl.BlockSpec((1, ROW_TILE, D_MODEL), lambda b, i: (b, i, 0))
    vec = pl.BlockSpec((1, D_HALF), lambda b, i: (0, 0))
    const = lambda a: pl.BlockSpec(a.shape, lambda b, i: (0, 0))
    return pl.pallas_call(
        functools.partial(_outproj_even_kernel, latent),
        grid=(B, T // ROW_TILE),
        in_specs=[half, half, half, half, half, vec, vec, const(bd), const(w_a), const(w_b), full,
                  pl.BlockSpec((1, nrow, D_MODEL), lambda b, i: (layer, 0, 2))],
        out_specs=full,
        out_shape=jax.ShapeDtypeStruct((B, T, D_MODEL), F32),
        compiler_params=_params("parallel", "parallel"),
    )(o_a, y_f, y_b, bonus, gb, ln_g.reshape(1, -1), ln_b.reshape(1, -1), bd, w_a, w_b, x, mod)


def _rwkv_mixer(zb, p, s_init):
    B, T, _ = zb.shape
    (w_f, w_b, kd_f, kd_b, ka_f, ka_b, nkk, r, v, bonus) = _rwkv_prep(
        zb, p["shift_mu"], p["w0"], p["w2"], p["a0"], p["a2"], p["k_k"], p["k_a"], p["r_k"])
    vs = CHAIN_ROWS // B
    vh = HS_B // vs
    chains = H_B * B * vs
    ys, fin = [], []
    for d, (w_d, ka_d, kd_d) in enumerate(((w_f, ka_f, kd_f), (w_b, ka_b, kd_b))):
        s6 = (B, HEAD_PAIRS, 2, vs, vh // SUBLANES, SUBLANES, HS_B)
        if vh > SUBLANES:
            perm, shape = (4, 6, 5, 2, 1, 0, 3), (vh // SUBLANES, HS_B, SUBLANES, chains)
        else:
            perm, shape = (4, 5, 6, 2, 1, 0, 3), (vh, HS_B, chains)
        if s_init is None:
            s0 = jnp.zeros(shape, F32)
        else:
            s0 = s_init[d].reshape(s6).transpose(perm).reshape(shape)
        y, s_fin = _rwkv_scan(w_d, nkk, ka_d, kd_d, r, v, s0, reverse=(d == 1))
        ys.append(y)
        inv = tuple(int(i) for i in np.argsort(perm))
        fin.append(s_fin.reshape([s6[i] for i in perm]).transpose(inv).reshape(B, H_B, HS_B, HS_B))
    return (ys[0], ys[1], bonus), fin


def _chunk_cumsum(x, chunk, suffix):
    T = x.shape[0]
    pos = lax.broadcasted_iota(jnp.int32, (T, 1), 0) % chunk
    step = 1
    while step < chunk:
        if suffix:
            x = x + jnp.where(pos < chunk - step, pltpu.roll(x, T - step, 0), 0.0)
        else:
            x = x + jnp.where(pos >= step, pltpu.roll(x, step, 0), 0.0)
        step *= 2
    return x


def _loop(n, body, static):
    if static:
        for i in range(n):
            body(i)
    else:
        lax.fori_loop(0, n, lambda i, c: (body(i), c)[1], 0, unroll=4)


def _gla_kernel(T, hp, q_ref, k_ref, v_ref, g_ref, gl_ref, w2_ref, gb_ref, lng_ref, tri_ref, s0f_ref, s0b_ref,
                *rest):
    o_ref, sf_ref, sb_ref, b_ref, acc_ref, qb_ref, dec_ref, u_ref, sst_ref, st_ref = rest[-10:]
    C = GLA_CHUNK
    n_c = T // C
    static = n_c <= 4
    qscale = DK_C ** -0.5
    contract_last = (((1,), (1,)), ((), ()))
    contract_first = (((0,), (0,)), ((), ()))
    vis_f = tri_ref[0]
    vis_b = tri_ref[1]
    heads = range(hp)
    klanes = lambda h: slice(h * DK_C, (h + 1) * DK_C)
    vlanes = lambda h: slice(h * DV_C, (h + 1) * DV_C)

    for h in heads:
        pre = _dot(gl_ref[0].astype(BF16), w2_ref[h]) + gb_ref[h]
        la = jax.nn.log_sigmoid(pre) * (1.0 / GLA_TAU)
        b_ref[h, :, :DK_C] = _chunk_cumsum(la[:, :DK_C], C, suffix=False)
        b_ref[h, :, DK_C:] = _chunk_cumsum(la[:, DK_C:], C, suffix=True)
        st_ref[h, 0] = s0f_ref[0, 0, h].T
        st_ref[h, 1] = s0b_ref[0, 0, h].T

    def chunk_rows(c):
        return pl.ds(c * C, C) if static else pl.ds(pl.multiple_of(c * C, C), C)

    def dec_rows(c, n=SUBLANES):
        return pl.ds(c * SUBLANES, n) if static else pl.ds(pl.multiple_of(c * SUBLANES, SUBLANES), n)

    def intra(c):
        rows = chunk_rows(c)
        for h in heads:
            q = q_ref[0, rows, klanes(h)] * qscale
            k = k_ref[0, rows, klanes(h)]
            vc = v_ref[0, rows, vlanes(h)].astype(BF16)
            b = b_ref[h, rows, :]
            btot_f = b[C - 1:C, :DK_C]
            btot_b = b[0:1, DK_C:]
            qe, ke, qb, kl = [], [], [], []
            for bd, btot in ((b[:, :DK_C], btot_f), (b[:, DK_C:], btot_b)):
                mref = 0.5 * btot
                e_half = jnp.exp(mref)
                q_up = q * jnp.exp(bd - mref)
                k_dn = k * jnp.exp(mref - bd)
                qe.append(q_up.astype(BF16))
                ke.append(k_dn.astype(BF16))
                qb.append((q_up * e_half).astype(BF16))
                kl.append((k_dn * e_half).astype(BF16))
            sc = lax.dot_general(jnp.concatenate(qe, axis=0), jnp.concatenate(ke, axis=0), contract_last,
                                 preferred_element_type=F32)
            att = sc[:C] * vis_f + pltpu.roll(sc[C:], C, 1) * vis_b
            acc_ref[rows, vlanes(h)] = _dot(att[:, :C].astype(BF16), vc)
            u_ref[h, c] = lax.dot_general(vc, jnp.concatenate(kl, axis=1), contract_first,
                                          preferred_element_type=F32)
            qb_ref[h, rows, :] = jnp.concatenate(qb, axis=1)
            dec_ref[h, dec_rows(c), :] = jnp.broadcast_to(
                jnp.exp(jnp.concatenate([btot_f, btot_b], axis=1)), (SUBLANES, 2 * DK_C))

    _loop(n_c, intra, static)

    def states(i):
        for h in heads:
            for d, c in ((0, i), (1, n_c - 1 - i)):
                lanes = slice(d * DK_C, (d + 1) * DK_C)
                st = st_ref[h, d]
                sst_ref[h, c, :, lanes] = st.astype(BF16)
                st_ref[h, d] = st * dec_ref[h, dec_rows(c, 1), lanes] + u_ref[h, c, :, lanes]

    _loop(n_c, states, static)

    def inter(c):
        rows = chunk_rows(c)
        for h in heads:
            acc_ref[rows, vlanes(h)] = acc_ref[rows, vlanes(h)] + lax.dot_general(
                qb_ref[h, rows, :], sst_ref[h, c], contract_last, preferred_element_type=F32)

    _loop(n_c, inter, static)

    for h in heads:
        sf_ref[0, 0, h] = st_ref[h, 0].T
        sb_ref[0, 0, h] = st_ref[h, 1].T
        o = acc_ref[:, vlanes(h)]
        o = o * lax.rsqrt(jnp.mean(o * o, axis=-1, keepdims=True) + EPS) * lng_ref[...]
        o_ref[0, :, vlanes(h)] = (o * _silu(g_ref[0, :, vlanes(h)])).astype(o_ref.dtype)


def _gla_mixer(q, k, v, g, gl, gw2, gbias, ln_g, s_init, layer_j, prev_fin=None):
    B, T, _ = q.shape
    C = GLA_CHUNK
    n_c = T // C
    hp = H_C if n_c <= 4 else 2
    idx = np.arange(C)
    tri = np.zeros((2, C, LANES), np.float32)
    tri[0, :, :C] = idx[:, None] >= idx[None, :]
    tri[1, :, :C] = idx[:, None] <= idx[None, :]
    tri = jnp.asarray(tri)
    w2 = jnp.stack([_block_diag2(gw2[0][:, h * DK_C:(h + 1) * DK_C], gw2[1][:, h * DK_C:(h + 1) * DK_C])
                    for h in range(H_C)]).astype(BF16)
    gb = jnp.stack([jnp.concatenate([gbias[0][h * DK_C:(h + 1) * DK_C], gbias[1][h * DK_C:(h + 1) * DK_C]])
                    for h in range(H_C)]).reshape(H_C, 1, 2 * DK_C)
    if s_init is None:
        s0f = s0b = jnp.zeros((1, 1, hp, DK_C, DV_C), F32)
        s_spec = pl.BlockSpec((1, 1, hp, DK_C, DV_C), lambda b, h: (0, 0, 0, 0, 0))
    else:
        s0f, s0b = s_init
        s_spec = pl.BlockSpec((1, 1, hp, DK_C, DV_C), lambda b, h: (b, layer_j, h, 0, 0))
    n_odd = DEPTH // 2
    st_shape = (B, n_odd, H_C, DK_C, DV_C)
    st_out = pl.BlockSpec((1, 1, hp, DK_C, DV_C), lambda b, h: (b, layer_j, h, 0, 0))
    args = [q, k, v, g, gl, w2, gb, ln_g.reshape(1, -1), tri, s0f, s0b]
    if prev_fin is None:
        prev_fin = (jnp.zeros(st_shape, F32), jnp.zeros(st_shape, F32))
    extra_specs = [pl.BlockSpec(memory_space=pl.ANY)] * 2
    aliases = {len(args): 1, len(args) + 1: 2}
    args += list(prev_fin)
    o, sf, sb = pl.pallas_call(
        functools.partial(_gla_kernel, T, hp),
        grid=(B, H_C // hp),
        input_output_aliases=aliases,
        in_specs=[
            pl.BlockSpec((1, T, hp * DK_C), lambda b, h: (b, 0, h)),
            pl.BlockSpec((1, T, hp * DK_C), lambda b, h: (b, 0, h)),
            pl.BlockSpec((1, T, hp * DV_C), lambda b, h: (b, 0, h)),
            pl.BlockSpec((1, T, hp * DV_C), lambda b, h: (b, 0, h)),
            pl.BlockSpec((1, T, 2 * G_LORA), lambda b, h: (b, 0, 0)),
            pl.BlockSpec((hp, 2 * G_LORA, 2 * DK_C), lambda b, h: (h, 0, 0)),
            pl.BlockSpec((hp, 1, 2 * DK_C), lambda b, h: (h, 0, 0)),
            pl.BlockSpec((1, DV_C), lambda b, h: (0, 0)),
            pl.BlockSpec((2, C, LANES), lambda b, h: (0, 0, 0)),
            s_spec, s_spec,
        ] + extra_specs,
        out_specs=[pl.BlockSpec((1, T, hp * DV_C), lambda b, h: (b, 0, h)), st_out, st_out],
        out_shape=[
            jax.ShapeDtypeStruct((B, T, H_C * DV_C), BF16),
            jax.ShapeDtypeStruct(st_shape, F32),
            jax.ShapeDtypeStruct(st_shape, F32),
        ],
        scratch_shapes=[
            pltpu.VMEM((hp, T, 2 * DK_C), F32), pltpu.VMEM((T, hp * DV_C), F32),
            pltpu.VMEM((hp, T, 2 * DK_C), BF16), pltpu.VMEM((hp, n_c * SUBLANES, 2 * DK_C), F32),
            pltpu.VMEM((hp, n_c, DV_C, 2 * DK_C), F32), pltpu.VMEM((hp, n_c, DV_C, 2 * DK_C), BF16),
            pltpu.VMEM((hp, 2, DV_C, DK_C), F32),
        ],
        compiler_params=_params("parallel", "parallel"),
    )(*args)
    return o, (sf, sb)


def kernel(x_prompt, x_sample, c, cache_attn_k, cache_attn_v, state_rwkv_fwd, state_rwkv_bwd, state_gla_fwd, state_gla_bwd, c_ctx, norm_g, mod_w, mod_b, ev_w_in, ev_w_out, ev_qn_g, ev_kn_g, ev_shift_mu, rw_w0, rw_w2, rw_a0, rw_a2, rw_kk, rw_ka, rw_rk, rw_ln_g, rw_ln_b, od_w_in, od_w_out, gla_w2, gla_b, gla_ln_g, final_g):
    n_dec = c.shape[0]
    cond = jnp.concatenate([c_ctx[None], c, jnp.zeros((SUBLANES - 1 - n_dec, D_MODEL), F32)], axis=0)
    mod = _modulation(cond, mod_w, mod_b)

    ev_in = ev_w_in.astype(BF16)
    ev_out = ev_w_out.astype(BF16)
    od_in = od_w_in.astype(BF16)
    od_out = od_w_out.astype(BF16)
    kw = KV_A * HD_A
    ck = cache_attn_k.reshape(cache_attn_k.shape[:3] + (kw,))
    cv = cache_attn_v.reshape(cache_attn_v.shape[:3] + (kw,))

    def trunk(x, latent):
        new = {"k": [], "v": [], "rf": [], "rb": []}
        gla_fin = None
        for i in range(DEPTH):
            j = i // 2
            fg = final_g if i == DEPTH - 1 else None
            if i % 2 == 0:
                qa, ka, va, ga, zb, gb = _inproj(x, norm_g[i], mod, i, ev_in[j], EV_SPLITS, latent)
                if latent:
                    o_a, _ = _attention(qa, ka, va, ga, ev_qn_g[j], ev_kn_g[j], ck, cv, j)
                    s_init = (state_rwkv_fwd[:, j], state_rwkv_bwd[:, j])
                else:
                    o_a, kn = _attention(qa, ka, va, ga, ev_qn_g[j], ev_kn_g[j])
                    s_init = None
                    new["k"].append(kn)
                    new["v"].append(va)
                p = dict(shift_mu=ev_shift_mu[j], w0=rw_w0[j], w2=rw_w2[j], a0=rw_a0[j], a2=rw_a2[j],
                         k_k=rw_kk[j], k_a=rw_ka[j], r_k=rw_rk[j])
                (y_f, y_b, bonus), fin = _rwkv_mixer(zb, p, s_init)
                new["rf"].append(fin[0])
                new["rb"].append(fin[1])
                x = _outproj_even(o_a, y_f, y_b, bonus, gb, rw_ln_g[j], rw_ln_b[j],
                                  ev_out[j, :D_HALF], ev_out[j, D_HALF:], x, mod, i, latent)
            else:
                q, k, v, g, gl = _inproj(x, norm_g[i], mod, i, od_in[j], OD_SPLITS, latent,
                                         dtypes=(F32, F32, BF16, F32, F32))
                s_init = (state_gla_fwd, state_gla_bwd) if latent else None
                o, gla_fin = _gla_mixer(q, k, v, g, gl, gla_w2[j], gla_b[j], gla_ln_g[j], s_init, j, gla_fin)
                x = _outproj([o], [od_out[j]], x, mod, i, latent, fg)
        new["gf"], new["gb"] = gla_fin
        return x, new

    y_prompt, new = trunk(x_prompt, False)
    y_sample, _ = trunk(x_sample, True)
    B, T = x_prompt.shape[:2]
    heads = lambda t: t.reshape(B, T, KV_A, HD_A)
    return (y_prompt, y_sample,
            jnp.stack([heads(t) for t in new["k"]], axis=1), jnp.stack([heads(t) for t in new["v"]], axis=1),
            jnp.stack(new["rf"], axis=1), jnp.stack(new["rb"], axis=1),
            new["gf"], new["gb"])
```

```python
import functools

import numpy as np
import jax
import jax.numpy as jnp
from jax import lax
from jax.experimental import pallas as pl
from jax.experimental.pallas import tpu as pltpu

F32 = jnp.float32
BF16 = jnp.bfloat16

D_MODEL = 1024
DEPTH = 4
GRID_W = 64
D_HALF = D_MODEL // 2
HD_A = 64
H_A = D_HALF // HD_A
KV_A = H_A // 4
ROPE_BASE = 10000.0
HS_B = 64
H_B = D_HALF // HS_B
W_LORA = 64
A_LORA = 64
RWKV_DECAY_SCALE = 0.606531
GN_EPS = 64e-5
B_SHIFT = 3 * D_HALF + 2 * W_LORA + 2 * A_LORA
H_C = 4
DK_C = D_MODEL // 2 // H_C
DV_C = D_MODEL // H_C
G_LORA = 16
GLA_TAU = 16.0
EPS = 1e-6

EV_SPLITS = (H_A * HD_A, KV_A * HD_A, KV_A * HD_A, D_HALF, B_SHIFT, D_HALF)
OD_SPLITS = (H_C * DK_C, H_C * DK_C, D_MODEL, D_MODEL, 2 * G_LORA)

LANES = 128
SUBLANES = 8
VMEM_LIMIT_BYTES = 56 * 1024 * 1024

ROW_TILE = 256
Q_TILE = 256
GLA_CHUNK = 64
SCAN_T = 16


def _params(*sem):
    return pltpu.CompilerParams(dimension_semantics=sem, vmem_limit_bytes=VMEM_LIMIT_BYTES)


def _silu(x):
    return x * jax.nn.sigmoid(x)


def _dot(a, b):
    return jnp.dot(a, b, preferred_element_type=F32)


def _seg_sum(x, ones_bd):
    hi = x.astype(BF16)
    r1 = x - hi.astype(F32)
    mid = r1.astype(BF16)
    lo = (r1 - mid.astype(F32)).astype(BF16)
    return _dot(hi, ones_bd) + _dot(mid, ones_bd) + _dot(lo, ones_bd)


def _get(ref):
    return ref[...].reshape(ref.shape[-2:])


def _put(ref, val):
    ref[...] = val.reshape(ref.shape)


def _block_diag_ones(n, blk):
    i = np.arange(n) // blk
    return jnp.asarray((i[:, None] == i[None, :]).astype(np.float32), dtype=BF16)


def _mod_kernel(cond_ref, w_ref, b_ref, o_ref):
    s = _silu(cond_ref[...])
    o_ref[0] = _dot(s.astype(BF16), w_ref[0].astype(BF16)) + b_ref[0]


def _modulation(cond, mod_w, mod_b):
    n = cond.shape[0]
    return pl.pallas_call(
        _mod_kernel,
        grid=(DEPTH, 3),
        in_specs=[
            pl.BlockSpec((n, D_MODEL), lambda i, j: (0, 0)),
            pl.BlockSpec((1, D_MODEL, D_MODEL), lambda i, j: (i, 0, j)),
            pl.BlockSpec((1, 1, D_MODEL), lambda i, j: (i, 0, j)),
        ],
        out_specs=pl.BlockSpec((1, n, D_MODEL), lambda i, j: (i, 0, j)),
        out_shape=jax.ShapeDtypeStruct((DEPTH, n, 3 * D_MODEL), F32),
        compiler_params=_params("parallel", "parallel"),
    )(cond, mod_w, mod_b.reshape(DEPTH, 1, 3 * D_MODEL))


def _mod_row(latent):
    return (1 + pl.program_id(0)) if latent else 0


def _inproj_kernel(latent, splits, x_ref, g_ref, sh_ref, sc_ref, w_ref, *out_refs):
    r = _mod_row(latent)
    x = x_ref[0]
    y = x * lax.rsqrt(jnp.mean(x * x, axis=-1, keepdims=True) + EPS) * g_ref[...]
    shift = sh_ref[0, pl.ds(r, 1), :]
    scale = sc_ref[0, pl.ds(r, 1), :]
    h = (y * (1.0 + scale) + shift).astype(BF16)
    off = 0
    for o_ref, n in zip(out_refs, splits):
        o_ref[0] = _dot(h, w_ref[:, off:off + n]).astype(o_ref.dtype)
        off += n


def _inproj(x, norm_g, mod, layer, w_bf16, splits, latent, dtypes=None):
    B, T, _ = x.shape
    dtypes = dtypes or (F32,) * len(splits)
    cols = w_bf16.shape[1]
    nrow = mod.shape[1]
    return pl.pallas_call(
        functools.partial(_inproj_kernel, latent, splits),
        grid=(B, T // ROW_TILE),
        in_specs=[
            pl.BlockSpec((1, ROW_TILE, D_MODEL), lambda b, i: (b, i, 0)),
            pl.BlockSpec((1, D_MODEL), lambda b, i: (0, 0)),
            pl.BlockSpec((1, nrow, D_MODEL), lambda b, i: (layer, 0, 0)),
            pl.BlockSpec((1, nrow, D_MODEL), lambda b, i: (layer, 0, 1)),
            pl.BlockSpec((D_MODEL, cols), lambda b, i: (0, 0)),
        ],
        out_specs=[pl.BlockSpec((1, ROW_TILE, n), lambda b, i: (b, i, 0)) for n in splits],
        out_shape=[jax.ShapeDtypeStruct((B, T, n), dt) for n, dt in zip(splits, dtypes)],
        compiler_params=_params("parallel", "parallel"),
    )(x, norm_g.reshape(1, D_MODEL), mod, mod, w_bf16)


def _outproj_kernel(latent, final, n_in, *refs):
    o_refs = refs[:n_in]
    w_refs = refs[n_in:2 * n_in]
    x_ref, gate_ref = refs[2 * n_in], refs[2 * n_in + 1]
    rest = refs[2 * n_in + 2:]
    r = _mod_row(latent)
    acc = _dot(o_refs[0][0].astype(BF16), w_refs[0][...])
    for o_ref, w_ref in zip(o_refs[1:], w_refs[1:]):
        acc = acc + _dot(o_ref[0].astype(BF16), w_ref[...])
    y = x_ref[0] + gate_ref[0, pl.ds(r, 1), :] * acc
    if final:
        fg_ref, out_ref = rest
        y = y * lax.rsqrt(jnp.mean(y * y, axis=-1, keepdims=True) + EPS) * fg_ref[...]
    else:
        (out_ref,) = rest
    out_ref[0] = y


def _outproj(outs, ws_bf16, x, mod, layer, latent, final_g=None):
    B, T, _ = x.shape
    nrow = mod.shape[1]
    n_in = len(outs)
    final = final_g is not None
    in_specs = [pl.BlockSpec((1, ROW_TILE, o.shape[-1]), lambda b, i: (b, i, 0)) for o in outs]
    in_specs += [pl.BlockSpec(w.shape, lambda b, i: (0, 0)) for w in ws_bf16]
    in_specs += [
        pl.BlockSpec((1, ROW_TILE, D_MODEL), lambda b, i: (b, i, 0)),
        pl.BlockSpec((1, nrow, D_MODEL), lambda b, i: (layer, 0, 2)),
    ]
    args = list(outs) + list(ws_bf16) + [x, mod]
    if final:
        in_specs.append(pl.BlockSpec((1, D_MODEL), lambda b, i: (0, 0)))
        args.append(final_g.reshape(1, D_MODEL))
    return pl.pallas_call(
        functools.partial(_outproj_kernel, latent, final, n_in),
        grid=(B, T // ROW_TILE),
        in_specs=in_specs,
        out_specs=pl.BlockSpec((1, ROW_TILE, D_MODEL), lambda b, i: (b, i, 0)),
        out_shape=jax.ShapeDtypeStruct((B, T, D_MODEL), F32),
        compiler_params=_params("parallel", "parallel"),
    )(*args)


def _rope_tables(T):
    n_rows = T // GRID_W
    row = jnp.repeat(jnp.arange(n_rows), GRID_W).astype(F32)
    col = jnp.tile(jnp.arange(GRID_W), n_rows).astype(F32)
    n_freq = HD_A // 4
    inv = ROPE_BASE ** (-jnp.arange(n_freq, dtype=F32) / n_freq)
    ang_r = row[:, None] * inv
    ang_c = col[:, None] * inv
    zero = jnp.zeros_like(ang_r)
    cos = jnp.concatenate([jnp.cos(ang_r), jnp.cos(ang_r), jnp.cos(ang_c), jnp.cos(ang_c)], axis=1)
    s1 = jnp.concatenate([-jnp.sin(ang_r), zero, -jnp.sin(ang_c), zero], axis=1)
    s2 = jnp.concatenate([zero, jnp.sin(ang_r), zero, jnp.sin(ang_c)], axis=1)
    return cos, s1, s2


def _rope(x, cos, s1, s2):
    n = x.shape[-1]
    q = HD_A // 4
    return x * cos + pltpu.roll(x, n - q, 1) * s1 + pltpu.roll(x, q, 1) * s2


def _attn_kernel(latent, S, *refs):
    if latent:
        (q_ref, k_ref, v_ref, ga_ref, qg_ref, kg_ref, bd_ref, cos_ref, s1_ref, s2_ref,
         cosk_ref, s1k_ref, s2k_ref, ck_ref, cv_ref, o_ref, km_ref, vm_ref) = refs
    else:
        (q_ref, k_ref, v_ref, ga_ref, qg_ref, kg_ref, bd_ref, o_ref, kn_ref, km_ref, vm_ref) = refs
    bd = bd_ref[...]
    inv_d = 1.0 / HD_A
    kw = KV_A * HD_A

    @pl.when(pl.program_id(1) == 0)
    def _():
        k = k_ref[0]
        kn = k * lax.rsqrt(_seg_sum(k * k, bd[:kw, :kw]) * inv_d + EPS) * kg_ref[...]
        v = v_ref[0]
        if latent:
            kn = _rope(kn, cosk_ref[...], s1k_ref[...], s2k_ref[...])
            k_all = jnp.concatenate([ck_ref[0, 0], kn], axis=0)
            v_all = jnp.concatenate([cv_ref[0, 0], v], axis=0)
        else:
            kn_ref[0] = kn
            k_all, v_all = kn, v
        lane = lax.broadcasted_iota(jnp.int32, (S, kw), 1)
        k_sw = pltpu.roll(k_all, HD_A, 1)
        v_sw = pltpu.roll(v_all, HD_A, 1)
        for j in range(KV_A):
            for half in range(2):
                keep = (lane < HD_A) if half == 0 else (lane >= HD_A)
                src_k, src_v = (k_all, v_all) if j == half else (k_sw, v_sw)
                km_ref[2 * j + half] = jnp.where(keep, src_k, 0.0).astype(BF16)
                vm_ref[2 * j + half] = jnp.where(keep, src_v, 0.0).astype(BF16)

    q = q_ref[0]
    qn = q * lax.rsqrt(_seg_sum(q * q, bd) * inv_d + EPS) * qg_ref[...]
    if latent:
        qn = _rope(qn, cos_ref[...], s1_ref[...], s2_ref[...])
    qb = (qn * HD_A ** -0.5).astype(BF16)
    for m in range(H_A // 2):
        blk = slice(m * LANES, (m + 1) * LANES)
        qs = qb[:, blk]
        acc = None
        for half in range(2):
            j = (2 * m + half) // (H_A // KV_A)
            s = lax.dot_general(qs, km_ref[2 * j + half], (((1,), (1,)), ((), ())),
                                preferred_element_type=F32)
            e = jnp.exp(s - jnp.max(s, axis=-1, keepdims=True))
            l = jnp.sum(e, axis=-1, keepdims=True)
            pv = _dot(e.astype(BF16), vm_ref[2 * j + half]) * (1.0 / l)
            acc = pv if acc is None else acc + pv
        o_ref[0, :, blk] = (acc * _silu(ga_ref[0, :, blk])).astype(o_ref.dtype)


def _attention(qa, ka, va, ga, qn_g, kn_g, ctx_k=None, ctx_v=None, layer_j=0):
    B, T, _ = qa.shape
    latent = ctx_k is not None
    S = T + (ctx_k.shape[2] if latent else 0)
    kw = KV_A * HD_A
    qw = H_A * HD_A
    bd = _block_diag_ones(qw, HD_A)
    qblk = pl.BlockSpec((1, Q_TILE, qw), lambda b, i: (b, i, 0))
    kblk = pl.BlockSpec((1, T, kw), lambda b, i: (b, 0, 0))
    in_specs = [qblk, kblk, kblk, qblk,
                pl.BlockSpec((1, qw), lambda b, i: (0, 0)),
                pl.BlockSpec((1, kw), lambda b, i: (0, 0)),
                pl.BlockSpec(bd.shape, lambda b, i: (0, 0))]
    args = [qa, ka, va, ga, jnp.tile(qn_g, H_A).reshape(1, -1), jnp.tile(kn_g, KV_A).reshape(1, -1), bd]
    out_specs = [qblk]
    out_shape = [jax.ShapeDtypeStruct((B, T, qw), BF16)]
    if latent:
        tabs = _rope_tables(T)
        P = ctx_k.shape[2]
        in_specs += [pl.BlockSpec((Q_TILE, qw), lambda b, i: (i, 0))] * 3
        in_specs += [pl.BlockSpec((T, kw), lambda b, i: (0, 0))] * 3
        in_specs += [pl.BlockSpec((1, 1, P, kw), lambda b, i: (b, layer_j, 0, 0))] * 2
        args += [jnp.tile(t, (1, H_A)) for t in tabs] + [jnp.tile(t, (1, KV_A)) for t in tabs] + [ctx_k, ctx_v]
    else:
        out_specs.append(kblk)
        out_shape.append(jax.ShapeDtypeStruct((B, T, kw), F32))
    res = pl.pallas_call(
        functools.partial(_attn_kernel, latent, S),
        grid=(B, T // Q_TILE),
        in_specs=in_specs,
        out_specs=out_specs,
        out_shape=out_shape,
        scratch_shapes=[pltpu.VMEM((2 * KV_A, S, kw), BF16), pltpu.VMEM((2 * KV_A, S, kw), BF16)],
        compiler_params=_params("parallel", "arbitrary"),
    )(*args)
    return res if not latent else (res[0], None)


def _rwkv_prep_kernel(n_t, z_ref, zp_ref, zn_ref, mu_ref, w0_ref, a0_ref, w2_ref, a2_ref,
                      kkg_ref, ka_ref, rk_ref, bd_ref,
                      wf_ref, wb_ref, kdf_ref, kdb_ref, kaf_ref, kab_ref, nkk_ref, r_ref, v_ref, bonus_ref):
    i = pl.program_id(1)
    z = z_ref[0]
    n = z.shape[0]
    prev_row = jnp.where(i > 0, zp_ref[0, SUBLANES - 1:SUBLANES, :], 0.0)
    next_row = jnp.where(i < n_t - 1, zn_ref[0, 0:1, :], 0.0)
    rows = lax.broadcasted_iota(jnp.int32, (n, 1), 0)
    zp = jnp.where(rows == 0, prev_row, pltpu.roll(z, 1, 0))
    zn = jnp.where(rows == n - 1, next_row, pltpu.roll(z, n - 1, 0))
    zs = z + mu_ref[...] * (0.5 * (zp + zn) - z)

    rb = zs[:, 0:D_HALF]
    kb = zs[:, D_HALF:2 * D_HALF]
    vb = zs[:, 2 * D_HALF:3 * D_HALF]
    lw = zs[:, 3 * D_HALF:3 * D_HALF + 2 * W_LORA]
    la = zs[:, 3 * D_HALF + 2 * W_LORA:]
    w = jnp.exp(-RWKV_DECAY_SCALE * jax.nn.sigmoid(w0_ref[...] + _dot(jnp.tanh(lw).astype(BF16), w2_ref[...])))
    a = jax.nn.sigmoid(a0_ref[...] + _dot(la.astype(BF16), a2_ref[...]))
    bd = bd_ref[...]
    kk = kb * kkg_ref[...]
    kk = kk * lax.rsqrt(_seg_sum(kk * kk, bd) + 1e-12)
    ka = ka_ref[...]
    a_f, a_b = a[:, :D_HALF], a[:, D_HALF:]
    kd_f = kb * (1.0 + (a_f - 1.0) * ka)
    kd_b = kb * (1.0 + (a_b - 1.0) * ka)
    _put(wf_ref, w[:, :D_HALF])
    _put(wb_ref, w[:, D_HALF:])
    _put(kdf_ref, kd_f)
    _put(kdb_ref, kd_b)
    _put(kaf_ref, kk * a_f)
    _put(kab_ref, kk * a_b)
    _put(nkk_ref, -kk)
    _put(r_ref, rb)
    _put(v_ref, vb)
    _put(bonus_ref, _seg_sum(rb * rk_ref[...] * (kd_f + kd_b), bd) * vb)


def _block_diag2(m0, m1):
    z = jnp.zeros_like(m0)
    return jnp.concatenate([jnp.concatenate([m0, z], axis=1), jnp.concatenate([z, m1], axis=1)], axis=0)


def _rwkv_prep(zb, shift_mu, w0, w2, a0, a2, k_k, k_a, r_k):
    B, T, _ = zb.shape
    n_t = T // ROW_TILE
    per_tile = ROW_TILE // SUBLANES
    bd = _block_diag_ones(D_HALF, HS_B)
    row = lambda x: x.reshape(1, -1)
    vec = pl.BlockSpec((1, D_HALF), lambda b, i: (0, 0))
    vec2 = pl.BlockSpec((1, 2 * D_HALF), lambda b, i: (0, 0))
    out = pl.BlockSpec((1, ROW_TILE, D_HALF), lambda b, i: (b, i, 0))
    return pl.pallas_call(
        functools.partial(_rwkv_prep_kernel, n_t),
        grid=(B, n_t),
        in_specs=[
            pl.BlockSpec((1, ROW_TILE, B_SHIFT), lambda b, i: (b, i, 0)),
            pl.BlockSpec((1, SUBLANES, B_SHIFT), lambda b, i: (b, jnp.maximum(i * per_tile - 1, 0), 0)),
            pl.BlockSpec((1, SUBLANES, B_SHIFT),
                         lambda b, i: (b, jnp.minimum((i + 1) * per_tile, n_t * per_tile - 1), 0)),
            pl.BlockSpec((1, B_SHIFT), lambda b, i: (0, 0)),
            vec2, vec2,
            pl.BlockSpec((2 * W_LORA, 2 * D_HALF), lambda b, i: (0, 0)),
            pl.BlockSpec((2 * A_LORA, 2 * D_HALF), lambda b, i: (0, 0)),
            vec, vec, vec,
            pl.BlockSpec(bd.shape, lambda b, i: (0, 0)),
        ],
        out_specs=[out] * 10,
        out_shape=[jax.ShapeDtypeStruct((B, T, D_HALF), F32)] * 10,
        compiler_params=_params("parallel", "parallel"),
    )(zb, zb, zb, row(shift_mu), row(w0), row(a0),
      _block_diag2(w2[0], w2[1]).astype(BF16), _block_diag2(a2[0], a2[1]).astype(BF16),
      row(k_k), row(k_a), row(r_k), bd)


CHAIN_ROWS = 32
HEAD_PAIRS = H_B // 2


def _step_rows(ref, t, vs):
    if vs == 1:
        return ref[:, t, :]
    return jnp.concatenate([jnp.broadcast_to(ref[b, t:t + 1, :], (vs, ref.shape[2])) for b in range(ref.shape[0])],
                           axis=0)


def _to_chains(x):
    return jnp.concatenate([x[:, j * LANES:(j + 1) * LANES] for j in range(HEAD_PAIRS)], axis=0).T


def _rwkv_scan_kernel(reverses, vs, *refs):
    nd = len(reverses)
    op_refs = [refs[6 * d:6 * d + 6] for d in range(nd)]
    s0_refs = refs[6 * nd:7 * nd]
    y_refs = refs[7 * nd:8 * nd]
    s_refs = refs[8 * nd:9 * nd]
    kbuf, vbuf, ybuf = refs[9 * nd:]
    vh = HS_B // vs
    hoist = vh > SUBLANES
    groups = 2 if vh >= 2 * SUBLANES else 1

    @pl.when(pl.program_id(0) == 0)
    def _():
        for d in range(nd):
            s_refs[d][...] = s0_refs[d][...]

    split = lax.broadcasted_iota(jnp.int32, (vh, LANES), 1) % vs
    W_, NKK_, KKA_, KD_, R_, V_ = range(6)

    def relayout(d, t):
        for a, ref in enumerate(op_refs[d]):
            kbuf[d, t, a] = _to_chains(_step_rows(ref, t, vs))

    def key_row(d, t, a, par, k):
        return kbuf[d, t, a, pl.ds(par * HS_B + k, 1), :]

    def tree_sum(parts):
        return (parts[0] + parts[1]) + (parts[2] + parts[3])

    def update_groups(d, t, par, gs, yslot):
        s_ref = s_refs[d]
        lanes = slice(par * LANES, (par + 1) * LANES)
        rows = [pl.ds(pl.multiple_of(g * SUBLANES, SUBLANES), SUBLANES) for g in gs]
        acc = [[None] * 4 for _ in gs]
        for k in range(HS_B):
            nkk = key_row(d, t, NKK_, par, k)
            for i, g in enumerate(gs):
                p = s_ref[g, k, :, lanes] * nkk
                acc[i][k % 4] = p if acc[i][k % 4] is None else acc[i][k % 4] + p
        sa = [tree_sum(a) for a in acc]
        vb = [vbuf[d, par, rw, :] for rw in rows]
        yacc = [[None] * 4 for _ in gs]
        for k in range(HS_B):
            w, kka, kd, r = (key_row(d, t, a, par, k) for a in (W_, KKA_, KD_, R_))
            for i, g in enumerate(gs):
                s = s_ref[g, k, :, lanes] * w + sa[i] * kka + vb[i] * kd
                s_ref[g, k, :, lanes] = s
                p = s * r
                yacc[i][k % 4] = p if yacc[i][k % 4] is None else yacc[i][k % 4] + p
        for i, rw in enumerate(rows):
            ybuf[d, yslot, par, rw, :] = tree_sum(yacc[i])

    def update_rows(d, t, par, yslot):
        s_ref = s_refs[d]
        lanes = slice(par * LANES, (par + 1) * LANES)
        w, nkk, kka, kd, r = (kbuf[d, t, a, par * HS_B:(par + 1) * HS_B, :] for a in (W_, NKK_, KKA_, KD_, R_))
        vblk = vbuf[d, par]
        ys = []
        for u in range(SUBLANES):
            s = s_ref[u, :, lanes]
            sa = jnp.sum(s * nkk, axis=0, keepdims=True)
            s = s * w + sa * kka + vblk[u:u + 1, :] * kd
            s_ref[u, :, lanes] = s
            ys.append(jnp.sum(s * r, axis=0, keepdims=True))
        ybuf[d, yslot, par] = jnp.concatenate(ys, axis=0)

    def step(d, t):
        yslot = t if hoist else 0
        for par in range(2):
            v_par = kbuf[d, t, V_, par * HS_B:(par + 1) * HS_B, :]
            v_own = v_par[:vh]
            for q in range(1, vs):
                v_own = jnp.where(split == q, v_par[q * vh:(q + 1) * vh], v_own)
            vbuf[d, par] = v_own
            if not hoist:
                update_rows(d, t, par, yslot)
            else:
                def body(g, c, par=par):
                    update_groups(d, t, par, [g * groups + i for i in range(groups)], yslot)
                    return c

                lax.fori_loop(0, vh // (groups * SUBLANES), body, 0)

    def emit_y(d, t):
        halves = []
        for par in range(2):
            y = ybuf[d, t if hoist else 0, par]
            if vs > 1:
                y = jnp.concatenate([jnp.where(split == q, y, 0.0) for q in range(vs)], axis=0)
            halves.append(y)
        yt = jnp.concatenate(halves, axis=0).T
        for j in range(HEAD_PAIRS):
            blk = yt[j * CHAIN_ROWS:(j + 1) * CHAIN_ROWS]
            if vs > 1:
                blk = jnp.sum(blk.reshape(CHAIN_ROWS // vs, vs, LANES), axis=1)
            y_refs[d][:, t, j * LANES:(j + 1) * LANES] = blk

    order = [[(SCAN_T - 1 - i) if rev else i for i in range(SCAN_T)] for rev in reverses]
    if hoist:
        for d in range(nd):
            for t in order[d]:
                relayout(d, t)
        for d in range(nd):
            for t in order[d]:
                step(d, t)
        for d in range(nd):
            for t in order[d]:
                emit_y(d, t)
    else:
        for i in range(SCAN_T):
            for d in range(nd):
                t = order[d][i]
                relayout(d, t)
                step(d, t)
                emit_y(d, t)


def _rwkv_scan(ops, s0s, reverses):
    B, T, W = ops[0][0].shape
    nd = len(reverses)
    vs = CHAIN_ROWS // B
    vh = HS_B // vs
    n_t = T // SCAN_T
    s0 = s0s[0]
    vecs = [pl.BlockSpec((B, SCAN_T, W), (lambda t: (0, n_t - 1 - t, 0)) if rev else (lambda t: (0, t, 0)))
            for rev in reverses]
    state = pl.BlockSpec(s0.shape, lambda t: (0,) * s0.ndim)
    res = pl.pallas_call(
        functools.partial(_rwkv_scan_kernel, tuple(reverses), vs),
        grid=(n_t,),
        in_specs=[vecs[d] for d in range(nd) for _ in range(6)] + [state] * nd,
        out_specs=vecs + [state] * nd,
        out_shape=[jax.ShapeDtypeStruct((B, T, W), F32)] * nd + [jax.ShapeDtypeStruct(s0.shape, F32)] * nd,
        scratch_shapes=[pltpu.VMEM((nd, SCAN_T, 6, LANES, LANES), F32), pltpu.VMEM((nd, 2, vh, LANES), F32),
                        pltpu.VMEM((nd, SCAN_T if vh > SUBLANES else 1, 2, vh, LANES), F32)],
        compiler_params=_params("arbitrary"),
    )(*[x for d in range(nd) for x in ops[d]], *s0s)
    return res[:nd], res[nd:]


def _outproj_even_kernel(latent, oa_ref, yf_ref, yb_ref, bonus_ref, gb_ref, lng_ref, lnb_ref, bd_ref,
                         wa_ref, wb_ref, x_ref, gate_ref, out_ref):
    r = _mod_row(latent)
    bd = bd_ref[...]
    y = yf_ref[0] + yb_ref[0]
    inv_n = 1.0 / HS_B
    d = y - _seg_sum(y, bd) * inv_n
    var = _seg_sum(d * d, bd) * inv_n
    yn = d * lax.rsqrt(var + GN_EPS) * lng_ref[...] + lnb_ref[...]
    o_b = ((yn + bonus_ref[0]) * _silu(gb_ref[0])).astype(BF16)
    acc = _dot(oa_ref[0], wa_ref[...]) + _dot(o_b, wb_ref[...])
    out_ref[0] = x_ref[0] + gate_ref[0, pl.ds(r, 1), :] * acc


def _outproj_even(o_a, y_f, y_b, bonus, gb, ln_g, ln_b, w_a, w_b, x, mod, layer, latent):
    B, T, _ = x.shape
    nrow = mod.shape[1]
    bd = _block_diag_ones(D_HALF, HS_B)
    half = pl.BlockSpec((1, ROW_TILE, D_HALF), lambda b, i: (b, i, 0))
    full = pl.BlockSpec((1, ROW_TILE, D_MODEL), lambda b, i: (b, i, 0))
    vec = pl.BlockSpec((1, D_HALF), lambda b, i: (0, 0))
    const = lambda a: pl.BlockSpec(a.shape, lambda b, i: (0, 0))
    return pl.pallas_call(
        functools.partial(_outproj_even_kernel, latent),
        grid=(B, T // ROW_TILE),
        in_specs=[half, half, half, half, half, vec, vec, const(bd), const(w_a), const(w_b), full,
                  pl.BlockSpec((1, nrow, D_MODEL), lambda b, i: (layer, 0, 2))],
        out_specs=full,
        out_shape=jax.ShapeDtypeStruct((B, T, D_MODEL), F32),
        compiler_params=_params("parallel", "parallel"),
    )(o_a, y_f, y_b, bonus, gb, ln_g.reshape(1, -1), ln_b.reshape(1, -1), bd, w_a, w_b, x, mod)


def _rwkv_mixer(zb, p, s_init):
    B, T, _ = zb.shape
    (w_f, w_b, kd_f, kd_b, ka_f, ka_b, nkk, r, v, bonus) = _rwkv_prep(
        zb, p["shift_mu"], p["w0"], p["w2"], p["a0"], p["a2"], p["k_k"], p["k_a"], p["r_k"])
    vs = CHAIN_ROWS // B
    vh = HS_B // vs
    chains = H_B * B * vs
    s6 = (B, HEAD_PAIRS, 2, vs, vh // SUBLANES, SUBLANES, HS_B)
    if vh > SUBLANES:
        perm, shape = (4, 6, 5, 2, 1, 0, 3), (vh // SUBLANES, HS_B, SUBLANES, chains)
    else:
        perm, shape = (4, 5, 6, 2, 1, 0, 3), (vh, HS_B, chains)
    inv = tuple(int(i) for i in np.argsort(perm))
    ops = [(w_f, nkk, ka_f, kd_f, r, v), (w_b, nkk, ka_b, kd_b, r, v)]
    s0s = [jnp.zeros(shape, F32) if s_init is None else s_init[d].reshape(s6).transpose(perm).reshape(shape)
           for d in range(2)]
    if vh > SUBLANES:
        res = [_rwkv_scan([ops[d]], [s0s[d]], [d == 1]) for d in range(2)]
        ys, s_fins = [r_[0][0] for r_ in res], [r_[1][0] for r_ in res]
    else:
        ys, s_fins = _rwkv_scan(ops, s0s, [False, True])
    fin = [s_.reshape([s6[i] for i in perm]).transpose(inv).reshape(B, H_B, HS_B, HS_B) for s_ in s_fins]
    return (ys[0], ys[1], bonus), fin


def _chunk_cumsum(x, chunk, suffix):
    T = x.shape[0]
    pos = lax.broadcasted_iota(jnp.int32, (T, 1), 0) % chunk
    step = 1
    while step < chunk:
        if suffix:
            x = x + jnp.where(pos < chunk - step, pltpu.roll(x, T - step, 0), 0.0)
        else:
            x = x + jnp.where(pos >= step, pltpu.roll(x, step, 0), 0.0)
        step *= 2
    return x


def _loop(n, body, static):
    if static:
        for i in range(n):
            body(i)
    else:
        lax.fori_loop(0, n, lambda i, c: (body(i), c)[1], 0, unroll=4)


def _gla_kernel(T, hp, q_ref, k_ref, v_ref, g_ref, gl_ref, w2_ref, gb_ref, lng_ref, tri_ref, s0f_ref, s0b_ref,
                *rest):
    o_ref, sf_ref, sb_ref, b_ref, acc_ref, qb_ref, dec_ref, u_ref, sst_ref, st_ref = rest[-10:]
    C = GLA_CHUNK
    n_c = T // C
    static = n_c <= 4
    qscale = DK_C ** -0.5
    contract_last = (((1,), (1,)), ((), ()))
    contract_first = (((0,), (0,)), ((), ()))
    vis_f = tri_ref[0]
    vis_b = tri_ref[1]
    heads = range(hp)
    klanes = lambda h: slice(h * DK_C, (h + 1) * DK_C)
    vlanes = lambda h: slice(h * DV_C, (h + 1) * DV_C)

    for h in heads:
        pre = _dot(gl_ref[0].astype(BF16), w2_ref[h]) + gb_ref[h]
        la = jax.nn.log_sigmoid(pre) * (1.0 / GLA_TAU)
        b_ref[h, :, :DK_C] = _chunk_cumsum(la[:, :DK_C], C, suffix=False)
        b_ref[h, :, DK_C:] = _chunk_cumsum(la[:, DK_C:], C, suffix=True)
        st_ref[h, 0] = s0f_ref[0, 0, h].T
        st_ref[h, 1] = s0b_ref[0, 0, h].T

    def chunk_rows(c):
        return pl.ds(c * C, C) if static else pl.ds(pl.multiple_of(c * C, C), C)

    def dec_rows(c, n=SUBLANES):
        return pl.ds(c * SUBLANES, n) if static else pl.ds(pl.multiple_of(c * SUBLANES, SUBLANES), n)

    def intra(c):
        rows = chunk_rows(c)
        for h in heads:
            q = q_ref[0, rows, klanes(h)] * qscale
            k = k_ref[0, rows, klanes(h)]
            vc = v_ref[0, rows, vlanes(h)].astype(BF16)
            b = b_ref[h, rows, :]
            btot_f = b[C - 1:C, :DK_C]
            btot_b = b[0:1, DK_C:]
            qe, ke, qb, kl = [], [], [], []
            for bd, btot in ((b[:, :DK_C], btot_f), (b[:, DK_C:], btot_b)):
                mref = 0.5 * btot
                e_half = jnp.exp(mref)
                q_up = q * jnp.exp(bd - mref)
                k_dn = k * jnp.exp(mref - bd)
                qe.append(q_up.astype(BF16))
                ke.append(k_dn.astype(BF16))
                qb.append((q_up * e_half).astype(BF16))
                kl.append((k_dn * e_half).astype(BF16))
            sc = lax.dot_general(jnp.concatenate(qe, axis=0), jnp.concatenate(ke, axis=0), contract_last,
                                 preferred_element_type=F32)
            att = sc[:C] * vis_f + pltpu.roll(sc[C:], C, 1) * vis_b
            acc_ref[rows, vlanes(h)] = _dot(att[:, :C].astype(BF16), vc)
            u_ref[h, c] = lax.dot_general(vc, jnp.concatenate(kl, axis=1), contract_first,
                                          preferred_element_type=F32)
            qb_ref[h, rows, :] = jnp.concatenate(qb, axis=1)
            dec_ref[h, dec_rows(c), :] = jnp.broadcast_to(
                jnp.exp(jnp.concatenate([btot_f, btot_b], axis=1)), (SUBLANES, 2 * DK_C))

    _loop(n_c, intra, static)

    def states(i):
        for h in heads:
            for d, c in ((0, i), (1, n_c - 1 - i)):
                lanes = slice(d * DK_C, (d + 1) * DK_C)
                st = st_ref[h, d]
                sst_ref[h, c, :, lanes] = st.astype(BF16)
                st_ref[h, d] = st * dec_ref[h, dec_rows(c, 1), lanes] + u_ref[h, c, :, lanes]

    _loop(n_c, states, static)

    def inter(c):
        rows = chunk_rows(c)
        for h in heads:
            acc_ref[rows, vlanes(h)] = acc_ref[rows, vlanes(h)] + lax.dot_general(
                qb_ref[h, rows, :], sst_ref[h, c], contract_last, preferred_element_type=F32)

    _loop(n_c, inter, static)

    for h in heads:
        sf_ref[0, 0, h] = st_ref[h, 0].T
        sb_ref[0, 0, h] = st_ref[h, 1].T
        o = acc_ref[:, vlanes(h)]
        o = o * lax.rsqrt(jnp.mean(o * o, axis=-1, keepdims=True) + EPS) * lng_ref[...]
        o_ref[0, :, vlanes(h)] = (o * _silu(g_ref[0, :, vlanes(h)])).astype(o_ref.dtype)


def _gla_mixer(q, k, v, g, gl, gw2, gbias, ln_g, s_init, layer_j, prev_fin=None):
    B, T, _ = q.shape
    C = GLA_CHUNK
    n_c = T // C
    hp = H_C if n_c <= 4 else 2
    idx = np.arange(C)
    tri = np.zeros((2, C, LANES), np.float32)
    tri[0, :, :C] = idx[:, None] >= idx[None, :]
    tri[1, :, :C] = idx[:, None] <= idx[None, :]
    tri = jnp.asarray(tri)
    w2 = jnp.stack([_block_diag2(gw2[0][:, h * DK_C:(h + 1) * DK_C], gw2[1][:, h * DK_C:(h + 1) * DK_C])
                    for h in range(H_C)]).astype(BF16)
    gb = jnp.stack([jnp.concatenate([gbias[0][h * DK_C:(h + 1) * DK_C], gbias[1][h * DK_C:(h + 1) * DK_C]])
                    for h in range(H_C)]).reshape(H_C, 1, 2 * DK_C)
    if s_init is None:
        s0f = s0b = jnp.zeros((1, 1, hp, DK_C, DV_C), F32)
        s_spec = pl.BlockSpec((1, 1, hp, DK_C, DV_C), lambda b, h: (0, 0, 0, 0, 0))
    else:
        s0f, s0b = s_init
        s_spec = pl.BlockSpec((1, 1, hp, DK_C, DV_C), lambda b, h: (b, layer_j, h, 0, 0))
    n_odd = DEPTH // 2
    st_shape = (B, n_odd, H_C, DK_C, DV_C)
    st_out = pl.BlockSpec((1, 1, hp, DK_C, DV_C), lambda b, h: (b, layer_j, h, 0, 0))
    args = [q, k, v, g, gl, w2, gb, ln_g.reshape(1, -1), tri, s0f, s0b]
    if prev_fin is None:
        prev_fin = (jnp.zeros(st_shape, F32), jnp.zeros(st_shape, F32))
    extra_specs = [pl.BlockSpec(memory_space=pl.ANY)] * 2
    aliases = {len(args): 1, len(args) + 1: 2}
    args += list(prev_fin)
    o, sf, sb = pl.pallas_call(
        functools.partial(_gla_kernel, T, hp),
        grid=(B, H_C // hp),
        input_output_aliases=aliases,
        in_specs=[
            pl.BlockSpec((1, T, hp * DK_C), lambda b, h: (b, 0, h)),
            pl.BlockSpec((1, T, hp * DK_C), lambda b, h: (b, 0, h)),
            pl.BlockSpec((1, T, hp * DV_C), lambda b, h: (b, 0, h)),
            pl.BlockSpec((1, T, hp * DV_C), lambda b, h: (b, 0, h)),
            pl.BlockSpec((1, T, 2 * G_LORA), lambda b, h: (b, 0, 0)),
            pl.BlockSpec((hp, 2 * G_LORA, 2 * DK_C), lambda b, h: (h, 0, 0)),
            pl.BlockSpec((hp, 1, 2 * DK_C), lambda b, h: (h, 0, 0)),
            pl.BlockSpec((1, DV_C), lambda b, h: (0, 0)),
            pl.BlockSpec((2, C, LANES), lambda b, h: (0, 0, 0)),
            s_spec, s_spec,
        ] + extra_specs,
        out_specs=[pl.BlockSpec((1, T, hp * DV_C), lambda b, h: (b, 0, h)), st_out, st_out],
        out_shape=[
            jax.ShapeDtypeStruct((B, T, H_C * DV_C), BF16),
            jax.ShapeDtypeStruct(st_shape, F32),
            jax.ShapeDtypeStruct(st_shape, F32),
        ],
        scratch_shapes=[
            pltpu.VMEM((hp, T, 2 * DK_C), F32), pltpu.VMEM((T, hp * DV_C), F32),
            pltpu.VMEM((hp, T, 2 * DK_C), BF16), pltpu.VMEM((hp, n_c * SUBLANES, 2 * DK_C), F32),
            pltpu.VMEM((hp, n_c, DV_C, 2 * DK_C), F32), pltpu.VMEM((hp, n_c, DV_C, 2 * DK_C), BF16),
            pltpu.VMEM((hp, 2, DV_C, DK_C), F32),
        ],
        compiler_params=_params("parallel", "parallel"),
    )(*args)
    return o, (sf, sb)


def kernel(x_prompt, x_sample, c, cache_attn_k, cache_attn_v, state_rwkv_fwd, state_rwkv_bwd, state_gla_fwd, state_gla_bwd, c_ctx, norm_g, mod_w, mod_b, ev_w_in, ev_w_out, ev_qn_g, ev_kn_g, ev_shift_mu, rw_w0, rw_w2, rw_a0, rw_a2, rw_kk, rw_ka, rw_rk, rw_ln_g, rw_ln_b, od_w_in, od_w_out, gla_w2, gla_b, gla_ln_g, final_g):
    n_dec = c.shape[0]
    cond = jnp.concatenate([c_ctx[None], c, jnp.zeros((SUBLANES - 1 - n_dec, D_MODEL), F32)], axis=0)
    mod = _modulation(cond, mod_w, mod_b)

    ev_in = ev_w_in.astype(BF16)
    ev_out = ev_w_out.astype(BF16)
    od_in = od_w_in.astype(BF16)
    od_out = od_w_out.astype(BF16)
    kw = KV_A * HD_A
    ck = cache_attn_k.reshape(cache_attn_k.shape[:3] + (kw,))
    cv = cache_attn_v.reshape(cache_attn_v.shape[:3] + (kw,))

    def trunk(x, latent):
        new = {"k": [], "v": [], "rf": [], "rb": []}
        gla_fin = None
        for i in range(DEPTH):
            j = i // 2
            fg = final_g if i == DEPTH - 1 else None
            if i % 2 == 0:
                qa, ka, va, ga, zb, gb = _inproj(x, norm_g[i], mod, i, ev_in[j], EV_SPLITS, latent)
                if latent:
                    o_a, _ = _attention(qa, ka, va, ga, ev_qn_g[j], ev_kn_g[j], ck, cv, j)
                    s_init = (state_rwkv_fwd[:, j], state_rwkv_bwd[:, j])
                else:
                    o_a, kn = _attention(qa, ka, va, ga, ev_qn_g[j], ev_kn_g[j])
                    s_init = None
                    new["k"].append(kn)
                    new["v"].append(va)
                p = dict(shift_mu=ev_shift_mu[j], w0=rw_w0[j], w2=rw_w2[j], a0=rw_a0[j], a2=rw_a2[j],
                         k_k=rw_kk[j], k_a=rw_ka[j], r_k=rw_rk[j])
                (y_f, y_b, bonus), fin = _rwkv_mixer(zb, p, s_init)
                new["rf"].append(fin[0])
                new["rb"].append(fin[1])
                x = _outproj_even(o_a, y_f, y_b, bonus, gb, rw_ln_g[j], rw_ln_b[j],
                                  ev_out[j, :D_HALF], ev_out[j, D_HALF:], x, mod, i, latent)
            else:
                q, k, v, g, gl = _inproj(x, norm_g[i], mod, i, od_in[j], OD_SPLITS, latent,
                                         dtypes=(F32, F32, BF16, F32, F32))
                s_init = (state_gla_fwd, state_gla_bwd) if latent else None
                o, gla_fin = _gla_mixer(q, k, v, g, gl, gla_w2[j], gla_b[j], gla_ln_g[j], s_init, j, gla_fin)
                x = _outproj([o], [od_out[j]], x, mod, i, latent, fg)
        new["gf"], new["gb"] = gla_fin
        return x, new

    y_prompt, new = trunk(x_prompt, False)
    y_sample, _ = trunk(x_sample, True)
    B, T = x_prompt.shape[:2]
    heads = lambda t: t.reshape(B, T, KV_A, HD_A)
    return (y_prompt, y_sample,
            jnp.stack([heads(t) for t in new["k"]], axis=1), jnp.stack([heads(t) for t in new["v"]], axis=1),
            jnp.stack(new["rf"], axis=1), jnp.stack(new["rb"], axis=1),
            new["gf"], new["gb"])
```
